```python
import math
import jax, jax.numpy as jnp
from jax import lax
import numpy as np

D_MODEL = 1024
BATCH = 2
SEQ = 8192
DEPTH = 2

GRID_W = 64
CTX_LEN = 256
N_MIXERS = 2
EXPAND = 2
E_HY = EXPAND * D_MODEL
FILTER_EMB = 33
FILTER_WIDTH = 64
DECAY_TARGET = 1e-2
FAST_DECAY_PCT = 0.3
SLOW_DECAY_PCT = 1.5
MAX_DECAY = math.log(DECAY_TARGET) / FAST_DECAY_PCT
MIN_DECAY = math.log(DECAY_TARGET) / SLOW_DECAY_PCT
HEAD_DIM = 128
N_HEADS = (EXPAND * D_MODEL) // HEAD_DIM
N_KV = N_HEADS // 4
GROUP = N_HEADS // N_KV
QD = N_HEADS * HEAD_DIM
KVD = N_KV * HEAD_DIM
ROPE_AXIS_DIM = HEAD_DIM // 2
ROPE_THETA = 10000.0
Q_BLOCK = 128
EPS = 1e-6

kernel_name = "hybrid_hyena_gqa_prefix_dit"


def rms_norm(x, g):
    xf = x.astype(jnp.float32)
    y = xf * lax.rsqrt(jnp.mean(xf * xf, axis=-1, keepdims=True) + EPS)
    return (y * g.astype(jnp.float32)).astype(x.dtype)


def short_conv(x, w, b):
    xp = jnp.pad(x, ((0, 0), (1, 1), (0, 0)))
    return xp[:, :-2] * w[0] + xp[:, 1:-1] * w[1] + xp[:, 2:] * w[2] + b


def hyena_filter(L, w1, b1, w2, b2, w3, b3, w4, freq):
    f32 = jnp.float32
    t = jnp.linspace(0.0, 1.0, L, dtype=f32)[:, None]
    bands = (FILTER_EMB - 1) // 2
    w = 2.0 * math.pi * jnp.arange(L, dtype=f32)[:, None] / L
    fr = jnp.linspace(1e-4, bands - 1, bands, dtype=f32)[None, :]
    z = jnp.concatenate([t, jnp.cos(fr * w), -jnp.sin(fr * w)], axis=-1)
    fq = freq.astype(f32)
    h = jnp.sin(fq * (z @ w1.astype(f32) + b1.astype(f32)))
    h = jnp.sin(fq * (h @ w2.astype(f32) + b2.astype(f32)))
    h = jnp.sin(fq * (h @ w3.astype(f32) + b3.astype(f32)))
    h = h @ w4.astype(f32)
    deltas = jnp.abs(jnp.linspace(MIN_DECAY, MAX_DECAY, E_HY, dtype=f32))
    decay = jnp.exp(-t * deltas[None, :])
    hf = h[:, :E_HY] * decay
    hb = h[:, E_HY:] * decay
    k = jnp.concatenate([hf, jnp.zeros((1, E_HY), f32), hb[:0:-1]], axis=0)
    return k / jnp.sum(jnp.abs(k), axis=0, keepdims=True)


def hyena_mixer(h, w_in, conv_w, conv_b, fw1, fb1, fw2, fb2, fw3, fb3, fw4, freq, d_skip, w_out):
    L = h.shape[1]
    proj = h @ w_in
    xv = short_conv(proj[..., :3 * E_HY], conv_w, conv_b)
    z = proj[..., 3 * E_HY:]
    x0, x1, v = jnp.split(xv, 3, axis=-1)
    u = x1 * v
    k = hyena_filter(L, fw1, fb1, fw2, fb2, fw3, fb3, fw4, freq)
    U = jnp.fft.rfft(u.astype(jnp.float32), n=2 * L, axis=1)
    K = jnp.fft.rfft(k, n=2 * L, axis=0)
    y = jnp.fft.irfft(U * K[None], n=2 * L, axis=1)[:, :L].astype(h.dtype)
    y = (y + u * d_skip) * x0
    return (y * jax.nn.silu(z)) @ w_out


def rope_half(x, ang):
    F = ang.shape[-1]
    shape = (ang.shape[0],) + (1,) * (x.ndim - 3) + (F,)
    cos = jnp.cos(ang).reshape(shape).astype(x.dtype)
    sin = jnp.sin(ang).reshape(shape).astype(x.dtype)
    x1, x2 = x[..., :F], x[..., F:]
    return jnp.concatenate([x1 * cos - x2 * sin, x1 * sin + x2 * cos], axis=-1)


def axial_rope(x, row_ang, col_ang):
    return jnp.concatenate([rope_half(x[..., :ROPE_AXIS_DIM], row_ang),
                            rope_half(x[..., ROPE_AXIS_DIM:], col_ang)], axis=-1)


def attend(q, k, v):
    s = jnp.einsum('bqkgd,bskd->bkgqs', q, k).astype(jnp.float32) * (HEAD_DIM ** -0.5)
    p = jax.nn.softmax(s, axis=-1).astype(v.dtype)
    return jnp.einsum('bkgqs,bskd->bqkgd', p, v)


def gqa_mixer(h_lat, h_ctx, w_in, q_g, k_g, w_out, want_ctx):
    B, L, _ = h_lat.shape
    C = h_ctx.shape[1]
    p = h_lat @ w_in
    q = rms_norm(p[..., :QD].reshape(B, L, N_KV, GROUP, HEAD_DIM), q_g)
    k = rms_norm(p[..., QD:QD + KVD].reshape(B, L, N_KV, HEAD_DIM), k_g)
    v = p[..., QD + KVD:QD + 2 * KVD].reshape(B, L, N_KV, HEAD_DIM)
    z = p[..., QD + 2 * KVD:]
    rows = L // GRID_W
    row = jnp.repeat(jnp.arange(rows, dtype=jnp.float32), GRID_W)
    col = jnp.tile(jnp.arange(GRID_W, dtype=jnp.float32), rows)
    inv = 1.0 / (ROPE_THETA ** (jnp.arange(0, ROPE_AXIS_DIM, 2, dtype=jnp.float32) / ROPE_AXIS_DIM))
    row_ang = row[:, None] * inv[None, :]
    col_ang = col[:, None] * inv[None, :]
    q = axial_rope(q, row_ang, col_ang)
    k = axial_rope(k, row_ang, col_ang)
    if want_ctx:
        pc = h_ctx @ w_in
    else:
        pc = h_ctx @ w_in[:, QD:QD + 2 * KVD]
        pc = jnp.pad(pc, ((0, 0), (0, 0), (QD, 0)))
    kc = rms_norm(pc[..., QD:QD + KVD].reshape(B, C, N_KV, HEAD_DIM), k_g)
    vc = pc[..., QD + KVD:QD + 2 * KVD].reshape(B, C, N_KV, HEAD_DIM)
    k_all = jnp.concatenate([k, kc], axis=1)
    v_all = jnp.concatenate([v, vc], axis=1)
    nb = L // Q_BLOCK
    qb = q.reshape(B, nb, Q_BLOCK, N_KV, GROUP, HEAD_DIM).transpose(1, 0, 2, 3, 4, 5)
    o = lax.map(lambda qq: attend(qq, k_all, v_all), qb)
    o = o.transpose(1, 0, 2, 3, 4, 5).reshape(B, L, QD)
    y_lat = (o * jax.nn.silu(z)) @ w_out
    if want_ctx:
        qc = rms_norm(pc[..., :QD].reshape(B, C, N_KV, GROUP, HEAD_DIM), q_g)
        oc = attend(qc, kc, vc).reshape(B, C, QD)
        y_ctx = (oc * jax.nn.silu(pc[..., QD + 2 * KVD:])) @ w_out
    else:
        y_ctx = None
    return y_lat, y_ctx


def setup_inputs(seed: int = 0) -> dict:
    key = jax.random.key(seed)
    ks = jax.random.split(key, 32)
    f32 = jnp.float32
    na = (DEPTH + 1) // 2
    nb = DEPTH // 2

    def nrm(k, shape, scale):
        return jax.random.normal(k, shape, f32) * scale

    return {
        "x": nrm(ks[0], (BATCH, SEQ, D_MODEL), 1.0),
        "c": nrm(ks[1], (BATCH, D_MODEL), 1.0),
        "ctx": nrm(ks[2], (BATCH, CTX_LEN, D_MODEL), 1.0),
        "c_ctx": nrm(ks[3], (D_MODEL,), 1.0),
        "norm_g": 1.0 + nrm(ks[4], (DEPTH, D_MODEL), 0.05),
        "ada_w": nrm(ks[5], (DEPTH, D_MODEL, 3 * D_MODEL), D_MODEL ** -0.5),
        "ada_b": nrm(ks[6], (DEPTH, 3 * D_MODEL), 0.02),
        "hy_w_in": nrm(ks[7], (na, D_MODEL, 4 * E_HY), D_MODEL ** -0.5),
        "hy_conv_w": nrm(ks[8], (na, 3, 3 * E_HY), 3 ** -0.5),
        "hy_conv_b": nrm(ks[9], (na, 3 * E_HY), 0.02),
        "hy_fw1": nrm(ks[10], (na, FILTER_EMB, FILTER_WIDTH), FILTER_EMB ** -0.5),
        "hy_fb1": nrm(ks[11], (na, FILTER_WIDTH), 0.02),
        "hy_fw2": nrm(ks[12], (na, FILTER_WIDTH, FILTER_WIDTH), FILTER_WIDTH ** -0.5),
        "hy_fb2": nrm(ks[13], (na, FILTER_WIDTH), 0.02),
        "hy_fw3": nrm(ks[14], (na, FILTER_WIDTH, FILTER_WIDTH), FILTER_WIDTH ** -0.5),
        "hy_fb3": nrm(ks[15], (na, FILTER_WIDTH), 0.02),
        "hy_fw4": nrm(ks[16], (na, FILTER_WIDTH, 2 * E_HY), FILTER_WIDTH ** -0.5),
        "hy_freq": 1.0 + nrm(ks[17], (na, FILTER_WIDTH), 0.05),
        "hy_d": nrm(ks[18], (na, E_HY), 0.5),
        "hy_w_out": nrm(ks[19], (na, E_HY, D_MODEL), E_HY ** -0.5),
        "at_w_in": nrm(ks[20], (nb, D_MODEL, 2 * QD + 2 * KVD), D_MODEL ** -0.5),
        "at_q_g": 1.0 + nrm(ks[21], (nb, HEAD_DIM), 0.05),
        "at_k_g": 1.0 + nrm(ks[22], (nb, HEAD_DIM), 0.05),
        "at_w_out": nrm(ks[23], (nb, QD, D_MODEL), QD ** -0.5),
    }


def reference(x, c, ctx, c_ctx, norm_g, ada_w, ada_b,
              hy_w_in, hy_conv_w, hy_conv_b, hy_fw1, hy_fb1, hy_fw2, hy_fb2, hy_fw3, hy_fb3,
              hy_fw4, hy_freq, hy_d, hy_w_out,
              at_w_in, at_q_g, at_k_g, at_w_out):
    sc_lat = jax.nn.silu(c)
    sc_ctx = jax.nn.silu(c_ctx)
    x_lat, x_ctx = x, ctx
    for i in range(DEPTH):
        last = i == DEPTH - 1
        mixer = i % N_MIXERS
        j = i // N_MIXERS
        need_ctx_in = (not last) or mixer == 1
        sh, sc, gt = jnp.split(sc_lat @ ada_w[i] + ada_b[i], 3, axis=-1)
        h_lat = rms_norm(x_lat, norm_g[i]) * (1.0 + sc[:, None]) + sh[:, None]
        if need_ctx_in:
            sh_c, sc_c, gt_c = jnp.split(sc_ctx @ ada_w[i] + ada_b[i], 3, axis=-1)
            h_ctx = rms_norm(x_ctx, norm_g[i]) * (1.0 + sc_c) + sh_c
        if mixer == 0:
            hp = (hy_w_in[j], hy_conv_w[j], hy_conv_b[j], hy_fw1[j], hy_fb1[j], hy_fw2[j], hy_fb2[j],
                  hy_fw3[j], hy_fb3[j], hy_fw4[j], hy_freq[j], hy_d[j], hy_w_out[j])
            y_lat = hyena_mixer(h_lat, *hp)
            y_ctx = None if last else hyena_mixer(h_ctx, *hp)
        else:
            y_lat, y_ctx = gqa_mixer(h_lat, h_ctx, at_w_in[j], at_q_g[j], at_k_g[j], at_w_out[j],
                                     not last)
        x_lat = x_lat + gt[:, None] * y_lat
        if not last:
            x_ctx = x_ctx + gt_c * y_ctx
    return x_lat
```

```python
import functools
import math

import numpy as np
import jax
import jax.numpy as jnp
from jax import lax
from jax.experimental import pallas as pl
from jax.experimental.pallas import tpu as pltpu

F32 = jnp.float32
BF16 = jnp.bfloat16
HIGHEST = lax.Precision.HIGHEST

EPS = 1e-6
HEAD_DIM = 128
GROUP = 4
GROUP_W = GROUP * HEAD_DIM
GRID_W = 64
ROPE_THETA = 10000.0
FILTER_BANDS = 16
DECAY_TARGET = 1e-2
MAX_DECAY = math.log(DECAY_TARGET) / 0.3
MIN_DECAY = math.log(DECAY_TARGET) / 1.5
HALO = 16
LANES = 128


def _sigmoid(x):
    return 1.0 / (1.0 + jnp.exp(-x))


def _mm(a, b):
    return jnp.dot(a, b, preferred_element_type=F32)


def _mm_f32(a, b):
    return jnp.dot(a, b, preferred_element_type=F32, precision=HIGHEST)


def _mod_kernel(c_ref, w_ref, b_ref, o_ref):
    cv = c_ref[...]
    o_ref[...] = _mm_f32(cv * _sigmoid(cv), w_ref[...]) + b_ref[...]


def _modulation(cv, ada_w, ada_b):
    depth, d, d3 = ada_w.shape
    rows = cv.shape[0]
    tn = d
    return pl.pallas_call(
        _mod_kernel,
        out_shape=jax.ShapeDtypeStruct((depth, rows, d3), F32),
        grid=(depth, d3 // tn),
        in_specs=[
            pl.BlockSpec((rows, d), lambda l, j: (0, 0)),
            pl.BlockSpec((None, d, tn), lambda l, j: (l, 0, j)),
            pl.BlockSpec((None, 1, tn), lambda l, j: (l, 0, j)),
        ],
        out_specs=pl.BlockSpec((None, rows, tn), lambda l, j: (l, 0, j)),
        name="modulation",
    )(cv, ada_w, ada_b.reshape(depth, 1, d3))


def _norm_mod(x, g, sc, sh):
    ms = jnp.mean(x * x, axis=-1, keepdims=True)
    return x * lax.rsqrt(ms + EPS) * (g * (1.0 + sc)) + sh


def _hy_proj_kernel(xm_ref, xp_ref, xn_ref, g_ref, sc_ref, sh_ref,
                    w0_ref, w1_ref, w2_ref, w3_ref,
                    cw0_ref, cw1_ref, cw2_ref, cb0_ref, cb1_ref, cb2_ref,
                    u_ref, gate_ref, hbuf, *, tm, n_i):
    i = pl.program_id(1)
    j = pl.program_id(2)

    @pl.when(j == 0)
    def _():
        g, sc, sh = g_ref[...], sc_ref[...], sh_ref[...]
        hbuf[HALO:HALO + tm, :] = _norm_mod(xm_ref[...], g, sc, sh).astype(BF16)
        hp = _norm_mod(xp_ref[...], g, sc, sh)
        hbuf[0:HALO, :] = jnp.where(i > 0, hp, 0.0).astype(BF16)
        hn = _norm_mod(xn_ref[...], g, sc, sh)
        hbuf[HALO + tm:, :] = jnp.where(i < n_i - 1, hn, 0.0).astype(BF16)

    h = hbuf[...]

    def conv(w_ref, cw_ref, cb_ref):
        p = _mm(h, w_ref[...])
        cw = cw_ref[...]
        return (p[HALO - 1:HALO - 1 + tm] * cw[0:1] + p[HALO:HALO + tm] * cw[1:2]
                + p[HALO + 1:HALO + 1 + tm] * cw[2:3] + cb_ref[...])

    x1 = conv(w1_ref, cw1_ref, cb1_ref)
    v = conv(w2_ref, cw2_ref, cb2_ref)
    u_ref[...] = (x1 * v).astype(BF16)
    x0 = conv(w0_ref, cw0_ref, cb0_ref)
    z = _mm(hbuf[HALO:HALO + tm, :], w3_ref[...])
    gate_ref[...] = (x0 * z * _sigmoid(z)).astype(BF16)


def _hy_proj(x, g, sc, sh, w_bf, conv_w, conv_b, *, tm, tn):
    b, l, d = x.shape
    e = w_bf.shape[1] // 4
    n_i, n_j = l // tm, e // tn
    hb = tm // HALO
    last_hb = l // HALO - 1

    def wspec(grp):
        return pl.BlockSpec((d, tn), lambda bb, i, j: (0, grp * n_j + j))

    def cwspec(grp):
        return pl.BlockSpec((3, tn), lambda bb, i, j: (0, grp * n_j + j))

    def cbspec(grp):
        return pl.BlockSpec((1, tn), lambda bb, i, j: (0, grp * n_j + j))

    vec = pl.BlockSpec((None, 1, d), lambda bb, i, j: (bb, 0, 0))
    out = pl.BlockSpec((None, tm, tn), lambda bb, i, j: (bb, i, j))
    return pl.pallas_call(
        functools.partial(_hy_proj_kernel, tm=tm, n_i=n_i),
        out_shape=(jax.ShapeDtypeStruct((b, l, e), BF16), jax.ShapeDtypeStruct((b, l, e), BF16)),
        grid=(b, n_i, n_j),
        in_specs=[
            pl.BlockSpec((None, tm, d), lambda bb, i, j: (bb, i, 0)),
            pl.BlockSpec((None, HALO, d), lambda bb, i, j: (bb, jnp.maximum(i * hb - 1, 0), 0)),
            pl.BlockSpec((None, HALO, d), lambda bb, i, j: (bb, jnp.minimum((i + 1) * hb, last_hb), 0)),
            pl.BlockSpec((1, d), lambda bb, i, j: (0, 0)), vec, vec,
            wspec(0), wspec(1), wspec(2), wspec(3),
            cwspec(0), cwspec(1), cwspec(2), cbspec(0), cbspec(1), cbspec(2),
        ],
        out_specs=(out, out),
        scratch_shapes=[pltpu.VMEM((tm + 2 * HALO, d), BF16)],
        compiler_params=pltpu.CompilerParams(
            dimension_semantics=("parallel", "parallel", "arbitrary")),
        name="hyena_in_proj",
    )(x, x, x, g.reshape(1, d), sc, sh, w_bf, w_bf, w_bf, w_bf,
      conv_w, conv_w, conv_w, conv_b.reshape(1, -1), conv_b.reshape(1, -1), conv_b.reshape(1, -1))


def _filter_kernel(emb_ref, w1_ref, b1_ref, w2_ref, b2_ref, w3_ref, b3_ref, fq_ref,
                   w4_ref, dl_ref, k_ref, s_ref, *, l, tr):
    t_idx = pl.program_id(0)
    r = t_idx * tr + lax.broadcasted_iota(jnp.int32, (tr, 1), 0)
    p = jnp.where(r < l, r, 2 * l - r).astype(F32)
    t = p / (l - 1.0)
    w = (2.0 * math.pi) * p / l
    emb = emb_ref[...]
    z = emb[2:3] * t + emb[3:4] * jnp.cos(emb[0:1] * w + emb[1:2])
    fq = fq_ref[...]
    h = jnp.sin(fq * (_mm_f32(z, w1_ref[...]) + b1_ref[...]))
    h = jnp.sin(fq * (_mm_f32(h, w2_ref[...]) + b2_ref[...]))
    h = jnp.sin(fq * (_mm_f32(h, w3_ref[...]) + b3_ref[...]))
    k = _mm_f32(h, w4_ref[...]) * jnp.exp(-t * dl_ref[...])
    k = jnp.where(r == l, 0.0, k)
    k_ref[...] = k.astype(BF16)

    @pl.when(t_idx == 0)
    def _():
        s_ref[...] = jnp.zeros_like(s_ref)

    s_ref[...] += jnp.sum(jnp.abs(k), axis=0, keepdims=True)


def _filter_embedding_table():
    fr = np.linspace(1e-4, FILTER_BANDS - 1, FILTER_BANDS)
    emb = np.zeros((8, LANES), np.float32)
    emb[0, 1:1 + FILTER_BANDS] = fr
    emb[0, 1 + FILTER_BANDS:1 + 2 * FILTER_BANDS] = fr
    emb[1, 1 + FILTER_BANDS:1 + 2 * FILTER_BANDS] = math.pi / 2
    emb[2, 0] = 1.0
    emb[3, 1:1 + 2 * FILTER_BANDS] = 1.0
    return emb


def _hyena_filter(l, fw1, fb1, fw2, fb2, fw3, fb3, fw4, freq, *, tr):
    width = fw1.shape[1]
    e = fw4.shape[1] // 2
    w1p = jnp.zeros((LANES, width), F32).at[:fw1.shape[0]].set(fw1)
    deltas = jnp.abs(jnp.linspace(MIN_DECAY, MAX_DECAY, e, dtype=F32)).reshape(1, e)
    half = l // tr
    small = lambda shape: pl.BlockSpec(shape, lambda t: (0, 0))
    return pl.pallas_call(
        functools.partial(_filter_kernel, l=l, tr=tr),
        out_shape=(jax.ShapeDtypeStruct((2 * l, e), BF16), jax.ShapeDtypeStruct((1, e), F32)),
        grid=(2 * half,),
        in_specs=[
            small((8, LANES)), small((LANES, width)), small((1, width)),
            small((width, width)), small((1, width)), small((width, width)), small((1, width)),
            small((1, width)),
            pl.BlockSpec((width, e), lambda t: (0, t // half)),
            small((1, e)),
        ],
        out_specs=(pl.BlockSpec((tr, e), lambda t: (t, 0)), pl.BlockSpec((1, e), lambda t: (0, 0))),
        compiler_params=pltpu.CompilerParams(dimension_semantics=("arbitrary",)),
        name="hyena_filter",
    )(jnp.asarray(_filter_embedding_table()), w1p, fb1.reshape(1, -1), fw2, fb2.reshape(1, -1),
      fw3, fb3.reshape(1, -1), freq.reshape(1, -1), fw4, deltas)


@functools.lru_cache(maxsize=None)
def _four_step_consts(n1, n2):
    n = n1 * n2
    hh = n1 // 2
    k1 = np.arange(n1)
    c2 = np.arange(n2)

    def angles(sign, n1_count):
        nn = n2 * np.arange(n1_count)[None, None, :] + c2[:, None, None]
        return sign * 2.0 * np.pi * ((k1[None, :, None] * nn) % n) / n

    a = angles(-1.0, hh)
    c, s = np.cos(a), np.sin(a)
    g = np.zeros((n2, n1, 2, 2, hh))
    g[:, :, 0, 0], g[:, :, 0, 1], g[:, :, 1, 0], g[:, :, 1, 1] = c, -s, s, c
    g = g.reshape(n2, 2 * n1, n1)

    a = angles(-1.0, n1)
    gk = np.stack([np.cos(a), np.sin(a)], axis=2).reshape(n2, 2 * n1, n1)

    a = np.transpose(angles(1.0, hh), (0, 2, 1))
    c, s = np.cos(a) / n, np.sin(a) / n
    h = np.zeros((n2, 2, hh, n1, 2))
    h[:, 0, :, :, 0], h[:, 0, :, :, 1], h[:, 1, :, :, 0], h[:, 1, :, :, 1] = c, -s, s, c
    h = h.reshape(n2, n1, 2 * n1)

    phi = -2.0 * np.pi * ((c2[:, None] * c2[None, :]) % n2) / n2
    c, s = np.cos(phi), np.sin(phi)
    fb = np.block([[c, -s], [s, c]])
    fbi = np.block([[c, s], [-s, c]])
    f = lambda x: np.asarray(x, np.float32)
    return f(g), f(gk), f(h), f(fb), f(fbi)


@functools.lru_cache(maxsize=None)
def _direct_dft_consts(nc):
    hh = nc // 2
    k = np.arange(nc)
    a = -2.0 * np.pi * ((k[:, None] * np.arange(hh)[None, :]) % nc) / nc
    c, s = np.cos(a), np.sin(a)
    fc = np.block([[c, -s], [s, c]])
    a = -2.0 * np.pi * ((k[:, None] * k[None, :]) % nc) / nc
    fck = np.concatenate([np.cos(a), np.sin(a)], axis=0)
    a = 2.0 * np.pi * ((np.arange(hh)[:, None] * k[None, :]) % nc) / nc
    c, s = np.cos(a) / nc, np.sin(a) / nc
    fci = np.block([[c, -s], [s, c]])
    f = lambda x: np.asarray(x, np.float32)
    return f(fc), f(fck), f(fci)


def _colmm_kernel(g_ref, x_ref, o_ref, *, nb, e, cb):
    for q in range(nb):
        for cc in range(e // cb):
            lo = q * e + cc * cb
            o_ref[:, lo:lo + cb] = _mm(g_ref[q], x_ref[:, lo:lo + cb]).astype(o_ref.dtype)


def _colmm(gm, x2, e, out_dtype, *, nb, cb, name):
    n2, r_out, r_in = gm.shape
    return pl.pallas_call(
        functools.partial(_colmm_kernel, nb=nb, e=e, cb=cb),
        out_shape=jax.ShapeDtypeStruct((r_out, n2 * e), out_dtype),
        grid=(n2 // nb,),
        in_specs=[
            pl.BlockSpec((nb, r_out, r_in), lambda s: (s, 0, 0)),
            pl.BlockSpec((r_in, nb * e), lambda s: (0, s)),
        ],
        out_specs=pl.BlockSpec((r_out, nb * e), lambda s: (0, s)),
        compiler_params=pltpu.CompilerParams(dimension_semantics=("parallel",)),
        name=name,
    )(gm, x2)


def _filter_spectrum_kernel(fb_ref, a_ref, s_ref, k_ref, *, kb):
    inv = 1.0 / s_ref[...]
    fb = fb_ref[...]
    for q in range(kb):
        k_ref[q] = (_mm(fb, a_ref[q]) * inv).astype(BF16)


def _filter_spectrum(fb, a3, s, *, kb, cb):
    n1, r, e = a3.shape
    return pl.pallas_call(
        functools.partial(_filter_spectrum_kernel, kb=kb),
        out_shape=jax.ShapeDtypeStruct((n1, r, e), BF16),
        grid=(n1 // kb, e // cb),
        in_specs=[
            pl.BlockSpec((r, r), lambda i, j: (0, 0)),
            pl.BlockSpec((kb, r, cb), lambda i, j: (i, 0, j)),
            pl.BlockSpec((1, cb), lambda i, j: (0, j)),
        ],
        out_specs=pl.BlockSpec((kb, r, cb), lambda i, j: (i, 0, j)),
        compiler_params=pltpu.CompilerParams(dimension_semantics=("parallel", "parallel")),
        name="hyena_filter_spectrum",
    )(fb, a3, s)


def _cmul(x, k, half):
    xr, xi = x[:half], x[half:]
    kr, ki = k[:half], k[half:]
    return jnp.concatenate([xr * kr - xi * ki, xr * ki + xi * kr], axis=0)


def _spectral_mul_kernel(fb_ref, fbi_ref, a_ref, k_ref, z_ref, *, kb):
    fb, fbi = fb_ref[...], fbi_ref[...]
    half = fb.shape[0] // 2
    for q in range(kb):
        x = _mm(fb, a_ref[q])
        y = _cmul(x, k_ref[q].astype(F32), half).astype(BF16)
        z_ref[q] = _mm(fbi, y).astype(BF16)


def _spectral_mul(fb, fbi, a3, k3, *, kb, cb):
    n1, r, e = a3.shape
    slab = pl.BlockSpec((kb, r, cb), lambda i, j: (i, 0, j))
    mat = pl.BlockSpec((r, r), lambda i, j: (0, 0))
    return pl.pallas_call(
        functools.partial(_spectral_mul_kernel, kb=kb),
        out_shape=jax.ShapeDtypeStruct((n1, r, e), BF16),
        grid=(n1 // kb, e // cb),
        in_specs=[mat, mat, slab, slab],
        out_specs=slab,
        compiler_params=pltpu.CompilerParams(dimension_semantics=("parallel", "parallel")),
        name="hyena_spectral_mul",
    )(fb, fbi, a3, k3)


def _ctx_conv_kernel(fc_ref, fck_ref, fci_ref, u_ref, k_ref, s_ref, y_ref):
    x = _mm(fc_ref[...], u_ref[...])
    ks = _mm(fck_ref[...], k_ref[...]) * (1.0 / s_ref[...])
    y = _cmul(x, ks, x.shape[0] // 2).astype(BF16)
    y_ref[...] = _mm(fci_ref[...], y).astype(BF16)


def _ctx_conv(fc, fck, fci, u2, k2, s, *, cb):
    nc, e = u2.shape
    full = lambda a: pl.BlockSpec(a.shape, lambda j: (0, 0))
    col = pl.BlockSpec((nc, cb), lambda j: (0, j))
    return pl.pallas_call(
        _ctx_conv_kernel,
        out_shape=jax.ShapeDtypeStruct((nc, e), BF16),
        grid=(e // cb,),
        in_specs=[full(fc), full(fck), full(fci), col, col, pl.BlockSpec((1, cb), lambda j: (0, j))],
        out_specs=col,
        compiler_params=pltpu.CompilerParams(dimension_semantics=("parallel",)),
        name="hyena_ctx_conv",
    )(fc, fck, fci, u2, k2, s)


def _long_conv_latent(u, k_raw, s, *, nb, cb, kb):
    b, l, e = u.shape
    n = 2 * l
    n1 = 1 << ((n.bit_length() - 1 + 1) // 2)
    n2 = n // n1
    g, gk, h, fb, fbi = (jnp.asarray(m).astype(BF16) for m in _four_step_consts(n1, n2))
    a = _colmm(g, u.reshape(n1, n2 * e), e, BF16, nb=nb, cb=cb, name="hyena_dft_rows")
    ak = _colmm(gk, k_raw.reshape(n1, n2 * e), e, BF16, nb=nb, cb=cb, name="hyena_filter_dft_rows")
    kspec = _filter_spectrum(fb, ak.reshape(n1, 2 * n2, e), s, kb=kb, cb=cb)
    z = _spectral_mul(fb, fbi, a.reshape(n1, 2 * n2, e), kspec, kb=kb, cb=cb)
    y = _colmm(h, z.reshape(2 * n1, n2 * e), e, BF16, nb=nb, cb=cb, name="hyena_idft_rows")
    return y.reshape(b, l, e)


def _long_conv_ctx(u, k_raw, s, *, cb):
    b, c, e = u.shape
    fc, fck, fci = (jnp.asarray(m).astype(BF16) for m in _direct_dft_consts(2 * c))
    return _ctx_conv(fc, fck, fci, u.reshape(b * c, e), k_raw, s, cb=cb).reshape(b, c, e)


def _hy_out_kernel(y_ref, u_ref, g_ref, d_ref, w_ref, x_ref, gt_ref, o_ref):
    m = (y_ref[...].astype(F32) + u_ref[...].astype(F32) * d_ref[...]) * g_ref[...].astype(F32)
    o_ref[...] = x_ref[...] + gt_ref[...] * _mm(m.astype(BF16), w_ref[...])


def _hy_out(y, u, g, d_skip, w_bf, x, gt, *, tm):
    b, l, e = y.shape
    d = x.shape[-1]
    act = pl.BlockSpec((None, tm, e), lambda bb, i: (bb, i, 0))
    res = pl.BlockSpec((None, tm, d), lambda bb, i: (bb, i, 0))
    return pl.pallas_call(
        _hy_out_kernel,
        out_shape=jax.ShapeDtypeStruct(x.shape, F32),
        grid=(b, l // tm),
        in_specs=[act, act, act, pl.BlockSpec((1, e), lambda bb, i: (0, 0)),
                  pl.BlockSpec((e, d), lambda bb, i: (0, 0)), res,
                  pl.BlockSpec((None, 1, d), lambda bb, i: (bb, 0, 0))],
        out_specs=res,
        compiler_params=pltpu.CompilerParams(dimension_semantics=("parallel", "parallel")),
        name="hyena_out_proj",
    )(y, u, g, d_skip.reshape(1, e), w_bf, x, gt)


def _at_out_kernel(m_ref, w_ref, x_ref, gt_ref, o_ref):
    o_ref[...] = x_ref[...] + gt_ref[...] * _mm(m_ref[...], w_ref[...])


def _at_out(m, w_bf, x, gt, *, tm):
    b, l, e = m.shape
    d = x.shape[-1]
    res = pl.BlockSpec((None, tm, d), lambda bb, i: (bb, i, 0))
    return pl.pallas_call(
        _at_out_kernel,
        out_shape=jax.ShapeDtypeStruct(x.shape, F32),
        grid=(b, l // tm),
        in_specs=[pl.BlockSpec((None, tm, e), lambda bb, i: (bb, i, 0)),
                  pl.BlockSpec((e, d), lambda bb, i: (0, 0)), res,
                  pl.BlockSpec((None, 1, d), lambda bb, i: (bb, 0, 0))],
        out_specs=res,
        compiler_params=pltpu.CompilerParams(dimension_semantics=("parallel", "parallel")),
        name="attn_out_proj",
    )(m, w_bf, x, gt)


def _head_norm_rope(p, gain, cos, sin, scale):
    lane = lax.broadcasted_iota(jnp.int32, (1, HEAD_DIM), 1)
    first = (lane % (HEAD_DIM // 2)) < (HEAD_DIM // 4)
    outs = []
    for hd in range(p.shape[1] // HEAD_DIM):
        xh = p[:, hd * HEAD_DIM:(hd + 1) * HEAD_DIM]
        ms = jnp.mean(xh * xh, axis=-1, keepdims=True)
        yh = xh * lax.rsqrt(ms + EPS) * gain
        partner = jnp.where(first, pltpu.roll(yh, HEAD_DIM - HEAD_DIM // 4, 1),
                            pltpu.roll(yh, HEAD_DIM // 4, 1))
        outs.append((yh * cos + partner * sin) * scale)
    return jnp.concatenate(outs, axis=1)


def _at_proj_kernel(x_ref, g_ref, sc_ref, sh_ref, w_ref, qg_ref, kg_ref, cos_ref, sin_ref,
                    q_ref, k_ref, v_ref, sz_ref, hbuf, *, n_q):
    j = pl.program_id(2)

    @pl.when(j == 0)
    def _():
        hbuf[...] = _norm_mod(x_ref[...], g_ref[...], sc_ref[...], sh_ref[...]).astype(BF16)

    p = _mm(hbuf[...], w_ref[...])

    @pl.when(j < n_q)
    def _():
        q_ref[...] = _head_norm_rope(p, qg_ref[...], cos_ref[...], sin_ref[...],
                                     HEAD_DIM ** -0.5).astype(BF16)

    @pl.when(j == n_q)
    def _():
        k_ref[...] = _head_norm_rope(p, kg_ref[...], cos_ref[...], sin_ref[...], 1.0).astype(BF16)

    @pl.when(j == n_q + 1)
    def _():
        v_ref[...] = p.astype(BF16)

    @pl.when(j > n_q + 1)
    def _():
        sz_ref[...] = (p * _sigmoid(p)).astype(BF16)


def _at_proj(x, g, sc, sh, w_bf, q_g, k_g, cos, sin, *, tm):
    b, l, d = x.shape
    kvd = GROUP_W
    qd = (w_bf.shape[1] - 2 * kvd) // 2
    n_q = qd // GROUP_W
    vec = pl.BlockSpec((None, 1, d), lambda bb, i, j: (bb, 0, 0))
    hv = pl.BlockSpec((1, HEAD_DIM), lambda bb, i, j: (0, 0))
    tab = pl.BlockSpec((tm, HEAD_DIM), lambda bb, i, j: (i, 0))
    return pl.pallas_call(
        functools.partial(_at_proj_kernel, n_q=n_q),
        out_shape=(jax.ShapeDtypeStruct((b, l, qd), BF16), jax.ShapeDtypeStruct((b, l, kvd), BF16),
                   jax.ShapeDtypeStruct((b, l, kvd), BF16), jax.ShapeDtypeStruct((b, l, qd), BF16)),
        grid=(b, l // tm, 2 * n_q + 2),
        in_specs=[
            pl.BlockSpec((None, tm, d), lambda bb, i, j: (bb, i, 0)),
            pl.BlockSpec((1, d), lambda bb, i, j: (0, 0)), vec, vec,
            pl.BlockSpec((d, GROUP_W), lambda bb, i, j: (0, j)),
            hv, hv, tab, tab,
        ],
        out_specs=(
            pl.BlockSpec((None, tm, GROUP_W), lambda bb, i, j: (bb, i, jnp.minimum(j, n_q - 1))),
            pl.BlockSpec((None, tm, GROUP_W), lambda bb, i, j: (bb, i, 0)),
            pl.BlockSpec((None, tm, GROUP_W), lambda bb, i, j: (bb, i, 0)),
            pl.BlockSpec((None, tm, GROUP_W), lambda bb, i, j: (bb, i, jnp.maximum(j - n_q - 2, 0))),
        ),
        scratch_shapes=[pltpu.VMEM((tm, d), BF16)],
        compiler_params=pltpu.CompilerParams(
            dimension_semantics=("parallel", "parallel", "arbitrary")),
        name="attn_in_proj",
    )(x, g.reshape(1, d), sc, sh, w_bf, q_g.reshape(1, -1), k_g.reshape(1, -1), cos, sin)


def _at_proj_kv_kernel(x_ref, g_ref, sc_ref, sh_ref, wk_ref, wv_ref, kg_ref, k_ref, v_ref):
    h = _norm_mod(x_ref[...], g_ref[...], sc_ref[...], sh_ref[...]).astype(BF16)
    p = _mm(h, wk_ref[...])
    gain = kg_ref[...]
    outs = []
    for hd in range(p.shape[1] // HEAD_DIM):
        xh = p[:, hd * HEAD_DIM:(hd + 1) * HEAD_DIM]
        ms = jnp.mean(xh * xh, axis=-1, keepdims=True)
        outs.append(xh * lax.rsqrt(ms + EPS) * gain)
    k_ref[...] = jnp.concatenate(outs, axis=1).astype(BF16)
    v_ref[...] = _mm(h, wv_ref[...]).astype(BF16)


def _at_proj_kv(x, g, sc, sh, w_bf, k_g, *, qd):
    b, c, d = x.shape
    kb = qd // GROUP_W
    vec = pl.BlockSpec((None, 1, d), lambda bb: (bb, 0, 0))
    out = pl.BlockSpec((None, c, GROUP_W), lambda bb: (bb, 0, 0))
    return pl.pallas_call(
        _at_proj_kv_kernel,
        out_shape=(jax.ShapeDtypeStruct((b, c, GROUP_W), BF16),) * 2,
        grid=(b,),
        in_specs=[
            pl.BlockSpec((None, c, d), lambda bb: (bb, 0, 0)),
            pl.BlockSpec((1, d), lambda bb: (0, 0)), vec, vec,
            pl.BlockSpec((d, GROUP_W), lambda bb: (0, kb)),
            pl.BlockSpec((d, GROUP_W), lambda bb: (0, kb + 1)),
            pl.BlockSpec((1, HEAD_DIM), lambda bb: (0, 0)),
        ],
        out_specs=(out, out),
        compiler_params=pltpu.CompilerParams(dimension_semantics=("parallel",)),
        name="attn_ctx_kv_proj",
    )(x, g.reshape(1, d), sc, sh, w_bf, w_bf, k_g.reshape(1, -1))


def _attn_kernel(q_ref, k_ref, v_ref, kc_ref, vc_ref, sz_ref, o_ref, m_ref, l_ref, acc_ref,
                 *, tq, tk, n_k):
    q = q_ref[...]
    q4 = jnp.concatenate([q[:, gi * HEAD_DIM:(gi + 1) * HEAD_DIM] for gi in range(GROUP)], axis=0)
    m_ref[...] = jnp.full_like(m_ref, -jnp.inf)
    l_ref[...] = jnp.zeros_like(l_ref)
    acc_ref[...] = jnp.zeros_like(acc_ref)

    def update(kt, vt):
        s = lax.dot_general(q4, kt, (((1,), (1,)), ((), ())), preferred_element_type=F32)
        m_old = m_ref[...]
        m_new = jnp.maximum(m_old, jnp.max(s, axis=-1, keepdims=True))
        p = jnp.exp(s - m_new)
        alpha = jnp.exp(m_old - m_new)
        l_ref[...] = alpha * l_ref[...] + jnp.sum(p, axis=-1, keepdims=True)
        acc_ref[...] = alpha * acc_ref[...] + _mm(p.astype(BF16), vt)
        m_ref[...] = m_new

    def body(t, carry):
        off = pl.multiple_of(t * tk, tk)
        update(k_ref[pl.ds(off, tk), :], v_ref[pl.ds(off, tk), :])
        return carry

    lax.fori_loop(0, n_k, body, 0)
    update(kc_ref[...], vc_ref[...])
    o = acc_ref[...] / l_ref[...]
    o = jnp.concatenate([o[gi * tq:(gi + 1) * tq] for gi in range(GROUP)], axis=1)
    o_ref[...] = (o * sz_ref[...].astype(F32)).astype(BF16)


def _attention(q, k, v, kc, vc, sz, *, tq, tk):
    b, l, qd = q.shape
    c = kc.shape[1]
    n_kv = qd // GROUP_W
    qspec = pl.BlockSpec((None, tq, GROUP_W), lambda bb, hh, i: (bb, i, hh))
    kspec = pl.BlockSpec((None, l, HEAD_DIM), lambda bb, hh, i: (bb, 0, hh))
    cspec = pl.BlockSpec((None, c, HEAD_DIM), lambda bb, hh, i: (bb, 0, hh))
    return pl.pallas_call(
        functools.partial(_attn_kernel, tq=tq, tk=tk, n_k=l // tk),
        out_shape=jax.ShapeDtypeStruct((b, l, qd), BF16),
        grid=(b, n_kv, l // tq),
        in_specs=[qspec, kspec, kspec, cspec, cspec, qspec],
        out_specs=qspec,
        scratch_shapes=[pltpu.VMEM((GROUP * tq, 1), F32), pltpu.VMEM((GROUP * tq, 1), F32),
                        pltpu.VMEM((GROUP * tq, HEAD_DIM), F32)],
        compiler_params=pltpu.CompilerParams(
            dimension_semantics=("parallel", "parallel", "arbitrary")),
        name="flash_attention",
    )(q, k, v, kc, vc, sz)


def _rope_tables(l):
    rows = l // GRID_W
    axis_dim = HEAD_DIM // 2
    row = jnp.repeat(jnp.arange(rows, dtype=F32), GRID_W)
    col = jnp.tile(jnp.arange(GRID_W, dtype=F32), rows)
    inv = 1.0 / (ROPE_THETA ** (jnp.arange(0, axis_dim, 2, dtype=F32) / axis_dim))
    ra, ca = row[:, None] * inv[None, :], col[:, None] * inv[None, :]
    cos = jnp.concatenate([jnp.cos(ra), jnp.cos(ra), jnp.cos(ca), jnp.cos(ca)], axis=-1)
    sin = jnp.concatenate([-jnp.sin(ra), jnp.sin(ra), -jnp.sin(ca), jnp.sin(ca)], axis=-1)
    return cos, sin


def _pick(n, target):
    t = min(n, target)
    while n % t:
        t //= 2
    return t


def kernel(x, c, ctx, c_ctx, norm_g, ada_w, ada_b,
           hy_w_in, hy_conv_w, hy_conv_b, hy_fw1, hy_fb1, hy_fw2, hy_fb2, hy_fw3, hy_fb3,
           hy_fw4, hy_freq, hy_d, hy_w_out,
           at_w_in, at_q_g, at_k_g, at_w_out):
    b, l, d = x.shape
    n_ctx = ctx.shape[1]
    e = hy_d.shape[-1]

    cv = jnp.concatenate([c, c_ctx[None], jnp.zeros((8 - b - 1, d), F32)], axis=0)
    mod = _modulation(cv, ada_w, ada_b)

    def lat(v):
        return v[:b, None, :]

    def cx(v):
        return jnp.broadcast_to(v[b][None, None, :], (b, 1, d))

    sh0, sc0, gt0 = jnp.split(mod[0], 3, axis=-1)
    w_in = hy_w_in[0].astype(BF16)
    w_out = hy_w_out[0].astype(BF16)
    fargs = (hy_fw1[0], hy_fb1[0], hy_fw2[0], hy_fb2[0], hy_fw3[0], hy_fb3[0], hy_fw4[0], hy_freq[0])
    tn = _pick(e, 256)
    cb = _pick(e, 512)

    u_lat, g_lat = _hy_proj(x, norm_g[0], lat(sc0), lat(sh0), w_in, hy_conv_w[0], hy_conv_b[0],
                            tm=_pick(l, 1024), tn=tn)
    u_ctx, g_ctx = _hy_proj(ctx, norm_g[0], cx(sc0), cx(sh0), w_in, hy_conv_w[0], hy_conv_b[0],
                            tm=_pick(n_ctx, 1024), tn=tn)
    k_lat, s_lat = _hyena_filter(l, *fargs, tr=_pick(l, 256))
    k_ctx, s_ctx = _hyena_filter(n_ctx, *fargs, tr=_pick(n_ctx, 256))
    y_lat = _long_conv_latent(u_lat, k_lat, s_lat, nb=4, cb=cb, kb=4)
    y_ctx = _long_conv_ctx(u_ctx, k_ctx, s_ctx, cb=cb)
    x1 = _hy_out(y_lat, u_lat, g_lat, hy_d[0], w_out, x, lat(gt0), tm=_pick(l, 512))
    ctx1 = _hy_out(y_ctx, u_ctx, g_ctx, hy_d[0], w_out, ctx, cx(gt0), tm=_pick(n_ctx, 512))

    sh1, sc1, gt1 = jnp.split(mod[1], 3, axis=-1)
    aw_in = at_w_in[0].astype(BF16)
    aw_out = at_w_out[0].astype(BF16)
    cos, sin = _rope_tables(l)
    q, k, v, sz = _at_proj(x1, norm_g[1], lat(sc1), lat(sh1), aw_in, at_q_g[0], at_k_g[0], cos, sin,
                           tm=_pick(l, 512))
    kc, vc = _at_proj_kv(ctx1, norm_g[1], cx(sc1), cx(sh1), aw_in, at_k_g[0], qd=q.shape[-1])
    om = _attention(q, k, v, kc, vc, sz, tq=_pick(l, 256), tk=_pick(l, 512))
    return _at_out(om, aw_out, x1, lat(gt1), tm=_pick(l, 512))
```

```python
import functools
import math

import numpy as np
import jax
import jax.numpy as jnp
from jax import lax
from jax.experimental import pallas as pl
from jax.experimental.pallas import tpu as pltpu

F32 = jnp.float32
BF16 = jnp.bfloat16
HIGHEST = lax.Precision.HIGHEST

EPS = 1e-6
HEAD_DIM = 128
GROUP = 4
GROUP_W = GROUP * HEAD_DIM
GRID_W = 64
ROPE_THETA = 10000.0
FILTER_BANDS = 16
DECAY_TARGET = 1e-2
MAX_DECAY = math.log(DECAY_TARGET) / 0.3
MIN_DECAY = math.log(DECAY_TARGET) / 1.5
HALO = 16
LANES = 128
VT_ROWS = HEAD_DIM + 16
LOG2E = math.log2(math.e)


def _sigmoid(x):
    return 1.0 / (1.0 + jnp.exp(-x))


def _mm(a, b):
    return jnp.dot(a, b, preferred_element_type=F32)


def _mm_f32(a, b):
    return jnp.dot(a, b, preferred_element_type=F32, precision=HIGHEST)


def _mod_kernel(c_ref, w_ref, b_ref, o_ref):
    cv = c_ref[...]
    o_ref[...] = _mm_f32(cv * _sigmoid(cv), w_ref[...]) + b_ref[...]


def _modulation(cv, ada_w, ada_b):
    depth, d, d3 = ada_w.shape
    rows = cv.shape[0]
    tn = d
    return pl.pallas_call(
        _mod_kernel,
        out_shape=jax.ShapeDtypeStruct((depth, rows, d3), F32),
        grid=(depth, d3 // tn),
        in_specs=[
            pl.BlockSpec((rows, d), lambda l, j: (0, 0)),
            pl.BlockSpec((None, d, tn), lambda l, j: (l, 0, j)),
            pl.BlockSpec((None, 1, tn), lambda l, j: (l, 0, j)),
        ],
        out_specs=pl.BlockSpec((None, rows, tn), lambda l, j: (l, 0, j)),
        name="modulation",
    )(cv, ada_w, ada_b.reshape(depth, 1, d3))


def _norm_mod(x, g, sc, sh):
    ms = jnp.mean(x * x, axis=-1, keepdims=True)
    return x * lax.rsqrt(ms + EPS) * (g * (1.0 + sc)) + sh


def _hy_proj_kernel(xm_ref, xp_ref, xn_ref, g_ref, sc_ref, sh_ref,
                    w0_ref, w1_ref, w2_ref, w3_ref,
                    cw0_ref, cw1_ref, cw2_ref, cb0_ref, cb1_ref, cb2_ref,
                    u_ref, gate_ref, hbuf, *, tm, n_i):
    i = pl.program_id(1)
    j = pl.program_id(2)

    @pl.when(j == 0)
    def _():
        g, sc, sh = g_ref[...], sc_ref[...], sh_ref[...]
        hbuf[HALO:HALO + tm, :] = _norm_mod(xm_ref[...], g, sc, sh).astype(BF16)
        hp = _norm_mod(xp_ref[...], g, sc, sh)
        hbuf[0:HALO, :] = jnp.where(i > 0, hp, 0.0).astype(BF16)
        hn = _norm_mod(xn_ref[...], g, sc, sh)
        hbuf[HALO + tm:, :] = jnp.where(i < n_i - 1, hn, 0.0).astype(BF16)

    h = hbuf[...]

    def conv(w_ref, cw_ref, cb_ref):
        p = _mm(h, w_ref[...])
        cw = cw_ref[...]
        return (p[HALO - 1:HALO - 1 + tm] * cw[0:1] + p[HALO:HALO + tm] * cw[1:2]
                + p[HALO + 1:HALO + 1 + tm] * cw[2:3] + cb_ref[...])

    x1 = conv(w1_ref, cw1_ref, cb1_ref)
    v = conv(w2_ref, cw2_ref, cb2_ref)
    u_ref[...] = (x1 * v).astype(BF16)
    x0 = conv(w0_ref, cw0_ref, cb0_ref)
    z = _mm(hbuf[HALO:HALO + tm, :], w3_ref[...])
    gate_ref[...] = (x0 * z * _sigmoid(z)).astype(BF16)


def _hy_proj(x, g, sc, sh, w_bf, conv_w, conv_b, *, tm, tn):
    b, l, d = x.shape
    e = w_bf.shape[1] // 4
    n_i, n_j = l // tm, e // tn
    hb = tm // HALO
    last_hb = l // HALO - 1

    def wspec(grp):
        return pl.BlockSpec((d, tn), lambda bb, i, j: (0, grp * n_j + j))

    def cwspec(grp):
        return pl.BlockSpec((3, tn), lambda bb, i, j: (0, grp * n_j + j))

    def cbspec(grp):
        return pl.BlockSpec((1, tn), lambda bb, i, j: (0, grp * n_j + j))

    vec = pl.BlockSpec((None, 1, d), lambda bb, i, j: (bb, 0, 0))
    out = pl.BlockSpec((None, tm, tn), lambda bb, i, j: (bb, i, j))
    return pl.pallas_call(
        functools.partial(_hy_proj_kernel, tm=tm, n_i=n_i),
        out_shape=(jax.ShapeDtypeStruct((b, l, e), BF16), jax.ShapeDtypeStruct((b, l, e), BF16)),
        grid=(b, n_i, n_j),
        in_specs=[
            pl.BlockSpec((None, tm, d), lambda bb, i, j: (bb, i, 0)),
            pl.BlockSpec((None, HALO, d), lambda bb, i, j: (bb, jnp.maximum(i * hb - 1, 0), 0)),
            pl.BlockSpec((None, HALO, d), lambda bb, i, j: (bb, jnp.minimum((i + 1) * hb, last_hb), 0)),
            pl.BlockSpec((1, d), lambda bb, i, j: (0, 0)), vec, vec,
            wspec(0), wspec(1), wspec(2), wspec(3),
            cwspec(0), cwspec(1), cwspec(2), cbspec(0), cbspec(1), cbspec(2),
        ],
        out_specs=(out, out),
        scratch_shapes=[pltpu.VMEM((tm + 2 * HALO, d), BF16)],
        compiler_params=pltpu.CompilerParams(
            dimension_semantics=("parallel", "parallel", "arbitrary")),
        name="hyena_in_proj",
    )(x, x, x, g.reshape(1, d), sc, sh, w_bf, w_bf, w_bf, w_bf,
      conv_w, conv_w, conv_w, conv_b.reshape(1, -1), conv_b.reshape(1, -1), conv_b.reshape(1, -1))


def _filter_kernel(emb_ref, w1_ref, b1_ref, w2_ref, b2_ref, w3_ref, b3_ref, fq_ref,
                   w4_ref, dl_ref, k_ref, s_ref, *, l, tr):
    t_idx = pl.program_id(0)
    r = t_idx * tr + lax.broadcasted_iota(jnp.int32, (tr, 1), 0)
    p = jnp.where(r < l, r, 2 * l - r).astype(F32)
    t = p / (l - 1.0)
    w = (2.0 * math.pi) * p / l
    emb = emb_ref[...]
    z = emb[2:3] * t + emb[3:4] * jnp.cos(emb[0:1] * w + emb[1:2])
    fq = fq_ref[...]
    h = jnp.sin(fq * (_mm_f32(z, w1_ref[...]) + b1_ref[...]))
    h = jnp.sin(fq * (_mm_f32(h, w2_ref[...]) + b2_ref[...]))
    h = jnp.sin(fq * (_mm_f32(h, w3_ref[...]) + b3_ref[...]))
    k = _mm_f32(h, w4_ref[...]) * jnp.exp(-t * dl_ref[...])
    k = jnp.where(r == l, 0.0, k)
    k_ref[...] = k.astype(BF16)

    @pl.when(t_idx == 0)
    def _():
        s_ref[...] = jnp.zeros_like(s_ref)

    s_ref[...] += jnp.sum(jnp.abs(k), axis=0, keepdims=True)


def _filter_embedding_table():
    fr = np.linspace(1e-4, FILTER_BANDS - 1, FILTER_BANDS)
    emb = np.zeros((8, LANES), np.float32)
    emb[0, 1:1 + FILTER_BANDS] = fr
    emb[0, 1 + FILTER_BANDS:1 + 2 * FILTER_BANDS] = fr
    emb[1, 1 + FILTER_BANDS:1 + 2 * FILTER_BANDS] = math.pi / 2
    emb[2, 0] = 1.0
    emb[3, 1:1 + 2 * FILTER_BANDS] = 1.0
    return emb


def _hyena_filter(l, fw1, fb1, fw2, fb2, fw3, fb3, fw4, freq, *, tr):
    width = fw1.shape[1]
    e = fw4.shape[1] // 2
    w1p = jnp.zeros((LANES, width), F32).at[:fw1.shape[0]].set(fw1)
    deltas = jnp.abs(jnp.linspace(MIN_DECAY, MAX_DECAY, e, dtype=F32)).reshape(1, e)
    half = l // tr
    small = lambda shape: pl.BlockSpec(shape, lambda t: (0, 0))
    return pl.pallas_call(
        functools.partial(_filter_kernel, l=l, tr=tr),
        out_shape=(jax.ShapeDtypeStruct((2 * l, e), BF16), jax.ShapeDtypeStruct((1, e), F32)),
        grid=(2 * half,),
        in_specs=[
            small((8, LANES)), small((LANES, width)), small((1, width)),
            small((width, width)), small((1, width)), small((width, width)), small((1, width)),
            small((1, width)),
            pl.BlockSpec((width, e), lambda t: (0, t // half)),
            small((1, e)),
        ],
        out_specs=(pl.BlockSpec((tr, e), lambda t: (t, 0)), pl.BlockSpec((1, e), lambda t: (0, 0))),
        compiler_params=pltpu.CompilerParams(dimension_semantics=("arbitrary",)),
        name="hyena_filter",
    )(jnp.asarray(_filter_embedding_table()), w1p, fb1.reshape(1, -1), fw2, fb2.reshape(1, -1),
      fw3, fb3.reshape(1, -1), freq.reshape(1, -1), fw4, deltas)


@functools.lru_cache(maxsize=None)
def _four_step_consts(n1, n2):
    n = n1 * n2
    hh = n1 // 2
    k1 = np.arange(n1)
    c2 = np.arange(n2)

    def angles(sign, n1_count):
        nn = n2 * np.arange(n1_count)[None, None, :] + c2[:, None, None]
        return sign * 2.0 * np.pi * ((k1[None, :, None] * nn) % n) / n

    a = angles(-1.0, hh)
    c, s = np.cos(a), np.sin(a)
    g = np.zeros((n2, n1, 2, 2, hh))
    g[:, :, 0, 0], g[:, :, 0, 1], g[:, :, 1, 0], g[:, :, 1, 1] = c, -s, s, c
    g = g.reshape(n2, 2 * n1, n1)

    a = angles(-1.0, n1)
    gk = np.stack([np.cos(a), np.sin(a)], axis=2).reshape(n2, 2 * n1, n1)

    a = np.transpose(angles(1.0, hh), (0, 2, 1))
    c, s = np.cos(a) / n, np.sin(a) / n
    h = np.zeros((n2, 2, hh, n1, 2))
    h[:, 0, :, :, 0], h[:, 0, :, :, 1], h[:, 1, :, :, 0], h[:, 1, :, :, 1] = c, -s, s, c
    h = h.reshape(n2, n1, 2 * n1)

    phi = -2.0 * np.pi * ((c2[:, None] * c2[None, :]) % n2) / n2
    c, s = np.cos(phi), np.sin(phi)
    fb = np.block([[c, -s], [s, c]])
    fbi = np.block([[c, s], [-s, c]])
    f = lambda x: np.asarray(x, np.float32)
    return f(g), f(gk), f(h), f(fb), f(fbi)


@functools.lru_cache(maxsize=None)
def _direct_dft_consts(nc):
    hh = nc // 2
    k = np.arange(nc)
    a = -2.0 * np.pi * ((k[:, None] * np.arange(hh)[None, :]) % nc) / nc
    c, s = np.cos(a), np.sin(a)
    fc = np.block([[c, -s], [s, c]])
    a = -2.0 * np.pi * ((k[:, None] * k[None, :]) % nc) / nc
    fck = np.concatenate([np.cos(a), np.sin(a)], axis=0)
    a = 2.0 * np.pi * ((np.arange(hh)[:, None] * k[None, :]) % nc) / nc
    c, s = np.cos(a) / nc, np.sin(a) / nc
    fci = np.block([[c, -s], [s, c]])
    f = lambda x: np.asarray(x, np.float32)
    return f(fc), f(fck), f(fci)


def _colmm_kernel(g_ref, x_ref, o_ref, *, nb, e, cb):
    for q in range(nb):
        for cc in range(e // cb):
            lo = q * e + cc * cb
            o_ref[:, lo:lo + cb] = _mm(g_ref[q], x_ref[:, lo:lo + cb]).astype(o_ref.dtype)


def _colmm(gm, x2, e, out_dtype, *, nb, cb, name):
    n2, r_out, r_in = gm.shape
    return pl.pallas_call(
        functools.partial(_colmm_kernel, nb=nb, e=e, cb=cb),
        out_shape=jax.ShapeDtypeStruct((r_out, n2 * e), out_dtype),
        grid=(n2 // nb,),
        in_specs=[
            pl.BlockSpec((nb, r_out, r_in), lambda s: (s, 0, 0)),
            pl.BlockSpec((r_in, nb * e), lambda s: (0, s)),
        ],
        out_specs=pl.BlockSpec((r_out, nb * e), lambda s: (0, s)),
        compiler_params=pltpu.CompilerParams(dimension_semantics=("parallel",)),
        name=name,
    )(gm, x2)


def _filter_spectrum_kernel(fb_ref, a_ref, s_ref, k_ref, *, kb):
    inv = 1.0 / s_ref[...]
    fb = fb_ref[...]
    for q in range(kb):
        k_ref[q] = (_mm(fb, a_ref[q]) * inv).astype(BF16)


def _filter_spectrum(fb, a3, s, *, kb, cb):
    n1, r, e = a3.shape
    return pl.pallas_call(
        functools.partial(_filter_spectrum_kernel, kb=kb),
        out_shape=jax.ShapeDtypeStruct((n1, r, e), BF16),
        grid=(n1 // kb, e // cb),
        in_specs=[
            pl.BlockSpec((r, r), lambda i, j: (0, 0)),
            pl.BlockSpec((kb, r, cb), lambda i, j: (i, 0, j)),
            pl.BlockSpec((1, cb), lambda i, j: (0, j)),
        ],
        out_specs=pl.BlockSpec((kb, r, cb), lambda i, j: (i, 0, j)),
        compiler_params=pltpu.CompilerParams(dimension_semantics=("parallel", "parallel")),
        name="hyena_filter_spectrum",
    )(fb, a3, s)


def _cmul(x, k, half):
    xr, xi = x[:half], x[half:]
    kr, ki = k[:half], k[half:]
    return jnp.concatenate([xr * kr - xi * ki, xr * ki + xi * kr], axis=0)


def _spectral_mul_kernel(fb_ref, fbi_ref, a_ref, k_ref, z_ref, *, kb):
    fb, fbi = fb_ref[...], fbi_ref[...]
    half = fb.shape[0] // 2
    for q in range(kb):
        x = _mm(fb, a_ref[q])
        y = _cmul(x, k_ref[q].astype(F32), half).astype(BF16)
        z_ref[q] = _mm(fbi, y).astype(BF16)


def _spectral_mul(fb, fbi, a3, k3, *, kb, cb):
    n1, r, e = a3.shape
    slab = pl.BlockSpec((kb, r, cb), lambda i, j: (i, 0, j))
    mat = pl.BlockSpec((r, r), lambda i, j: (0, 0))
    return pl.pallas_call(
        functools.partial(_spectral_mul_kernel, kb=kb),
        out_shape=jax.ShapeDtypeStruct((n1, r, e), BF16),
        grid=(n1 // kb, e // cb),
        in_specs=[mat, mat, slab, slab],
        out_specs=slab,
        compiler_params=pltpu.CompilerParams(dimension_semantics=("parallel", "parallel")),
        name="hyena_spectral_mul",
    )(fb, fbi, a3, k3)


def _ctx_conv_kernel(fc_ref, fck_ref, fci_ref, u_ref, k_ref, s_ref, y_ref):
    x = _mm(fc_ref[...], u_ref[...])
    ks = _mm(fck_ref[...], k_ref[...]) * (1.0 / s_ref[...])
    y = _cmul(x, ks, x.shape[0] // 2).astype(BF16)
    y_ref[...] = _mm(fci_ref[...], y).astype(BF16)


def _ctx_conv(fc, fck, fci, u2, k2, s, *, cb):
    nc, e = u2.shape
    full = lambda a: pl.BlockSpec(a.shape, lambda j: (0, 0))
    col = pl.BlockSpec((nc, cb), lambda j: (0, j))
    return pl.pallas_call(
        _ctx_conv_kernel,
        out_shape=jax.ShapeDtypeStruct((nc, e), BF16),
        grid=(e // cb,),
        in_specs=[full(fc), full(fck), full(fci), col, col, pl.BlockSpec((1, cb), lambda j: (0, j))],
        out_specs=col,
        compiler_params=pltpu.CompilerParams(dimension_semantics=("parallel",)),
        name="hyena_ctx_conv",
    )(fc, fck, fci, u2, k2, s)


def _long_conv_latent(u, k_raw, s, *, nb, cb, kb):
    b, l, e = u.shape
    n = 2 * l
    n1 = 1 << ((n.bit_length() - 1 + 1) // 2)
    n2 = n // n1
    g, gk, h, fb, fbi = (jnp.asarray(m).astype(BF16) for m in _four_step_consts(n1, n2))
    a = _colmm(g, u.reshape(n1, n2 * e), e, BF16, nb=nb, cb=cb, name="hyena_dft_rows")
    ak = _colmm(gk, k_raw.reshape(n1, n2 * e), e, BF16, nb=nb, cb=cb, name="hyena_filter_dft_rows")
    kspec = _filter_spectrum(fb, ak.reshape(n1, 2 * n2, e), s, kb=kb, cb=cb)
    z = _spectral_mul(fb, fbi, a.reshape(n1, 2 * n2, e), kspec, kb=kb, cb=cb)
    y = _colmm(h, z.reshape(2 * n1, n2 * e), e, BF16, nb=nb, cb=cb, name="hyena_idft_rows")
    return y.reshape(b, l, e)


def _long_conv_ctx(u, k_raw, s, *, cb):
    b, c, e = u.shape
    fc, fck, fci = (jnp.asarray(m).astype(BF16) for m in _direct_dft_consts(2 * c))
    return _ctx_conv(fc, fck, fci, u.reshape(b * c, e), k_raw, s, cb=cb).reshape(b, c, e)


def _hy_out_kernel(y_ref, u_ref, g_ref, d_ref, w_ref, x_ref, gt_ref, o_ref):
    m = (y_ref[...].astype(F32) + u_ref[...].astype(F32) * d_ref[...]) * g_ref[...].astype(F32)
    o_ref[...] = x_ref[...] + gt_ref[...] * _mm(m.astype(BF16), w_ref[...])


def _hy_out(y, u, g, d_skip, w_bf, x, gt, *, tm):
    b, l, e = y.shape
    d = x.shape[-1]
    act = pl.BlockSpec((None, tm, e), lambda bb, i: (bb, i, 0))
    res = pl.BlockSpec((None, tm, d), lambda bb, i: (bb, i, 0))
    return pl.pallas_call(
        _hy_out_kernel,
        out_shape=jax.ShapeDtypeStruct(x.shape, F32),
        grid=(b, l // tm),
        in_specs=[act, act, act, pl.BlockSpec((1, e), lambda bb, i: (0, 0)),
                  pl.BlockSpec((e, d), lambda bb, i: (0, 0)), res,
                  pl.BlockSpec((None, 1, d), lambda bb, i: (bb, 0, 0))],
        out_specs=res,
        compiler_params=pltpu.CompilerParams(dimension_semantics=("parallel", "parallel")),
        name="hyena_out_proj",
    )(y, u, g, d_skip.reshape(1, e), w_bf, x, gt)


def _at_out_kernel(m_ref, w_ref, x_ref, gt_ref, o_ref):
    o_ref[...] = x_ref[...] + gt_ref[...] * _mm(m_ref[...], w_ref[...])


def _at_out(m, w_bf, x, gt, *, tm):
    b, l, e = m.shape
    d = x.shape[-1]
    res = pl.BlockSpec((None, tm, d), lambda bb, i: (bb, i, 0))
    return pl.pallas_call(
        _at_out_kernel,
        out_shape=jax.ShapeDtypeStruct(x.shape, F32),
        grid=(b, l // tm),
        in_specs=[pl.BlockSpec((None, tm, e), lambda bb, i: (bb, i, 0)),
                  pl.BlockSpec((e, d), lambda bb, i: (0, 0)), res,
                  pl.BlockSpec((None, 1, d), lambda bb, i: (bb, 0, 0))],
        out_specs=res,
        compiler_params=pltpu.CompilerParams(dimension_semantics=("parallel", "parallel")),
        name="attn_out_proj",
    )(m, w_bf, x, gt)


def _head_norm_rope(p, gain, cos, sin, scale):
    lane = lax.broadcasted_iota(jnp.int32, (1, HEAD_DIM), 1)
    first = (lane % (HEAD_DIM // 2)) < (HEAD_DIM // 4)
    outs = []
    for hd in range(p.shape[1] // HEAD_DIM):
        xh = p[:, hd * HEAD_DIM:(hd + 1) * HEAD_DIM]
        ms = jnp.mean(xh * xh, axis=-1, keepdims=True)
        yh = xh * lax.rsqrt(ms + EPS) * gain
        partner = jnp.where(first, pltpu.roll(yh, HEAD_DIM - HEAD_DIM // 4, 1),
                            pltpu.roll(yh, HEAD_DIM // 4, 1))
        outs.append((yh * cos + partner * sin) * scale)
    return jnp.concatenate(outs, axis=1)


def _at_proj_kernel(x_ref, g_ref, sc_ref, sh_ref, w_ref, qg_ref, kg_ref, cos_ref, sin_ref,
                    q_ref, k_ref, v_ref, sz_ref, hbuf, *, n_q):
    j = pl.program_id(2)

    @pl.when(j == 0)
    def _():
        hbuf[...] = _norm_mod(x_ref[...], g_ref[...], sc_ref[...], sh_ref[...]).astype(BF16)

    p = _mm(hbuf[...], w_ref[...])

    @pl.when(j < n_q)
    def _():
        q_ref[...] = _head_norm_rope(p, qg_ref[...], cos_ref[...], sin_ref[...],
                                     HEAD_DIM ** -0.5 * LOG2E).astype(BF16)

    @pl.when(j == n_q)
    def _():
        k_ref[...] = _head_norm_rope(p, kg_ref[...], cos_ref[...], sin_ref[...], 1.0).astype(BF16)

    @pl.when(j == n_q + 1)
    def _():
        v_ref[...] = p.astype(BF16)

    @pl.when(j > n_q + 1)
    def _():
        sz_ref[...] = (p * _sigmoid(p)).astype(BF16)


def _at_proj(x, g, sc, sh, w_bf, q_g, k_g, cos, sin, *, tm):
    b, l, d = x.shape
    kvd = GROUP_W
    qd = (w_bf.shape[1] - 2 * kvd) // 2
    n_q = qd // GROUP_W
    vec = pl.BlockSpec((None, 1, d), lambda bb, i, j: (bb, 0, 0))
    hv = pl.BlockSpec((1, HEAD_DIM), lambda bb, i, j: (0, 0))
    tab = pl.BlockSpec((tm, HEAD_DIM), lambda bb, i, j: (i, 0))
    return pl.pallas_call(
        functools.partial(_at_proj_kernel, n_q=n_q),
        out_shape=(jax.ShapeDtypeStruct((b, l, qd), BF16), jax.ShapeDtypeStruct((b, l, kvd), BF16),
                   jax.ShapeDtypeStruct((b, l, kvd), BF16), jax.ShapeDtypeStruct((b, l, qd), BF16)),
        grid=(b, l // tm, 2 * n_q + 2),
        in_specs=[
            pl.BlockSpec((None, tm, d), lambda bb, i, j: (bb, i, 0)),
            pl.BlockSpec((1, d), lambda bb, i, j: (0, 0)), vec, vec,
            pl.BlockSpec((d, GROUP_W), lambda bb, i, j: (0, j)),
            hv, hv, tab, tab,
        ],
        out_specs=(
            pl.BlockSpec((None, tm, GROUP_W), lambda bb, i, j: (bb, i, jnp.minimum(j, n_q - 1))),
            pl.BlockSpec((None, tm, GROUP_W), lambda bb, i, j: (bb, i, 0)),
            pl.BlockSpec((None, tm, GROUP_W), lambda bb, i, j: (bb, i, 0)),
            pl.BlockSpec((None, tm, GROUP_W), lambda bb, i, j: (bb, i, jnp.maximum(j - n_q - 2, 0))),
        ),
        scratch_shapes=[pltpu.VMEM((tm, d), BF16)],
        compiler_params=pltpu.CompilerParams(
            dimension_semantics=("parallel", "parallel", "arbitrary")),
        name="attn_in_proj",
    )(x, g.reshape(1, d), sc, sh, w_bf, q_g.reshape(1, -1), k_g.reshape(1, -1), cos, sin)


def _at_proj_kv_kernel(x_ref, g_ref, sc_ref, sh_ref, wk_ref, wv_ref, kg_ref, k_ref, v_ref):
    h = _norm_mod(x_ref[...], g_ref[...], sc_ref[...], sh_ref[...]).astype(BF16)
    p = _mm(h, wk_ref[...])
    gain = kg_ref[...]
    outs = []
    for hd in range(p.shape[1] // HEAD_DIM):
        xh = p[:, hd * HEAD_DIM:(hd + 1) * HEAD_DIM]
        ms = jnp.mean(xh * xh, axis=-1, keepdims=True)
        outs.append(xh * lax.rsqrt(ms + EPS) * gain)
    k_ref[...] = jnp.concatenate(outs, axis=1).astype(BF16)
    v_ref[...] = _mm(h, wv_ref[...]).astype(BF16)


def _at_proj_kv(x, g, sc, sh, w_bf, k_g, *, qd):
    b, c, d = x.shape
    kb = qd // GROUP_W
    vec = pl.BlockSpec((None, 1, d), lambda bb: (bb, 0, 0))
    out = pl.BlockSpec((None, c, GROUP_W), lambda bb: (bb, 0, 0))
    return pl.pallas_call(
        _at_proj_kv_kernel,
        out_shape=(jax.ShapeDtypeStruct((b, c, GROUP_W), BF16),) * 2,
        grid=(b,),
        in_specs=[
            pl.BlockSpec((None, c, d), lambda bb: (bb, 0, 0)),
            pl.BlockSpec((1, d), lambda bb: (0, 0)), vec, vec,
            pl.BlockSpec((d, GROUP_W), lambda bb: (0, kb)),
            pl.BlockSpec((d, GROUP_W), lambda bb: (0, kb + 1)),
            pl.BlockSpec((1, HEAD_DIM), lambda bb: (0, 0)),
        ],
        out_specs=(out, out),
        compiler_params=pltpu.CompilerParams(dimension_semantics=("parallel",)),
        name="attn_ctx_kv_proj",
    )(x, g.reshape(1, d), sc, sh, w_bf, w_bf, k_g.reshape(1, -1))


def _attn_kernel(q_ref, k_ref, v_ref, kc_ref, vc_ref, sz_ref, o_ref, vt_ref, acc_ref,
                 *, tq, tk, n_k, c):
    @pl.when(pl.program_id(2) == 0)
    def _():
        ones = jnp.ones((VT_ROWS - HEAD_DIM, tk), BF16)
        for t in range(n_k):
            vt_ref[t, 0:HEAD_DIM, :] = v_ref[t * tk:(t + 1) * tk, :].astype(F32).T.astype(BF16)
            vt_ref[t, HEAD_DIM:, :] = ones
        vt_ref[n_k, 0:HEAD_DIM, 0:c] = vc_ref[...].astype(F32).T.astype(BF16)
        vt_ref[n_k, HEAD_DIM:, :] = ones

    q = q_ref[...]
    qt = jnp.concatenate([q[:, gi * HEAD_DIM:(gi + 1) * HEAD_DIM].astype(F32).T.astype(BF16)
                          for gi in range(GROUP)], axis=1)
    acc_ref[...] = jnp.zeros_like(acc_ref)

    def update(kt, vt, m_old):
        s = _mm(kt, qt)
        m_new = jnp.maximum(m_old, jnp.max(s, axis=0, keepdims=True))
        p = jnp.exp2(s - m_new).astype(BF16)
        acc_ref[...] = jnp.exp2(m_old - m_new) * acc_ref[...] + _mm(vt, p)
        return m_new

    def body(t, m):
        off = pl.multiple_of(t * tk, tk)
        return update(k_ref[pl.ds(off, tk), :], vt_ref[t], m)

    m = lax.fori_loop(0, n_k, body, jnp.full((1, GROUP * tq), -jnp.inf, F32))
    update(kc_ref[...], vt_ref[n_k, :, 0:c], m)
    acc = acc_ref[...]
    ot = acc[0:HEAD_DIM] * (1.0 / acc[HEAD_DIM:HEAD_DIM + 1])
    o = jnp.concatenate([ot[:, gi * tq:(gi + 1) * tq].T for gi in range(GROUP)], axis=1)
    o_ref[...] = (o * sz_ref[...].astype(F32)).astype(BF16)


def _attention(q, k, v, kc, vc, sz, *, tq, tk):
    b, l, qd = q.shape
    c = kc.shape[1]
    n_kv = qd // GROUP_W
    n_k = l // tk
    qspec = pl.BlockSpec((None, tq, GROUP_W), lambda bb, hh, i: (bb, i, hh))
    kspec = pl.BlockSpec((None, l, HEAD_DIM), lambda bb, hh, i: (bb, 0, hh))
    cspec = pl.BlockSpec((None, c, HEAD_DIM), lambda bb, hh, i: (bb, 0, hh))
    return pl.pallas_call(
        functools.partial(_attn_kernel, tq=tq, tk=tk, n_k=n_k, c=c),
        out_shape=jax.ShapeDtypeStruct((b, l, qd), BF16),
        grid=(b, n_kv, l // tq),
        in_specs=[qspec, kspec, kspec, cspec, cspec, qspec],
        out_specs=qspec,
        scratch_shapes=[pltpu.VMEM((n_k + 1, VT_ROWS, tk), BF16),
                        pltpu.VMEM((VT_ROWS, GROUP * tq), F32)],
        compiler_params=pltpu.CompilerParams(
            dimension_semantics=("parallel", "parallel", "arbitrary")),
        name="flash_attention",
    )(q, k, v, kc, vc, sz)


def _rope_tables(l):
    rows = l // GRID_W
    axis_dim = HEAD_DIM // 2
    row = jnp.repeat(jnp.arange(rows, dtype=F32), GRID_W)
    col = jnp.tile(jnp.arange(GRID_W, dtype=F32), rows)
    inv = 1.0 / (ROPE_THETA ** (jnp.arange(0, axis_dim, 2, dtype=F32) / axis_dim))
    ra, ca = row[:, None] * inv[None, :], col[:, None] * inv[None, :]
    cos = jnp.concatenate([jnp.cos(ra), jnp.cos(ra), jnp.cos(ca), jnp.cos(ca)], axis=-1)
    sin = jnp.concatenate([-jnp.sin(ra), jnp.sin(ra), -jnp.sin(ca), jnp.sin(ca)], axis=-1)
    return cos, sin


def _pick(n, target):
    t = min(n, target)
    while n % t:
        t //= 2
    return t


def kernel(x, c, ctx, c_ctx, norm_g, ada_w, ada_b,
           hy_w_in, hy_conv_w, hy_conv_b, hy_fw1, hy_fb1, hy_fw2, hy_fb2, hy_fw3, hy_fb3,
           hy_fw4, hy_freq, hy_d, hy_w_out,
           at_w_in, at_q_g, at_k_g, at_w_out):
    b, l, d = x.shape
    n_ctx = ctx.shape[1]
    e = hy_d.shape[-1]

    cv = jnp.concatenate([c, c_ctx[None], jnp.zeros((8 - b - 1, d), F32)], axis=0)
    mod = _modulation(cv, ada_w, ada_b)

    def lat(v):
        return v[:b, None, :]

    def cx(v):
        return jnp.broadcast_to(v[b][None, None, :], (b, 1, d))

    sh0, sc0, gt0 = jnp.split(mod[0], 3, axis=-1)
    w_in = hy_w_in[0].astype(BF16)
    w_out = hy_w_out[0].astype(BF16)
    fargs = (hy_fw1[0], hy_fb1[0], hy_fw2[0], hy_fb2[0], hy_fw3[0], hy_fb3[0], hy_fw4[0], hy_freq[0])
    tn = _pick(e, 256)
    cb = _pick(e, 512)

    u_lat, g_lat = _hy_proj(x, norm_g[0], lat(sc0), lat(sh0), w_in, hy_conv_w[0], hy_conv_b[0],
                            tm=_pick(l, 1024), tn=tn)
    u_ctx, g_ctx = _hy_proj(ctx, norm_g[0], cx(sc0), cx(sh0), w_in, hy_conv_w[0], hy_conv_b[0],
                            tm=_pick(n_ctx, 1024), tn=tn)
    k_lat, s_lat = _hyena_filter(l, *fargs, tr=_pick(l, 256))
    k_ctx, s_ctx = _hyena_filter(n_ctx, *fargs, tr=_pick(n_ctx, 256))
    y_lat = _long_conv_latent(u_lat, k_lat, s_lat, nb=4, cb=cb, kb=4)
    y_ctx = _long_conv_ctx(u_ctx, k_ctx, s_ctx, cb=cb)
    x1 = _hy_out(y_lat, u_lat, g_lat, hy_d[0], w_out, x, lat(gt0), tm=_pick(l, 512))
    ctx1 = _hy_out(y_ctx, u_ctx, g_ctx, hy_d[0], w_out, ctx, cx(gt0), tm=_pick(n_ctx, 512))

    sh1, sc1, gt1 = jnp.split(mod[1], 3, axis=-1)
    aw_in = at_w_in[0].astype(BF16)
    aw_out = at_w_out[0].astype(BF16)
    cos, sin = _rope_tables(l)
    q, k, v, sz = _at_proj(x1, norm_g[1], lat(sc1), lat(sh1), aw_in, at_q_g[0], at_k_g[0], cos, sin,
                           tm=_pick(l, 512))
    kc, vc = _at_proj_kv(ctx1, norm_g[1], cx(sc1), cx(sh1), aw_in, at_k_g[0], qd=q.shape[-1])
    om = _attention(q, k, v, kc, vc, sz, tq=_pick(l, 256), tk=_pick(l, 512))
    return _at_out(om, aw_out, x1, lat(gt1), tm=_pick(l, 512))
```

```python
import functools
import math

import numpy as np
import jax
import jax.numpy as jnp
from jax import lax
from jax.experimental import pallas as pl
from jax.experimental.pallas import tpu as pltpu

F32 = jnp.float32
BF16 = jnp.bfloat16
HIGHEST = lax.Precision.HIGHEST

EPS = 1e-6
HEAD_DIM = 128
GROUP = 4
GROUP_W = GROUP * HEAD_DIM
GRID_W = 64
ROPE_THETA = 10000.0
FILTER_BANDS = 16
DECAY_TARGET = 1e-2
MAX_DECAY = math.log(DECAY_TARGET) / 0.3
MIN_DECAY = math.log(DECAY_TARGET) / 1.5
HALO = 16
LANES = 128
VT_ROWS = HEAD_DIM + 16
LOG2E = math.log2(math.e)


def _sigmoid(x):
    return 1.0 / (1.0 + jnp.exp(-x))


def _mm(a, b):
    return jnp.dot(a, b, preferred_element_type=F32)


def _mm_f32(a, b):
    return jnp.dot(a, b, preferred_element_type=F32, precision=HIGHEST)


def _mod_kernel(c_ref, w_ref, b_ref, o_ref):
    cv = c_ref[...]
    o_ref[...] = _mm_f32(cv * _sigmoid(cv), w_ref[...]) + b_ref[...]


def _modulation(cv, ada_w, ada_b):
    depth, d, d3 = ada_w.shape
    rows = cv.shape[0]
    tn = d
    return pl.pallas_call(
        _mod_kernel,
        out_shape=jax.ShapeDtypeStruct((depth, rows, d3), F32),
        grid=(depth, d3 // tn),
        in_specs=[
            pl.BlockSpec((rows, d), lambda l, j: (0, 0)),
            pl.BlockSpec((None, d, tn), lambda l, j: (l, 0, j)),
            pl.BlockSpec((None, 1, tn), lambda l, j: (l, 0, j)),
        ],
        out_specs=pl.BlockSpec((None, rows, tn), lambda l, j: (l, 0, j)),
        name="modulation",
    )(cv, ada_w, ada_b.reshape(depth, 1, d3))


def _norm_mod(x, g, sc, sh):
    ms = jnp.mean(x * x, axis=-1, keepdims=True)
    return x * lax.rsqrt(ms + EPS) * (g * (1.0 + sc)) + sh


def _hy_proj_kernel(xm_ref, xp_ref, xn_ref, g_ref, sc_ref, sh_ref, w_ref, cw_ref, cb_ref,
                    u_ref, gate_ref, hbuf, *, tm, tn, n_i):
    i = pl.program_id(1)
    e = u_ref.shape[-1]
    g, sc, sh = g_ref[...], sc_ref[...], sh_ref[...]
    hbuf[HALO:HALO + tm, :] = _norm_mod(xm_ref[...], g, sc, sh).astype(BF16)
    hp = _norm_mod(xp_ref[...], g, sc, sh)
    hbuf[0:HALO, :] = jnp.where(i > 0, hp, 0.0).astype(BF16)
    hn = _norm_mod(xn_ref[...], g, sc, sh)
    hbuf[HALO + tm:, :] = jnp.where(i < n_i - 1, hn, 0.0).astype(BF16)

    def conv(lo):
        p = _mm(hbuf[...], w_ref[:, lo:lo + tn])
        cw = cw_ref[:, lo:lo + tn]
        return (p[HALO - 1:HALO - 1 + tm] * cw[0:1] + p[HALO:HALO + tm] * cw[1:2]
                + p[HALO + 1:HALO + 1 + tm] * cw[2:3] + cb_ref[:, lo:lo + tn])

    for j in range(e // tn):
        lo = j * tn
        u_ref[:, lo:lo + tn] = (conv(e + lo) * conv(2 * e + lo)).astype(BF16)
        z = _mm(hbuf[HALO:HALO + tm, :], w_ref[:, 3 * e + lo:3 * e + lo + tn])
        gate_ref[:, lo:lo + tn] = (conv(lo) * z * _sigmoid(z)).astype(BF16)


def _hy_proj(x, g, sc, sh, w_bf, conv_w, conv_b, *, tm, tn):
    b, l, d = x.shape
    e = w_bf.shape[1] // 4
    n_i = l // tm
    hb = tm // HALO
    last_hb = l // HALO - 1
    const = lambda shape: pl.BlockSpec(shape, lambda bb, i: (0, 0), pipeline_mode=pl.Buffered(1))
    vec = pl.BlockSpec((None, 1, d), lambda bb, i: (bb, 0, 0))
    out = pl.BlockSpec((None, tm, e), lambda bb, i: (bb, i, 0))
    return pl.pallas_call(
        functools.partial(_hy_proj_kernel, tm=tm, tn=tn, n_i=n_i),
        out_shape=(jax.ShapeDtypeStruct((b, l, e), BF16), jax.ShapeDtypeStruct((b, l, e), BF16)),
        grid=(b, n_i),
        in_specs=[
            pl.BlockSpec((None, tm, d), lambda bb, i: (bb, i, 0)),
            pl.BlockSpec((None, HALO, d), lambda bb, i: (bb, jnp.maximum(i * hb - 1, 0), 0)),
            pl.BlockSpec((None, HALO, d), lambda bb, i: (bb, jnp.minimum((i + 1) * hb, last_hb), 0)),
            const((1, d)), vec, vec,
            const((d, 4 * e)), const((3, 3 * e)), const((1, 3 * e)),
        ],
        out_specs=(out, out),
        scratch_shapes=[pltpu.VMEM((tm + 2 * HALO, d), BF16)],
        compiler_params=pltpu.CompilerParams(dimension_semantics=("parallel", "parallel")),
        name="hyena_in_proj",
    )(x, x, x, g.reshape(1, d), sc, sh, w_bf, conv_w, conv_b.reshape(1, -1))


def _filter_kernel(emb_ref, w1_ref, b1_ref, w2_ref, b2_ref, w3_ref, b3_ref, fq_ref,
                   w4_ref, dl_ref, k_ref, s_ref, *, l, tr):
    t_idx = pl.program_id(0)
    r = t_idx * tr + lax.broadcasted_iota(jnp.int32, (tr, 1), 0)
    p = jnp.where(r < l, r, 2 * l - r).astype(F32)
    t = p / (l - 1.0)
    w = (2.0 * math.pi) * p / l
    emb = emb_ref[...]
    z = emb[2:3] * t + emb[3:4] * jnp.cos(emb[0:1] * w + emb[1:2])
    fq = fq_ref[...]
    h = jnp.sin(fq * (_mm_f32(z, w1_ref[...]) + b1_ref[...]))
    h = jnp.sin(fq * (_mm_f32(h, w2_ref[...]) + b2_ref[...]))
    h = jnp.sin(fq * (_mm_f32(h, w3_ref[...]) + b3_ref[...]))
    k = _mm_f32(h, w4_ref[...]) * jnp.exp(-t * dl_ref[...])
    k = jnp.where(r == l, 0.0, k)
    k_ref[...] = k.astype(BF16)

    @pl.when(t_idx == 0)
    def _():
        s_ref[...] = jnp.zeros_like(s_ref)

    s_ref[...] += jnp.sum(jnp.abs(k), axis=0, keepdims=True)


def _filter_embedding_table():
    fr = np.linspace(1e-4, FILTER_BANDS - 1, FILTER_BANDS)
    emb = np.zeros((8, LANES), np.float32)
    emb[0, 1:1 + FILTER_BANDS] = fr
    emb[0, 1 + FILTER_BANDS:1 + 2 * FILTER_BANDS] = fr
    emb[1, 1 + FILTER_BANDS:1 + 2 * FILTER_BANDS] = math.pi / 2
    emb[2, 0] = 1.0
    emb[3, 1:1 + 2 * FILTER_BANDS] = 1.0
    return emb


def _hyena_filter(l, fw1, fb1, fw2, fb2, fw3, fb3, fw4, freq, *, tr):
    width = fw1.shape[1]
    e = fw4.shape[1] // 2
    w1p = jnp.zeros((LANES, width), F32).at[:fw1.shape[0]].set(fw1)
    deltas = jnp.abs(jnp.linspace(MIN_DECAY, MAX_DECAY, e, dtype=F32)).reshape(1, e)
    half = l // tr
    small = lambda shape: pl.BlockSpec(shape, lambda t: (0, 0))
    return pl.pallas_call(
        functools.partial(_filter_kernel, l=l, tr=tr),
        out_shape=(jax.ShapeDtypeStruct((2 * l, e), BF16), jax.ShapeDtypeStruct((1, e), F32)),
        grid=(2 * half,),
        in_specs=[
            small((8, LANES)), small((LANES, width)), small((1, width)),
            small((width, width)), small((1, width)), small((width, width)), small((1, width)),
            small((1, width)),
            pl.BlockSpec((width, e), lambda t: (0, t // half)),
            small((1, e)),
        ],
        out_specs=(pl.BlockSpec((tr, e), lambda t: (t, 0)), pl.BlockSpec((1, e), lambda t: (0, 0))),
        compiler_params=pltpu.CompilerParams(dimension_semantics=("arbitrary",)),
        name="hyena_filter",
    )(jnp.asarray(_filter_embedding_table()), w1p, fb1.reshape(1, -1), fw2, fb2.reshape(1, -1),
      fw3, fb3.reshape(1, -1), freq.reshape(1, -1), fw4, deltas)


@functools.lru_cache(maxsize=None)
def _four_step_consts(n1, n2):
    n = n1 * n2
    hh = n1 // 2
    k1 = np.arange(n1)
    c2 = np.arange(n2)

    def angles(sign, n1_count):
        nn = n2 * np.arange(n1_count)[None, None, :] + c2[:, None, None]
        return sign * 2.0 * np.pi * ((k1[None, :, None] * nn) % n) / n

    a = angles(-1.0, hh)
    c, s = np.cos(a), np.sin(a)
    g = np.zeros((n2, n1, 2, 2, hh))
    g[:, :, 0, 0], g[:, :, 0, 1], g[:, :, 1, 0], g[:, :, 1, 1] = c, -s, s, c
    g = g.reshape(n2, 2 * n1, n1)

    a = angles(-1.0, n1)
    gk = np.stack([np.cos(a), np.sin(a)], axis=2).reshape(n2, 2 * n1, n1)

    a = np.transpose(angles(1.0, hh), (0, 2, 1))
    c, s = np.cos(a) / n, np.sin(a) / n
    h = np.zeros((n2, 2, hh, n1, 2))
    h[:, 0, :, :, 0], h[:, 0, :, :, 1], h[:, 1, :, :, 0], h[:, 1, :, :, 1] = c, -s, s, c
    h = h.reshape(n2, n1, 2 * n1)

    phi = -2.0 * np.pi * ((c2[:, None] * c2[None, :]) % n2) / n2
    c, s = np.cos(phi), np.sin(phi)
    fb = np.block([[c, -s], [s, c]])
    fbi = np.block([[c, s], [-s, c]])
    f = lambda x: np.asarray(x, np.float32)
    return f(g), f(gk), f(h), f(fb), f(fbi)


@functools.lru_cache(maxsize=None)
def _direct_dft_consts(nc):
    hh = nc // 2
    k = np.arange(nc)
    a = -2.0 * np.pi * ((k[:, None] * np.arange(hh)[None, :]) % nc) / nc
    c, s = np.cos(a), np.sin(a)
    fc = np.block([[c, -s], [s, c]])
    a = -2.0 * np.pi * ((k[:, None] * k[None, :]) % nc) / nc
    fck = np.concatenate([np.cos(a), np.sin(a)], axis=0)
    a = 2.0 * np.pi * ((np.arange(hh)[:, None] * k[None, :]) % nc) / nc
    c, s = np.cos(a) / nc, np.sin(a) / nc
    fci = np.block([[c, -s], [s, c]])
    f = lambda x: np.asarray(x, np.float32)
    return f(fc), f(fck), f(fci)


def _colmm_kernel(g_ref, x_ref, o_ref, xs_ref, rs_ref, *, nb):
    xs_ref[...] = pltpu.einshape("abc->bac", x_ref[...])
    for q in range(nb):
        rs_ref[q] = _mm(g_ref[q], xs_ref[q]).astype(rs_ref.dtype)
    o_ref[...] = pltpu.einshape("abc->bac", rs_ref[...])


def _colmm(gm, x3, *, nb, cb, name):
    n2, r_out, r_in = gm.shape
    e = x3.shape[-1]
    return pl.pallas_call(
        functools.partial(_colmm_kernel, nb=nb),
        out_shape=jax.ShapeDtypeStruct((r_out, n2, e), BF16),
        grid=(n2 // nb, e // cb),
        in_specs=[
            pl.BlockSpec((nb, r_out, r_in), lambda s, j: (s, 0, 0)),
            pl.BlockSpec((r_in, nb, cb), lambda s, j: (0, s, j)),
        ],
        out_specs=pl.BlockSpec((r_out, nb, cb), lambda s, j: (0, s, j)),
        scratch_shapes=[pltpu.VMEM((nb, r_in, cb), BF16), pltpu.VMEM((nb, r_out, cb), BF16)],
        compiler_params=pltpu.CompilerParams(dimension_semantics=("parallel", "parallel")),
        name=name,
    )(gm, x3)


def _filter_spectrum_kernel(fb_ref, a_ref, s_ref, k_ref, *, kb):
    inv = 1.0 / s_ref[...]
    fb = fb_ref[...]
    for q in range(kb):
        k_ref[q] = (_mm(fb, a_ref[q]) * inv).astype(BF16)


def _filter_spectrum(fb, a3, s, *, kb, cb):
    n1, r, e = a3.shape
    return pl.pallas_call(
        functools.partial(_filter_spectrum_kernel, kb=kb),
        out_shape=jax.ShapeDtypeStruct((n1, r, e), BF16),
        grid=(n1 // kb, e // cb),
        in_specs=[
            pl.BlockSpec((r, r), lambda i, j: (0, 0)),
            pl.BlockSpec((kb, r, cb), lambda i, j: (i, 0, j)),
            pl.BlockSpec((1, cb), lambda i, j: (0, j)),
        ],
        out_specs=pl.BlockSpec((kb, r, cb), lambda i, j: (i, 0, j)),
        compiler_params=pltpu.CompilerParams(dimension_semantics=("parallel", "parallel")),
        name="hyena_filter_spectrum",
    )(fb, a3, s)


def _cmul(x, k, half):
    xr, xi = x[:half], x[half:]
    kr, ki = k[:half], k[half:]
    return jnp.concatenate([xr * kr - xi * ki, xr * ki + xi * kr], axis=0)


def _spectral_mul_kernel(fb_ref, fbi_ref, a_ref, k_ref, z_ref, *, kb):
    fb, fbi = fb_ref[...], fbi_ref[...]
    half = fb.shape[0] // 2
    for q in range(kb):
        x = _mm(fb, a_ref[q])
        y = _cmul(x, k_ref[q].astype(F32), half).astype(BF16)
        z_ref[q] = _mm(fbi, y).astype(BF16)


def _spectral_mul(fb, fbi, a3, k3, *, kb, cb):
    n1, r, e = a3.shape
    slab = pl.BlockSpec((kb, r, cb), lambda i, j: (i, 0, j))
    mat = pl.BlockSpec((r, r), lambda i, j: (0, 0))
    return pl.pallas_call(
        functools.partial(_spectral_mul_kernel, kb=kb),
        out_shape=jax.ShapeDtypeStruct((n1, r, e), BF16),
        grid=(n1 // kb, e // cb),
        in_specs=[mat, mat, slab, slab],
        out_specs=slab,
        compiler_params=pltpu.CompilerParams(dimension_semantics=("parallel", "parallel")),
        name="hyena_spectral_mul",
    )(fb, fbi, a3, k3)


def _ctx_conv_kernel(fc_ref, fck_ref, fci_ref, u_ref, k_ref, s_ref, y_ref):
    x = _mm(fc_ref[...], u_ref[...])
    ks = _mm(fck_ref[...], k_ref[...]) * (1.0 / s_ref[...])
    y = _cmul(x, ks, x.shape[0] // 2).astype(BF16)
    y_ref[...] = _mm(fci_ref[...], y).astype(BF16)


def _ctx_conv(fc, fck, fci, u2, k2, s, *, cb):
    nc, e = u2.shape
    full = lambda a: pl.BlockSpec(a.shape, lambda j: (0, 0))
    col = pl.BlockSpec((nc, cb), lambda j: (0, j))
    return pl.pallas_call(
        _ctx_conv_kernel,
        out_shape=jax.ShapeDtypeStruct((nc, e), BF16),
        grid=(e // cb,),
        in_specs=[full(fc), full(fck), full(fci), col, col, pl.BlockSpec((1, cb), lambda j: (0, j))],
        out_specs=col,
        compiler_params=pltpu.CompilerParams(dimension_semantics=("parallel",)),
        name="hyena_ctx_conv",
    )(fc, fck, fci, u2, k2, s)


def _long_conv_latent(u, k_raw, s, *, nb, cb, kb):
    b, l, e = u.shape
    n = 2 * l
    n1 = 1 << ((n.bit_length() - 1 + 1) // 2)
    n2 = n // n1
    g, gk, h, fb, fbi = (jnp.asarray(m).astype(BF16) for m in _four_step_consts(n1, n2))
    a = _colmm(g, u.reshape(n1, n2, e), nb=nb, cb=cb, name="hyena_dft_rows")
    ak = _colmm(gk, k_raw.reshape(n1, n2, e), nb=nb, cb=cb, name="hyena_filter_dft_rows")
    kspec = _filter_spectrum(fb, ak.reshape(n1, 2 * n2, e), s, kb=kb, cb=cb)
    z = _spectral_mul(fb, fbi, a.reshape(n1, 2 * n2, e), kspec, kb=kb, cb=cb)
    y = _colmm(h, z.reshape(2 * n1, n2, e), nb=nb, cb=cb, name="hyena_idft_rows")
    return y.reshape(b, l, e)


def _long_conv_ctx(u, k_raw, s, *, cb):
    b, c, e = u.shape
    fc, fck, fci = (jnp.asarray(m).astype(BF16) for m in _direct_dft_consts(2 * c))
    return _ctx_conv(fc, fck, fci, u.reshape(b * c, e), k_raw, s, cb=cb).reshape(b, c, e)


def _hy_out_kernel(y_ref, u_ref, g_ref, d_ref, w_ref, x_ref, gt_ref, o_ref):
    m = (y_ref[...].astype(F32) + u_ref[...].astype(F32) * d_ref[...]) * g_ref[...].astype(F32)
    o_ref[...] = x_ref[...] + gt_ref[...] * _mm(m.astype(BF16), w_ref[...])


def _hy_out(y, u, g, d_skip, w_bf, x, gt, *, tm):
    b, l, e = y.shape
    d = x.shape[-1]
    act = pl.BlockSpec((None, tm, e), lambda bb, i: (bb, i, 0))
    res = pl.BlockSpec((None, tm, d), lambda bb, i: (bb, i, 0))
    return pl.pallas_call(
        _hy_out_kernel,
        out_shape=jax.ShapeDtypeStruct(x.shape, F32),
        grid=(b, l // tm),
        in_specs=[act, act, act, pl.BlockSpec((1, e), lambda bb, i: (0, 0)),
                  pl.BlockSpec((e, d), lambda bb, i: (0, 0)), res,
                  pl.BlockSpec((None, 1, d), lambda bb, i: (bb, 0, 0))],
        out_specs=res,
        compiler_params=pltpu.CompilerParams(dimension_semantics=("parallel", "parallel")),
        name="hyena_out_proj",
    )(y, u, g, d_skip.reshape(1, e), w_bf, x, gt)


def _at_out_kernel(m_ref, w_ref, x_ref, gt_ref, o_ref):
    o_ref[...] = x_ref[...] + gt_ref[...] * _mm(m_ref[...], w_ref[...])


def _at_out(m, w_bf, x, gt, *, tm):
    b, l, e = m.shape
    d = x.shape[-1]
    res = pl.BlockSpec((None, tm, d), lambda bb, i: (bb, i, 0))
    return pl.pallas_call(
        _at_out_kernel,
        out_shape=jax.ShapeDtypeStruct(x.shape, F32),
        grid=(b, l // tm),
        in_specs=[pl.BlockSpec((None, tm, e), lambda bb, i: (bb, i, 0)),
                  pl.BlockSpec((e, d), lambda bb, i: (0, 0)), res,
                  pl.BlockSpec((None, 1, d), lambda bb, i: (bb, 0, 0))],
        out_specs=res,
        compiler_params=pltpu.CompilerParams(dimension_semantics=("parallel", "parallel")),
        name="attn_out_proj",
    )(m, w_bf, x, gt)


def _head_norm_rope(p, gain, cos, sin, scale):
    lane = lax.broadcasted_iota(jnp.int32, (1, HEAD_DIM), 1)
    first = (lane % (HEAD_DIM // 2)) < (HEAD_DIM // 4)
    outs = []
    for hd in range(p.shape[1] // HEAD_DIM):
        xh = p[:, hd * HEAD_DIM:(hd + 1) * HEAD_DIM]
        ms = jnp.mean(xh * xh, axis=-1, keepdims=True)
        yh = xh * lax.rsqrt(ms + EPS) * gain
        partner = jnp.where(first, pltpu.roll(yh, HEAD_DIM - HEAD_DIM // 4, 1),
                            pltpu.roll(yh, HEAD_DIM // 4, 1))
        outs.append((yh * cos + partner * sin) * scale)
    return jnp.concatenate(outs, axis=1)


def _at_proj_kernel(x_ref, g_ref, sc_ref, sh_ref, w_ref, qg_ref, kg_ref, cos_ref, sin_ref,
                    q_ref, k_ref, v_ref, sz_ref, hbuf):
    qd = q_ref.shape[-1]
    hbuf[...] = _norm_mod(x_ref[...], g_ref[...], sc_ref[...], sh_ref[...]).astype(BF16)

    def proj(lo):
        return _mm(hbuf[...], w_ref[:, lo:lo + GROUP_W])

    for lo in range(0, qd, GROUP_W):
        q_ref[:, lo:lo + GROUP_W] = _head_norm_rope(proj(lo), qg_ref[...], cos_ref[...], sin_ref[...],
                                                    HEAD_DIM ** -0.5 * LOG2E).astype(BF16)
    k_ref[...] = _head_norm_rope(proj(qd), kg_ref[...], cos_ref[...], sin_ref[...], 1.0).astype(BF16)
    v_ref[...] = proj(qd + GROUP_W).astype(BF16)
    for lo in range(0, qd, GROUP_W):
        p = proj(qd + 2 * GROUP_W + lo)
        sz_ref[:, lo:lo + GROUP_W] = (p * _sigmoid(p)).astype(BF16)


def _at_proj(x, g, sc, sh, w_bf, q_g, k_g, cos, sin, *, tm):
    b, l, d = x.shape
    kvd = GROUP_W
    qd = (w_bf.shape[1] - 2 * kvd) // 2
    const = lambda shape: pl.BlockSpec(shape, lambda bb, i: (0, 0), pipeline_mode=pl.Buffered(1))
    vec = pl.BlockSpec((None, 1, d), lambda bb, i: (bb, 0, 0))
    tab = pl.BlockSpec((tm, HEAD_DIM), lambda bb, i: (i, 0))
    wide = pl.BlockSpec((None, tm, qd), lambda bb, i: (bb, i, 0))
    narrow = pl.BlockSpec((None, tm, kvd), lambda bb, i: (bb, i, 0))
    return pl.pallas_call(
        _at_proj_kernel,
        out_shape=(jax.ShapeDtypeStruct((b, l, qd), BF16), jax.ShapeDtypeStruct((b, l, kvd), BF16),
                   jax.ShapeDtypeStruct((b, l, kvd), BF16), jax.ShapeDtypeStruct((b, l, qd), BF16)),
        grid=(b, l // tm),
        in_specs=[
            pl.BlockSpec((None, tm, d), lambda bb, i: (bb, i, 0)),
            const((1, d)), vec, vec, const(w_bf.shape),
            const((1, HEAD_DIM)), const((1, HEAD_DIM)), tab, tab,
        ],
        out_specs=(wide, narrow, narrow, wide),
        scratch_shapes=[pltpu.VMEM((tm, d), BF16)],
        compiler_params=pltpu.CompilerParams(dimension_semantics=("parallel", "parallel")),
        name="attn_in_proj",
    )(x, g.reshape(1, d), sc, sh, w_bf, q_g.reshape(1, -1), k_g.reshape(1, -1), cos, sin)


def _at_proj_kv_kernel(x_ref, g_ref, sc_ref, sh_ref, wk_ref, wv_ref, kg_ref, k_ref, v_ref):
    h = _norm_mod(x_ref[...], g_ref[...], sc_ref[...], sh_ref[...]).astype(BF16)
    p = _mm(h, wk_ref[...])
    gain = kg_ref[...]
    outs = []
    for hd in range(p.shape[1] // HEAD_DIM):
        xh = p[:, hd * HEAD_DIM:(hd + 1) * HEAD_DIM]
        ms = jnp.mean(xh * xh, axis=-1, keepdims=True)
        outs.append(xh * lax.rsqrt(ms + EPS) * gain)
    k_ref[...] = jnp.concatenate(outs, axis=1).astype(BF16)
    v_ref[...] = _mm(h, wv_ref[...]).astype(BF16)


def _at_proj_kv(x, g, sc, sh, w_bf, k_g, *, qd):
    b, c, d = x.shape
    kb = qd // GROUP_W
    vec = pl.BlockSpec((None, 1, d), lambda bb: (bb, 0, 0))
    out = pl.BlockSpec((None, c, GROUP_W), lambda bb: (bb, 0, 0))
    return pl.pallas_call(
        _at_proj_kv_kernel,
        out_shape=(jax.ShapeDtypeStruct((b, c, GROUP_W), BF16),) * 2,
        grid=(b,),
        in_specs=[
            pl.BlockSpec((None, c, d), lambda bb: (bb, 0, 0)),
            pl.BlockSpec((1, d), lambda bb: (0, 0)), vec, vec,
            pl.BlockSpec((d, GROUP_W), lambda bb: (0, kb)),
            pl.BlockSpec((d, GROUP_W), lambda bb: (0, kb + 1)),
            pl.BlockSpec((1, HEAD_DIM), lambda bb: (0, 0)),
        ],
        out_specs=(out, out),
        compiler_params=pltpu.CompilerParams(dimension_semantics=("parallel",)),
        name="attn_ctx_kv_proj",
    )(x, g.reshape(1, d), sc, sh, w_bf, w_bf, k_g.reshape(1, -1))


def _attn_kernel(q_ref, k_ref, v_ref, kc_ref, vc_ref, sz_ref, o_ref, vt_ref, acc_ref, sbuf,
                 *, tq, tk, n_k, c, nsub):
    @pl.when(pl.program_id(2) == 0)
    def _():
        ones = jnp.ones((VT_ROWS - HEAD_DIM, tk), BF16)
        for t in range(n_k):
            vt_ref[t, 0:HEAD_DIM, :] = v_ref[t * tk:(t + 1) * tk, :].astype(F32).T.astype(BF16)
            vt_ref[t, HEAD_DIM:, :] = ones
        vt_ref[n_k, 0:HEAD_DIM, 0:c] = vc_ref[...].astype(F32).T.astype(BF16)
        vt_ref[n_k, HEAD_DIM:, :] = ones

    qts = []
    for sb in range(nsub):
        q = q_ref[sb * tq:(sb + 1) * tq, :]
        qts.append(jnp.concatenate([q[:, gi * HEAD_DIM:(gi + 1) * HEAD_DIM].astype(F32).T.astype(BF16)
                                    for gi in range(GROUP)], axis=1))
    acc_ref[...] = jnp.zeros_like(acc_ref)

    def scores(kt, slot):
        for sb in range(nsub):
            sbuf[slot, sb, 0:kt.shape[0]] = _mm(kt, qts[sb])

    def consume(slot, vt, ms):
        out = []
        for sb in range(nsub):
            s = sbuf[slot, sb, 0:vt.shape[1]]
            m_new = jnp.maximum(ms[sb], jnp.max(s, axis=0, keepdims=True))
            p = jnp.exp2(s - m_new).astype(BF16)
            acc_ref[sb] = jnp.exp2(ms[sb] - m_new) * acc_ref[sb] + _mm(vt, p)
            out.append(m_new)
        return tuple(out)

    def key_chunk(t):
        return k_ref[pl.ds(pl.multiple_of(t * tk, tk), tk), :]

    def body(i, ms):
        t = 2 * i
        scores(key_chunk(t + 1), 1)
        ms = consume(0, vt_ref[t], ms)
        scores(key_chunk(t + 2), 0)
        return consume(1, vt_ref[t + 1], ms)

    scores(k_ref[0:tk, :], 0)
    ms = tuple(jnp.full((1, GROUP * tq), -jnp.inf, F32) for _ in range(nsub))
    ms = lax.fori_loop(0, n_k // 2 - 1, body, ms)
    scores(k_ref[(n_k - 1) * tk:n_k * tk, :], 1)
    ms = consume(0, vt_ref[n_k - 2], ms)
    scores(kc_ref[...], 0)
    ms = consume(1, vt_ref[n_k - 1], ms)
    consume(0, vt_ref[n_k, :, 0:c], ms)
    for sb in range(nsub):
        acc = acc_ref[sb]
        ot = acc[0:HEAD_DIM] * (1.0 / acc[HEAD_DIM:HEAD_DIM + 1])
        o = jnp.concatenate([ot[:, gi * tq:(gi + 1) * tq].T for gi in range(GROUP)], axis=1)
        rows = slice(sb * tq, (sb + 1) * tq)
        o_ref[rows, :] = (o * sz_ref[rows, :].astype(F32)).astype(BF16)


def _attention(q, k, v, kc, vc, sz, *, tq, tk, nsub):
    b, l, qd = q.shape
    c = kc.shape[1]
    n_kv = qd // GROUP_W
    n_k = l // tk
    assert n_k % 2 == 0 and n_k >= 2 and c <= tk
    qspec = pl.BlockSpec((None, nsub * tq, GROUP_W), lambda bb, hh, i: (bb, i, hh))
    kspec = pl.BlockSpec((None, l, HEAD_DIM), lambda bb, hh, i: (bb, 0, hh))
    cspec = pl.BlockSpec((None, c, HEAD_DIM), lambda bb, hh, i: (bb, 0, hh))
    return pl.pallas_call(
        functools.partial(_attn_kernel, tq=tq, tk=tk, n_k=n_k, c=c, nsub=nsub),
        out_shape=jax.ShapeDtypeStruct((b, l, qd), BF16),
        grid=(b, n_kv, l // (nsub * tq)),
        in_specs=[qspec, kspec, kspec, cspec, cspec, qspec],
        out_specs=qspec,
        scratch_shapes=[pltpu.VMEM((n_k + 1, VT_ROWS, tk), BF16),
                        pltpu.VMEM((nsub, VT_ROWS, GROUP * tq), F32),
                        pltpu.VMEM((2, nsub, tk, GROUP * tq), F32)],
        compiler_params=pltpu.CompilerParams(
            dimension_semantics=("parallel", "parallel", "arbitrary")),
        name="flash_attention",
    )(q, k, v, kc, vc, sz)


def _rope_tables(l):
    rows = l // GRID_W
    axis_dim = HEAD_DIM // 2
    row = jnp.repeat(jnp.arange(rows, dtype=F32), GRID_W)
    col = jnp.tile(jnp.arange(GRID_W, dtype=F32), rows)
    inv = 1.0 / (ROPE_THETA ** (jnp.arange(0, axis_dim, 2, dtype=F32) / axis_dim))
    ra, ca = row[:, None] * inv[None, :], col[:, None] * inv[None, :]
    cos = jnp.concatenate([jnp.cos(ra), jnp.cos(ra), jnp.cos(ca), jnp.cos(ca)], axis=-1)
    sin = jnp.concatenate([-jnp.sin(ra), jnp.sin(ra), -jnp.sin(ca), jnp.sin(ca)], axis=-1)
    return cos, sin


def _pick(n, target):
    t = min(n, target)
    while n % t:
        t //= 2
    return t


def kernel(x, c, ctx, c_ctx, norm_g, ada_w, ada_b,
           hy_w_in, hy_conv_w, hy_conv_b, hy_fw1, hy_fb1, hy_fw2, hy_fb2, hy_fw3, hy_fb3,
           hy_fw4, hy_freq, hy_d, hy_w_out,
           at_w_in, at_q_g, at_k_g, at_w_out):
    b, l, d = x.shape
    n_ctx = ctx.shape[1]
    e = hy_d.shape[-1]

    cv = jnp.concatenate([c, c_ctx[None], jnp.zeros((8 - b - 1, d), F32)], axis=0)
    mod = _modulation(cv, ada_w, ada_b)

    def lat(v):
        return v[:b, None, :]

    def cx(v):
        return jnp.broadcast_to(v[b][None, None, :], (b, 1, d))

    sh0, sc0, gt0 = jnp.split(mod[0], 3, axis=-1)
    w_in = hy_w_in[0].astype(BF16)
    w_out = hy_w_out[0].astype(BF16)
    fargs = (hy_fw1[0], hy_fb1[0], hy_fw2[0], hy_fb2[0], hy_fw3[0], hy_fb3[0], hy_fw4[0], hy_freq[0])
    tn = _pick(e, 256)
    cb = _pick(e, 512)

    u_lat, g_lat = _hy_proj(x, norm_g[0], lat(sc0), lat(sh0), w_in, hy_conv_w[0], hy_conv_b[0],
                            tm=_pick(l, 1024), tn=tn)
    u_ctx, g_ctx = _hy_proj(ctx, norm_g[0], cx(sc0), cx(sh0), w_in, hy_conv_w[0], hy_conv_b[0],
                            tm=_pick(n_ctx, 1024), tn=tn)
    k_lat, s_lat = _hyena_filter(l, *fargs, tr=_pick(l, 256))
    k_ctx, s_ctx = _hyena_filter(n_ctx, *fargs, tr=_pick(n_ctx, 256))
    y_lat = _long_conv_latent(u_lat, k_lat, s_lat, nb=16, cb=cb, kb=4)
    y_ctx = _long_conv_ctx(u_ctx, k_ctx, s_ctx, cb=cb)
    x1 = _hy_out(y_lat, u_lat, g_lat, hy_d[0], w_out, x, lat(gt0), tm=_pick(l, 512))
    ctx1 = _hy_out(y_ctx, u_ctx, g_ctx, hy_d[0], w_out, ctx, cx(gt0), tm=_pick(n_ctx, 512))

    sh1, sc1, gt1 = jnp.split(mod[1], 3, axis=-1)
    aw_in = at_w_in[0].astype(BF16)
    aw_out = at_w_out[0].astype(BF16)
    cos, sin = _rope_tables(l)
    q, k, v, sz = _at_proj(x1, norm_g[1], lat(sc1), lat(sh1), aw_in, at_q_g[0], at_k_g[0], cos, sin,
                           tm=_pick(l, 512))
    kc, vc = _at_proj_kv(ctx1, norm_g[1], cx(sc1), cx(sh1), aw_in, at_k_g[0], qd=q.shape[-1])
    om = _attention(q, k, v, kc, vc, sz, tq=_pick(l // 2, 256), tk=_pick(l // 2, 512), nsub=2)
    return _at_out(om, aw_out, x1, lat(gt1), tm=_pick(l, 512))
```

```python
import functools
import math

import numpy as np
import jax
import jax.numpy as jnp
from jax import lax
from jax.experimental import pallas as pl
from jax.experimental.pallas import tpu as pltpu

F32 = jnp.float32
BF16 = jnp.bfloat16
HIGHEST = lax.Precision.HIGHEST

EPS = 1e-6
HEAD_DIM = 128
GROUP = 4
GROUP_W = GROUP * HEAD_DIM
GRID_W = 64
ROPE_THETA = 10000.0
FILTER_BANDS = 16
DECAY_TARGET = 1e-2
MAX_DECAY = math.log(DECAY_TARGET) / 0.3
MIN_DECAY = math.log(DECAY_TARGET) / 1.5
HALO = 16
LANES = 128
VT_ROWS = HEAD_DIM + 16
LOG2E = math.log2(math.e)


def _sigmoid(x):
    return 1.0 / (1.0 + jnp.exp(-x))


def _mm(a, b):
    return jnp.dot(a, b, preferred_element_type=F32)


def _mm_f32(a, b):
    return jnp.dot(a, b, preferred_element_type=F32, precision=HIGHEST)


def _mod_kernel(c_ref, w_ref, b_ref, o_ref):
    cv = c_ref[...]
    o_ref[...] = _mm_f32(cv * _sigmoid(cv), w_ref[...]) + b_ref[...]


def _modulation(cv, ada_w, ada_b):
    depth, d, d3 = ada_w.shape
    rows = cv.shape[0]
    tn = d
    return pl.pallas_call(
        _mod_kernel,
        out_shape=jax.ShapeDtypeStruct((depth, rows, d3), F32),
        grid=(depth, d3 // tn),
        in_specs=[
            pl.BlockSpec((rows, d), lambda l, j: (0, 0)),
            pl.BlockSpec((None, d, tn), lambda l, j: (l, 0, j)),
            pl.BlockSpec((None, 1, tn), lambda l, j: (l, 0, j)),
        ],
        out_specs=pl.BlockSpec((None, rows, tn), lambda l, j: (l, 0, j)),
        name="modulation",
    )(cv, ada_w, ada_b.reshape(depth, 1, d3))


def _norm_mod(x, g, sc, sh):
    ms = jnp.mean(x * x, axis=-1, keepdims=True)
    return x * lax.rsqrt(ms + EPS) * (g * (1.0 + sc)) + sh


def _hy_proj_kernel(xm_ref, xp_ref, xn_ref, g_ref, sc_ref, sh_ref, w_ref, cw_ref, cb_ref,
                    u_ref, gate_ref, hbuf, *, tm, tn, n_i):
    i = pl.program_id(1)
    e = u_ref.shape[-1]
    g, sc, sh = g_ref[...], sc_ref[...], sh_ref[...]
    hbuf[HALO:HALO + tm, :] = _norm_mod(xm_ref[...], g, sc, sh).astype(BF16)
    hp = _norm_mod(xp_ref[...], g, sc, sh)
    hbuf[0:HALO, :] = jnp.where(i > 0, hp, 0.0).astype(BF16)
    hn = _norm_mod(xn_ref[...], g, sc, sh)
    hbuf[HALO + tm:, :] = jnp.where(i < n_i - 1, hn, 0.0).astype(BF16)

    def conv(lo):
        p = _mm(hbuf[...], w_ref[:, lo:lo + tn])
        cw = cw_ref[:, lo:lo + tn]
        rows = p.shape[0]
        prev = pltpu.roll(p, 1, 0)[HALO:HALO + tm]
        nxt = pltpu.roll(p, rows - 1, 0)[HALO:HALO + tm]
        return prev * cw[0:1] + p[HALO:HALO + tm] * cw[1:2] + nxt * cw[2:3] + cb_ref[:, lo:lo + tn]

    for j in range(e // tn):
        lo = j * tn
        u_ref[:, lo:lo + tn] = (conv(e + lo) * conv(2 * e + lo)).astype(BF16)
        z = _mm(hbuf[HALO:HALO + tm, :], w_ref[:, 3 * e + lo:3 * e + lo + tn])
        gate_ref[:, lo:lo + tn] = (conv(lo) * z * _sigmoid(z)).astype(BF16)


def _hy_proj(x, g, sc, sh, w_bf, conv_w, conv_b, *, tm, tn):
    b, l, d = x.shape
    e = w_bf.shape[1] // 4
    n_i = l // tm
    hb = tm // HALO
    last_hb = l // HALO - 1
    const = lambda shape: pl.BlockSpec(shape, lambda bb, i: (0, 0), pipeline_mode=pl.Buffered(1))
    vec = pl.BlockSpec((None, 1, d), lambda bb, i: (bb, 0, 0))
    out = pl.BlockSpec((None, tm, e), lambda bb, i: (bb, i, 0))
    return pl.pallas_call(
        functools.partial(_hy_proj_kernel, tm=tm, tn=tn, n_i=n_i),
        out_shape=(jax.ShapeDtypeStruct((b, l, e), BF16), jax.ShapeDtypeStruct((b, l, e), BF16)),
        grid=(b, n_i),
        in_specs=[
            pl.BlockSpec((None, tm, d), lambda bb, i: (bb, i, 0)),
            pl.BlockSpec((None, HALO, d), lambda bb, i: (bb, jnp.maximum(i * hb - 1, 0), 0)),
            pl.BlockSpec((None, HALO, d), lambda bb, i: (bb, jnp.minimum((i + 1) * hb, last_hb), 0)),
            const((1, d)), vec, vec,
            const((d, 4 * e)), const((3, 3 * e)), const((1, 3 * e)),
        ],
        out_specs=(out, out),
        scratch_shapes=[pltpu.VMEM((tm + 2 * HALO, d), BF16)],
        compiler_params=pltpu.CompilerParams(dimension_semantics=("parallel", "parallel")),
        name="hyena_in_proj",
    )(x, x, x, g.reshape(1, d), sc, sh, w_bf, conv_w, conv_b.reshape(1, -1))


def _split_bf16(a):
    hi = a.astype(BF16)
    return hi, (a - hi.astype(F32)).astype(BF16)


def _mm_split(w, a):
    w_hi, w_lo = _split_bf16(w)
    a_hi, a_lo = _split_bf16(a)
    return _mm(w_hi, a_hi) + _mm(w_lo, a_hi) + _mm(w_hi, a_lo)


def _filter_kernel(emb_ref, w1_ref, w2_ref, w3_ref, col_ref, w4_ref, dl_ref, k_ref, s_ref, *, l, tr):
    t_idx = pl.program_id(0)
    r_row = t_idx * tr + lax.broadcasted_iota(jnp.int32, (1, tr), 1)
    p_row = jnp.where(r_row < l, r_row, 2 * l - r_row).astype(F32)
    emb = emb_ref[...]
    z = (emb[:, 2:3] * (p_row / (l - 1.0))
         + emb[:, 3:4] * jnp.cos(emb[:, 0:1] * ((2.0 * math.pi) * p_row / l) + emb[:, 1:2]))
    col = col_ref[...]
    fq = col[:, 3:4]
    h = jnp.sin(fq * (_mm_split(w1_ref[...], z) + col[:, 0:1]))
    h = jnp.sin(fq * (_mm_split(w2_ref[...], h) + col[:, 1:2]))
    h = jnp.sin(fq * (_mm_split(w3_ref[...], h) + col[:, 2:3]))
    h_hi, h_lo = _split_bf16(h)
    w_hi, w_lo = _split_bf16(w4_ref[...])
    k = lax.dot_general(jnp.concatenate([h_hi, h_lo, h_hi], axis=0),
                        jnp.concatenate([w_hi, w_hi, w_lo], axis=0),
                        (((0,), (0,)), ((), ())), preferred_element_type=F32)
    r_col = t_idx * tr + lax.broadcasted_iota(jnp.int32, (tr, 1), 0)
    t_col = jnp.where(r_col < l, r_col, 2 * l - r_col).astype(F32) / (l - 1.0)
    k = jnp.where(r_col == l, 0.0, k * jnp.exp(-t_col * dl_ref[...]))
    k_ref[...] = k.astype(BF16)

    @pl.when(t_idx == 0)
    def _():
        s_ref[...] = jnp.zeros_like(s_ref)

    s_ref[...] += jnp.sum(jnp.abs(k), axis=0, keepdims=True)


EMB_ROWS = 48


def _filter_embedding_table():
    fr = np.linspace(1e-4, FILTER_BANDS - 1, FILTER_BANDS)
    emb = np.zeros((EMB_ROWS, 4), np.float32)
    emb[1:1 + FILTER_BANDS, 0] = fr
    emb[1 + FILTER_BANDS:1 + 2 * FILTER_BANDS, 0] = fr
    emb[1 + FILTER_BANDS:1 + 2 * FILTER_BANDS, 1] = math.pi / 2
    emb[0, 2] = 1.0
    emb[1:1 + 2 * FILTER_BANDS, 3] = 1.0
    return emb


def _hyena_filter(l, fw1, fb1, fw2, fb2, fw3, fb3, fw4, freq, *, tr):
    width = fw1.shape[1]
    e = fw4.shape[1] // 2
    w1t = jnp.zeros((width, EMB_ROWS), F32).at[:, :fw1.shape[0]].set(fw1.T)
    cols = jnp.stack([fb1, fb2, fb3, freq], axis=1)
    deltas = jnp.abs(jnp.linspace(MIN_DECAY, MAX_DECAY, e, dtype=F32)).reshape(1, e)
    half = l // tr
    small = lambda shape: pl.BlockSpec(shape, lambda t: (0, 0))
    return pl.pallas_call(
        functools.partial(_filter_kernel, l=l, tr=tr),
        out_shape=(jax.ShapeDtypeStruct((2 * l, e), BF16), jax.ShapeDtypeStruct((1, e), F32)),
        grid=(2 * half,),
        in_specs=[
            small((EMB_ROWS, 4)), small((width, EMB_ROWS)), small((width, width)), small((width, width)),
            small((width, 4)),
            pl.BlockSpec((width, e), lambda t: (0, t // half)),
            small((1, e)),
        ],
        out_specs=(pl.BlockSpec((tr, e), lambda t: (t, 0)), pl.BlockSpec((1, e), lambda t: (0, 0))),
        compiler_params=pltpu.CompilerParams(dimension_semantics=("arbitrary",)),
        name="hyena_filter",
    )(jnp.asarray(_filter_embedding_table()), w1t, fw2.T, fw3.T, cols, fw4, deltas)


@functools.lru_cache(maxsize=None)
def _four_step_consts(n1, n2):
    n = n1 * n2
    hh = n1 // 2
    k1 = np.arange(n1)
    c2 = np.arange(n2)

    def angles(sign, n1_count):
        nn = n2 * np.arange(n1_count)[None, None, :] + c2[:, None, None]
        return sign * 2.0 * np.pi * ((k1[None, :, None] * nn) % n) / n

    a = angles(-1.0, hh)
    c, s = np.cos(a), np.sin(a)
    g = np.zeros((n2, n1, 2, 2, hh))
    g[:, :, 0, 0], g[:, :, 0, 1], g[:, :, 1, 0], g[:, :, 1, 1] = c, -s, s, c
    g = g.reshape(n2, 2 * n1, n1)

    a = angles(-1.0, n1)
    gk = np.stack([np.cos(a), np.sin(a)], axis=2).reshape(n2, 2 * n1, n1)

    a = np.transpose(angles(1.0, hh), (0, 2, 1))
    c, s = np.cos(a) / n, np.sin(a) / n
    h = np.zeros((n2, 2, hh, n1, 2))
    h[:, 0, :, :, 0], h[:, 0, :, :, 1], h[:, 1, :, :, 0], h[:, 1, :, :, 1] = c, -s, s, c
    h = h.reshape(n2, n1, 2 * n1)

    phi = -2.0 * np.pi * ((c2[:, None] * c2[None, :]) % n2) / n2
    c, s = np.cos(phi), np.sin(phi)
    fb = np.block([[c, -s], [s, c]])
    fbi = np.block([[c, s], [-s, c]])
    f = lambda x: np.asarray(x, np.float32)
    return f(g), f(gk), f(h), f(fb), f(fbi)


@functools.lru_cache(maxsize=None)
def _direct_dft_consts(nc):
    hh = nc // 2
    k = np.arange(nc)
    a = -2.0 * np.pi * ((k[:, None] * np.arange(hh)[None, :]) % nc) / nc
    c, s = np.cos(a), np.sin(a)
    fc = np.block([[c, -s], [s, c]])
    a = -2.0 * np.pi * ((k[:, None] * k[None, :]) % nc) / nc
    fck = np.concatenate([np.cos(a), np.sin(a)], axis=0)
    a = 2.0 * np.pi * ((np.arange(hh)[:, None] * k[None, :]) % nc) / nc
    c, s = np.cos(a) / nc, np.sin(a) / nc
    fci = np.block([[c, -s], [s, c]])
    f = lambda x: np.asarray(x, np.float32)
    return f(fc), f(fck), f(fci)


def _colmm_kernel(g_ref, x_ref, o_ref, xs_ref, rs_ref, *, nb):
    xs_ref[...] = pltpu.einshape("abc->bac", x_ref[...])
    for q in range(nb):
        rs_ref[q] = _mm(g_ref[q], xs_ref[q]).astype(rs_ref.dtype)
    o_ref[...] = pltpu.einshape("abc->bac", rs_ref[...])


def _colmm(gm, x3, *, nb, cb, name):
    n2, r_out, r_in = gm.shape
    e = x3.shape[-1]
    return pl.pallas_call(
        functools.partial(_colmm_kernel, nb=nb),
        out_shape=jax.ShapeDtypeStruct((r_out, n2, e), BF16),
        grid=(n2 // nb, e // cb),
        in_specs=[
            pl.BlockSpec((nb, r_out, r_in), lambda s, j: (s, 0, 0)),
            pl.BlockSpec((r_in, nb, cb), lambda s, j: (0, s, j)),
        ],
        out_specs=pl.BlockSpec((r_out, nb, cb), lambda s, j: (0, s, j)),
        scratch_shapes=[pltpu.VMEM((nb, r_in, cb), BF16), pltpu.VMEM((nb, r_out, cb), BF16)],
        compiler_params=pltpu.CompilerParams(dimension_semantics=("parallel", "parallel")),
        name=name,
    )(gm, x3)


def _cmul(x, k, half):
    xr, xi = x[:half], x[half:]
    kr, ki = k[:half], k[half:]
    return jnp.concatenate([xr * kr - xi * ki, xr * ki + xi * kr], axis=0)


def _spectral_mul_kernel(fb_ref, fbi_ref, a_ref, ak_ref, s_ref, z_ref, *, kb):
    fb, fbi = fb_ref[...], fbi_ref[...]
    half = fb.shape[0] // 2
    inv = 1.0 / s_ref[...]
    xs = [_mm(fb, a_ref[q]) for q in range(kb)]
    ks = [_mm(fb, ak_ref[q]) * inv for q in range(kb)]
    ys = [_cmul(xs[q], ks[q], half).astype(BF16) for q in range(kb)]
    for q in range(kb):
        z_ref[q] = _mm(fbi, ys[q]).astype(BF16)


def _spectral_mul(fb, fbi, a3, ak3, s, *, kb, cb):
    n1, r, e = a3.shape
    slab = pl.BlockSpec((kb, r, cb), lambda i, j: (i, 0, j))
    mat = pl.BlockSpec((r, r), lambda i, j: (0, 0))
    return pl.pallas_call(
        functools.partial(_spectral_mul_kernel, kb=kb),
        out_shape=jax.ShapeDtypeStruct((n1, r, e), BF16),
        grid=(n1 // kb, e // cb),
        in_specs=[mat, mat, slab, slab, pl.BlockSpec((1, cb), lambda i, j: (0, j))],
        out_specs=slab,
        compiler_params=pltpu.CompilerParams(dimension_semantics=("parallel", "parallel")),
        name="hyena_spectral_mul",
    )(fb, fbi, a3, ak3, s)


def _ctx_conv_kernel(fc_ref, fck_ref, fci_ref, u_ref, k_ref, s_ref, y_ref):
    x = _mm(fc_ref[...], u_ref[...])
    ks = _mm(fck_ref[...], k_ref[...]) * (1.0 / s_ref[...])
    y = _cmul(x, ks, x.shape[0] // 2).astype(BF16)
    y_ref[...] = _mm(fci_ref[...], y).astype(BF16)


def _ctx_conv(fc, fck, fci, u2, k2, s, *, cb):
    nc, e = u2.shape
    full = lambda a: pl.BlockSpec(a.shape, lambda j: (0, 0))
    col = pl.BlockSpec((nc, cb), lambda j: (0, j))
    return pl.pallas_call(
        _ctx_conv_kernel,
        out_shape=jax.ShapeDtypeStruct((nc, e), BF16),
        grid=(e // cb,),
        in_specs=[full(fc), full(fck), full(fci), col, col, pl.BlockSpec((1, cb), lambda j: (0, j))],
        out_specs=col,
        compiler_params=pltpu.CompilerParams(dimension_semantics=("parallel",)),
        name="hyena_ctx_conv",
    )(fc, fck, fci, u2, k2, s)


def _long_conv_latent(u, k_raw, s, *, nb, cb, kb):
    b, l, e = u.shape
    n = 2 * l
    n1 = 1 << ((n.bit_length() - 1 + 1) // 2)
    n2 = n // n1
    g, gk, h, fb, fbi = (jnp.asarray(m).astype(BF16) for m in _four_step_consts(n1, n2))
    a = _colmm(g, u.reshape(n1, n2, e), nb=nb, cb=cb, name="hyena_dft_rows")
    ak = _colmm(gk, k_raw.reshape(n1, n2, e), nb=nb, cb=cb, name="hyena_filter_dft_rows")
    z = _spectral_mul(fb, fbi, a.reshape(n1, 2 * n2, e), ak.reshape(n1, 2 * n2, e), s, kb=kb, cb=cb)
    y = _colmm(h, z.reshape(2 * n1, n2, e), nb=nb, cb=cb, name="hyena_idft_rows")
    return y.reshape(b, l, e)


def _long_conv_ctx(u, k_raw, s, *, cb):
    b, c, e = u.shape
    fc, fck, fci = (jnp.asarray(m).astype(BF16) for m in _direct_dft_consts(2 * c))
    return _ctx_conv(fc, fck, fci, u.reshape(b * c, e), k_raw, s, cb=cb).reshape(b, c, e)


def _hy_out_kernel(y_ref, u_ref, g_ref, d_ref, w_ref, x_ref, gt_ref, o_ref):
    m = (y_ref[...].astype(F32) + u_ref[...].astype(F32) * d_ref[...]) * g_ref[...].astype(F32)
    o_ref[...] = x_ref[...] + gt_ref[...] * _mm(m.astype(BF16), w_ref[...])


def _hy_out(y, u, g, d_skip, w_bf, x, gt, *, tm):
    b, l, e = y.shape
    d = x.shape[-1]
    act = pl.BlockSpec((None, tm, e), lambda bb, i: (bb, i, 0))
    res = pl.BlockSpec((None, tm, d), lambda bb, i: (bb, i, 0))
    return pl.pallas_call(
        _hy_out_kernel,
        out_shape=jax.ShapeDtypeStruct(x.shape, F32),
        grid=(b, l // tm),
        in_specs=[act, act, act, pl.BlockSpec((1, e), lambda bb, i: (0, 0)),
                  pl.BlockSpec((e, d), lambda bb, i: (0, 0)), res,
                  pl.BlockSpec((None, 1, d), lambda bb, i: (bb, 0, 0))],
        out_specs=res,
        compiler_params=pltpu.CompilerParams(dimension_semantics=("parallel", "parallel")),
        name="hyena_out_proj",
    )(y, u, g, d_skip.reshape(1, e), w_bf, x, gt)


def _at_out_kernel(m_ref, w_ref, x_ref, gt_ref, o_ref):
    o_ref[...] = x_ref[...] + gt_ref[...] * _mm(m_ref[...], w_ref[...])


def _at_out(m, w_bf, x, gt, *, tm):
    b, l, e = m.shape
    d = x.shape[-1]
    res = pl.BlockSpec((None, tm, d), lambda bb, i: (bb, i, 0))
    return pl.pallas_call(
        _at_out_kernel,
        out_shape=jax.ShapeDtypeStruct(x.shape, F32),
        grid=(b, l // tm),
        in_specs=[pl.BlockSpec((None, tm, e), lambda bb, i: (bb, i, 0)),
                  pl.BlockSpec((e, d), lambda bb, i: (0, 0)), res,
                  pl.BlockSpec((None, 1, d), lambda bb, i: (bb, 0, 0))],
        out_specs=res,
        compiler_params=pltpu.CompilerParams(dimension_semantics=("parallel", "parallel")),
        name="attn_out_proj",
    )(m, w_bf, x, gt)


def _head_norm_rope(p, gain, cos, sin, scale):
    lane = lax.broadcasted_iota(jnp.int32, (1, HEAD_DIM), 1)
    first = (lane % (HEAD_DIM // 2)) < (HEAD_DIM // 4)
    outs = []
    for hd in range(p.shape[1] // HEAD_DIM):
        xh = p[:, hd * HEAD_DIM:(hd + 1) * HEAD_DIM]
        ms = jnp.mean(xh * xh, axis=-1, keepdims=True)
        yh = xh * lax.rsqrt(ms + EPS) * gain
        partner = jnp.where(first, pltpu.roll(yh, HEAD_DIM - HEAD_DIM // 4, 1),
                            pltpu.roll(yh, HEAD_DIM // 4, 1))
        outs.append((yh * cos + partner * sin) * scale)
    return jnp.concatenate(outs, axis=1)


def _at_proj_kernel(x_ref, g_ref, sc_ref, sh_ref, w_ref, qg_ref, kg_ref, cos_ref, sin_ref,
                    q_ref, k_ref, v_ref, sz_ref, hbuf):
    qd = q_ref.shape[-1]
    hbuf[...] = _norm_mod(x_ref[...], g_ref[...], sc_ref[...], sh_ref[...]).astype(BF16)

    def proj(lo):
        return _mm(hbuf[...], w_ref[:, lo:lo + GROUP_W])

    for lo in range(0, qd, GROUP_W):
        q_ref[:, lo:lo + GROUP_W] = _head_norm_rope(proj(lo), qg_ref[...], cos_ref[...], sin_ref[...],
                                                    HEAD_DIM ** -0.5 * LOG2E).astype(BF16)
    k_ref[...] = _head_norm_rope(proj(qd), kg_ref[...], cos_ref[...], sin_ref[...], 1.0).astype(BF16)
    v_ref[...] = proj(qd + GROUP_W).astype(BF16)
    for lo in range(0, qd, GROUP_W):
        p = proj(qd + 2 * GROUP_W + lo)
        sz_ref[:, lo:lo + GROUP_W] = (p * _sigmoid(p)).astype(BF16)


def _at_proj(x, g, sc, sh, w_bf, q_g, k_g, cos, sin, *, tm):
    b, l, d = x.shape
    kvd = GROUP_W
    qd = (w_bf.shape[1] - 2 * kvd) // 2
    const = lambda shape: pl.BlockSpec(shape, lambda bb, i: (0, 0), pipeline_mode=pl.Buffered(1))
    vec = pl.BlockSpec((None, 1, d), lambda bb, i: (bb, 0, 0))
    tab = pl.BlockSpec((tm, HEAD_DIM), lambda bb, i: (i, 0))
    wide = pl.BlockSpec((None, tm, qd), lambda bb, i: (bb, i, 0))
    narrow = pl.BlockSpec((None, tm, kvd), lambda bb, i: (bb, i, 0))
    return pl.pallas_call(
        _at_proj_kernel,
        out_shape=(jax.ShapeDtypeStruct((b, l, qd), BF16), jax.ShapeDtypeStruct((b, l, kvd), BF16),
                   jax.ShapeDtypeStruct((b, l, kvd), BF16), jax.ShapeDtypeStruct((b, l, qd), BF16)),
        grid=(b, l // tm),
        in_specs=[
            pl.BlockSpec((None, tm, d), lambda bb, i: (bb, i, 0)),
            const((1, d)), vec, vec, const(w_bf.shape),
            const((1, HEAD_DIM)), const((1, HEAD_DIM)), tab, tab,
        ],
        out_specs=(wide, narrow, narrow, wide),
        scratch_shapes=[pltpu.VMEM((tm, d), BF16)],
        compiler_params=pltpu.CompilerParams(dimension_semantics=("parallel", "parallel")),
        name="attn_in_proj",
    )(x, g.reshape(1, d), sc, sh, w_bf, q_g.reshape(1, -1), k_g.reshape(1, -1), cos, sin)


def _at_proj_kv_kernel(x_ref, g_ref, sc_ref, sh_ref, wk_ref, wv_ref, kg_ref, k_ref, v_ref):
    h = _norm_mod(x_ref[...], g_ref[...], sc_ref[...], sh_ref[...]).astype(BF16)
    p = _mm(h, wk_ref[...])
    gain = kg_ref[...]
    outs = []
    for hd in range(p.shape[1] // HEAD_DIM):
        xh = p[:, hd * HEAD_DIM:(hd + 1) * HEAD_DIM]
        ms = jnp.mean(xh * xh, axis=-1, keepdims=True)
        outs.append(xh * lax.rsqrt(ms + EPS) * gain)
    k_ref[...] = jnp.concatenate(outs, axis=1).astype(BF16)
    v_ref[...] = _mm(h, wv_ref[...]).astype(BF16)


def _at_proj_kv(x, g, sc, sh, w_bf, k_g, *, qd):
    b, c, d = x.shape
    kb = qd // GROUP_W
    vec = pl.BlockSpec((None, 1, d), lambda bb: (bb, 0, 0))
    out = pl.BlockSpec((None, c, GROUP_W), lambda bb: (bb, 0, 0))
    return pl.pallas_call(
        _at_proj_kv_kernel,
        out_shape=(jax.ShapeDtypeStruct((b, c, GROUP_W), BF16),) * 2,
        grid=(b,),
        in_specs=[
            pl.BlockSpec((None, c, d), lambda bb: (bb, 0, 0)),
            pl.BlockSpec((1, d), lambda bb: (0, 0)), vec, vec,
            pl.BlockSpec((d, GROUP_W), lambda bb: (0, kb)),
            pl.BlockSpec((d, GROUP_W), lambda bb: (0, kb + 1)),
            pl.BlockSpec((1, HEAD_DIM), lambda bb: (0, 0)),
        ],
        out_specs=(out, out),
        compiler_params=pltpu.CompilerParams(dimension_semantics=("parallel",)),
        name="attn_ctx_kv_proj",
    )(x, g.reshape(1, d), sc, sh, w_bf, w_bf, k_g.reshape(1, -1))


def _attn_kernel(q_ref, k_ref, v_ref, kc_ref, vc_ref, sz_ref, o_ref, vt_ref, acc_ref, sbuf,
                 *, tq, tk, n_k, c, nsub):
    @pl.when(pl.program_id(2) == 0)
    def _():
        ones = jnp.ones((VT_ROWS - HEAD_DIM, tk), BF16)
        for t in range(n_k):
            vt_ref[t, 0:HEAD_DIM, :] = v_ref[t * tk:(t + 1) * tk, :].astype(F32).T.astype(BF16)
            vt_ref[t, HEAD_DIM:, :] = ones
        vt_ref[n_k, 0:HEAD_DIM, 0:c] = vc_ref[...].astype(F32).T.astype(BF16)
        vt_ref[n_k, HEAD_DIM:, :] = ones

    qts = []
    for sb in range(nsub):
        q = q_ref[sb * tq:(sb + 1) * tq, :]
        qts.append(jnp.concatenate([q[:, gi * HEAD_DIM:(gi + 1) * HEAD_DIM].astype(F32).T.astype(BF16)
                                    for gi in range(GROUP)], axis=1))
    acc_ref[...] = jnp.zeros_like(acc_ref)

    def scores(kt, slot):
        for sb in range(nsub):
            sbuf[slot, sb, 0:kt.shape[0]] = _mm(kt, qts[sb])

    def consume(slot, vt, ms):
        out = []
        for sb in range(nsub):
            s = sbuf[slot, sb, 0:vt.shape[1]]
            m_new = jnp.maximum(ms[sb], jnp.max(s, axis=0, keepdims=True))
            p = jnp.exp2(s - m_new).astype(BF16)
            acc_ref[sb] = jnp.exp2(ms[sb] - m_new) * acc_ref[sb] + _mm(vt, p)
            out.append(m_new)
        return tuple(out)

    def key_chunk(t):
        return k_ref[pl.ds(pl.multiple_of(t * tk, tk), tk), :]

    def body(i, ms):
        t = 2 * i
        scores(key_chunk(t + 1), 1)
        ms = consume(0, vt_ref[t], ms)
        scores(key_chunk(t + 2), 0)
        return consume(1, vt_ref[t + 1], ms)

    scores(k_ref[0:tk, :], 0)
    ms = tuple(jnp.full((1, GROUP * tq), -jnp.inf, F32) for _ in range(nsub))
    ms = lax.fori_loop(0, n_k // 2 - 1, body, ms)
    scores(k_ref[(n_k - 1) * tk:n_k * tk, :], 1)
    ms = consume(0, vt_ref[n_k - 2], ms)
    scores(kc_ref[...], 0)
    ms = consume(1, vt_ref[n_k - 1], ms)
    consume(0, vt_ref[n_k, :, 0:c], ms)
    for sb in range(nsub):
        acc = acc_ref[sb]
        ot = acc[0:HEAD_DIM] * (1.0 / acc[HEAD_DIM:HEAD_DIM + 1])
        o = jnp.concatenate([ot[:, gi * tq:(gi + 1) * tq].T for gi in range(GROUP)], axis=1)
        rows = slice(sb * tq, (sb + 1) * tq)
        o_ref[rows, :] = (o * sz_ref[rows, :].astype(F32)).astype(BF16)


def _attention(q, k, v, kc, vc, sz, *, tq, tk, nsub):
    b, l, qd = q.shape
    c = kc.shape[1]
    n_kv = qd // GROUP_W
    n_k = l // tk
    assert n_k % 2 == 0 and n_k >= 2 and c <= tk
    qspec = pl.BlockSpec((None, nsub * tq, GROUP_W), lambda bb, hh, i: (bb, i, hh))
    kspec = pl.BlockSpec((None, l, HEAD_DIM), lambda bb, hh, i: (bb, 0, hh))
    cspec = pl.BlockSpec((None, c, HEAD_DIM), lambda bb, hh, i: (bb, 0, hh))
    return pl.pallas_call(
        functools.partial(_attn_kernel, tq=tq, tk=tk, n_k=n_k, c=c, nsub=nsub),
        out_shape=jax.ShapeDtypeStruct((b, l, qd), BF16),
        grid=(b, n_kv, l // (nsub * tq)),
        in_specs=[qspec, kspec, kspec, cspec, cspec, qspec],
        out_specs=qspec,
        scratch_shapes=[pltpu.VMEM((n_k + 1, VT_ROWS, tk), BF16),
                        pltpu.VMEM((nsub, VT_ROWS, GROUP * tq), F32),
                        pltpu.VMEM((2, nsub, tk, GROUP * tq), F32)],
        compiler_params=pltpu.CompilerParams(
            dimension_semantics=("parallel", "parallel", "arbitrary")),
        name="flash_attention",
    )(q, k, v, kc, vc, sz)


def _rope_tables(l):
    rows = l // GRID_W
    axis_dim = HEAD_DIM // 2
    row = jnp.repeat(jnp.arange(rows, dtype=F32), GRID_W)
    col = jnp.tile(jnp.arange(GRID_W, dtype=F32), rows)
    inv = 1.0 / (ROPE_THETA ** (jnp.arange(0, axis_dim, 2, dtype=F32) / axis_dim))
    ra, ca = row[:, None] * inv[None, :], col[:, None] * inv[None, :]
    cos = jnp.concatenate([jnp.cos(ra), jnp.cos(ra), jnp.cos(ca), jnp.cos(ca)], axis=-1)
    sin = jnp.concatenate([-jnp.sin(ra), jnp.sin(ra), -jnp.sin(ca), jnp.sin(ca)], axis=-1)
    return cos, sin


def _pick(n, target):
    t = min(n, target)
    while n % t:
        t //= 2
    return t


def kernel(x, c, ctx, c_ctx, norm_g, ada_w, ada_b,
           hy_w_in, hy_conv_w, hy_conv_b, hy_fw1, hy_fb1, hy_fw2, hy_fb2, hy_fw3, hy_fb3,
           hy_fw4, hy_freq, hy_d, hy_w_out,
           at_w_in, at_q_g, at_k_g, at_w_out):
    b, l, d = x.shape
    n_ctx = ctx.shape[1]
    e = hy_d.shape[-1]

    cv = jnp.concatenate([c, c_ctx[None], jnp.zeros((8 - b - 1, d), F32)], axis=0)
    mod = _modulation(cv, ada_w, ada_b)

    def lat(v):
        return v[:b, None, :]

    def cx(v):
        return jnp.broadcast_to(v[b][None, None, :], (b, 1, d))

    sh0, sc0, gt0 = jnp.split(mod[0], 3, axis=-1)
    w_in = hy_w_in[0].astype(BF16)
    w_out = hy_w_out[0].astype(BF16)
    fargs = (hy_fw1[0], hy_fb1[0], hy_fw2[0], hy_fb2[0], hy_fw3[0], hy_fb3[0], hy_fw4[0], hy_freq[0])
    tn = _pick(e, 256)
    cb = _pick(e, 512)

    u_lat, g_lat = _hy_proj(x, norm_g[0], lat(sc0), lat(sh0), w_in, hy_conv_w[0], hy_conv_b[0],
                            tm=_pick(l, 1024), tn=tn)
    u_ctx, g_ctx = _hy_proj(ctx, norm_g[0], cx(sc0), cx(sh0), w_in, hy_conv_w[0], hy_conv_b[0],
                            tm=_pick(n_ctx, 1024), tn=tn)
    k_lat, s_lat = _hyena_filter(l, *fargs, tr=_pick(l, 256))
    k_ctx, s_ctx = _hyena_filter(n_ctx, *fargs, tr=_pick(n_ctx, 256))
    y_lat = _long_conv_latent(u_lat, k_lat, s_lat, nb=16, cb=cb, kb=4)
    y_ctx = _long_conv_ctx(u_ctx, k_ctx, s_ctx, cb=cb)
    x1 = _hy_out(y_lat, u_lat, g_lat, hy_d[0], w_out, x, lat(gt0), tm=_pick(l, 512))
    ctx1 = _hy_out(y_ctx, u_ctx, g_ctx, hy_d[0], w_out, ctx, cx(gt0), tm=_pick(n_ctx, 512))

    sh1, sc1, gt1 = jnp.split(mod[1], 3, axis=-1)
    aw_in = at_w_in[0].astype(BF16)
    aw_out = at_w_out[0].astype(BF16)
    cos, sin = _rope_tables(l)
    q, k, v, sz = _at_proj(x1, norm_g[1], lat(sc1), lat(sh1), aw_in, at_q_g[0], at_k_g[0], cos, sin,
                           tm=_pick(l, 512))
    kc, vc = _at_proj_kv(ctx1, norm_g[1], cx(sc1), cx(sh1), aw_in, at_k_g[0], qd=q.shape[-1])
    om = _attention(q, k, v, kc, vc, sz, tq=_pick(l // 2, 256), tk=_pick(l // 2, 512), nsub=2)
    return _at_out(om, aw_out, x1, lat(gt1), tm=_pick(l, 512))
```

```python
import functools
import math

import numpy as np
import jax
import jax.numpy as jnp
from jax import lax
from jax.experimental import pallas as pl
from jax.experimental.pallas import tpu as pltpu

F32 = jnp.float32
BF16 = jnp.bfloat16
HIGHEST = lax.Precision.HIGHEST

EPS = 1e-6
HEAD_DIM = 128
GROUP = 4
GROUP_W = GROUP * HEAD_DIM
GRID_W = 64
ROPE_THETA = 10000.0
FILTER_BANDS = 16
DECAY_TARGET = 1e-2
MAX_DECAY = math.log(DECAY_TARGET) / 0.3
MIN_DECAY = math.log(DECAY_TARGET) / 1.5
HALO = 16
LANES = 128
VT_ROWS = HEAD_DIM + 16
LOG2E = math.log2(math.e)
KV_UNROLL = 4


def _sigmoid(x):
    return 1.0 / (1.0 + jnp.exp(-x))


def _mm(a, b):
    return jnp.dot(a, b, preferred_element_type=F32)


def _mm_f32(a, b):
    return jnp.dot(a, b, preferred_element_type=F32, precision=HIGHEST)


def _mod_kernel(c_ref, w_ref, b_ref, o_ref):
    cv = c_ref[...]
    o_ref[...] = _mm_f32(cv * _sigmoid(cv), w_ref[...]) + b_ref[...]


def _modulation(cv, ada_w, ada_b):
    depth, d, d3 = ada_w.shape
    rows = cv.shape[0]
    tn = d
    return pl.pallas_call(
        _mod_kernel,
        out_shape=jax.ShapeDtypeStruct((depth, rows, d3), F32),
        grid=(depth, d3 // tn),
        in_specs=[
            pl.BlockSpec((rows, d), lambda l, j: (0, 0)),
            pl.BlockSpec((None, d, tn), lambda l, j: (l, 0, j)),
            pl.BlockSpec((None, 1, tn), lambda l, j: (l, 0, j)),
        ],
        out_specs=pl.BlockSpec((None, rows, tn), lambda l, j: (l, 0, j)),
        name="modulation",
    )(cv, ada_w, ada_b.reshape(depth, 1, d3))


def _norm_mod(x, g, sc, sh):
    ms = jnp.mean(x * x, axis=-1, keepdims=True)
    return x * lax.rsqrt(ms + EPS) * (g * (1.0 + sc)) + sh


def _hy_proj_kernel(xm_ref, xp_ref, xn_ref, g_ref, sc_ref, sh_ref, w_ref, cw_ref, cb_ref,
                    u_ref, gate_ref, hbuf, *, tm, tn, n_i):
    i = pl.program_id(1)
    e = u_ref.shape[-1]
    g, sc, sh = g_ref[...], sc_ref[...], sh_ref[...]
    hbuf[HALO:HALO + tm, :] = _norm_mod(xm_ref[...], g, sc, sh).astype(BF16)
    hp = _norm_mod(xp_ref[...], g, sc, sh)
    hbuf[0:HALO, :] = jnp.where(i > 0, hp, 0.0).astype(BF16)
    hn = _norm_mod(xn_ref[...], g, sc, sh)
    hbuf[HALO + tm:, :] = jnp.where(i < n_i - 1, hn, 0.0).astype(BF16)

    def conv(lo):
        p = _mm(hbuf[...], w_ref[:, lo:lo + tn])
        cw = cw_ref[:, lo:lo + tn]
        rows = p.shape[0]
        prev = pltpu.roll(p, 1, 0)[HALO:HALO + tm]
        nxt = pltpu.roll(p, rows - 1, 0)[HALO:HALO + tm]
        return prev * cw[0:1] + p[HALO:HALO + tm] * cw[1:2] + nxt * cw[2:3] + cb_ref[:, lo:lo + tn]

    for j in range(e // tn):
        lo = j * tn
        u_ref[:, lo:lo + tn] = (conv(e + lo) * conv(2 * e + lo)).astype(BF16)
        z = _mm(hbuf[HALO:HALO + tm, :], w_ref[:, 3 * e + lo:3 * e + lo + tn])
        gate_ref[:, lo:lo + tn] = (conv(lo) * z * _sigmoid(z)).astype(BF16)


def _hy_proj(x, g, sc, sh, w_bf, conv_w, conv_b, *, tm, tn):
    b, l, d = x.shape
    e = w_bf.shape[1] // 4
    n_i = l // tm
    hb = tm // HALO
    last_hb = l // HALO - 1
    const = lambda shape: pl.BlockSpec(shape, lambda bb, i: (0, 0), pipeline_mode=pl.Buffered(1))
    vec = pl.BlockSpec((None, 1, d), lambda bb, i: (bb, 0, 0))
    out = pl.BlockSpec((None, tm, e), lambda bb, i: (bb, i, 0))
    return pl.pallas_call(
        functools.partial(_hy_proj_kernel, tm=tm, tn=tn, n_i=n_i),
        out_shape=(jax.ShapeDtypeStruct((b, l, e), BF16), jax.ShapeDtypeStruct((b, l, e), BF16)),
        grid=(b, n_i),
        in_specs=[
            pl.BlockSpec((None, tm, d), lambda bb, i: (bb, i, 0)),
            pl.BlockSpec((None, HALO, d), lambda bb, i: (bb, jnp.maximum(i * hb - 1, 0), 0)),
            pl.BlockSpec((None, HALO, d), lambda bb, i: (bb, jnp.minimum((i + 1) * hb, last_hb), 0)),
            const((1, d)), vec, vec,
            const((d, 4 * e)), const((3, 3 * e)), const((1, 3 * e)),
        ],
        out_specs=(out, out),
        scratch_shapes=[pltpu.VMEM((tm + 2 * HALO, d), BF16)],
        compiler_params=pltpu.CompilerParams(dimension_semantics=("parallel", "parallel")),
        name="hyena_in_proj",
    )(x, x, x, g.reshape(1, d), sc, sh, w_bf, conv_w, conv_b.reshape(1, -1))


def _split_bf16(a):
    hi = a.astype(BF16)
    return hi, (a - hi.astype(F32)).astype(BF16)


def _mm_split(w, a):
    w_hi, w_lo = _split_bf16(w)
    a_hi, a_lo = _split_bf16(a)
    return _mm(w_hi, a_hi) + _mm(w_lo, a_hi) + _mm(w_hi, a_lo)


def _filter_kernel(emb_ref, w1_ref, w2_ref, w3_ref, col_ref, w4_ref, dl_ref, k_ref, s_ref, *, l, tr):
    t_idx = pl.program_id(0)
    r_row = t_idx * tr + lax.broadcasted_iota(jnp.int32, (1, tr), 1)
    p_row = jnp.where(r_row < l, r_row, 2 * l - r_row).astype(F32)
    emb = emb_ref[...]
    z = (emb[:, 2:3] * (p_row / (l - 1.0))
         + emb[:, 3:4] * jnp.cos(emb[:, 0:1] * ((2.0 * math.pi) * p_row / l) + emb[:, 1:2]))
    col = col_ref[...]
    fq = col[:, 3:4]
    h = jnp.sin(fq * (_mm_split(w1_ref[...], z) + col[:, 0:1]))
    h = jnp.sin(fq * (_mm_split(w2_ref[...], h) + col[:, 1:2]))
    h = jnp.sin(fq * (_mm_split(w3_ref[...], h) + col[:, 2:3]))
    h_hi, h_lo = _split_bf16(h)
    w_hi, w_lo = _split_bf16(w4_ref[...])
    k = lax.dot_general(jnp.concatenate([h_hi, h_lo, h_hi], axis=0),
                        jnp.concatenate([w_hi, w_hi, w_lo], axis=0),
                        (((0,), (0,)), ((), ())), preferred_element_type=F32)
    r_col = t_idx * tr + lax.broadcasted_iota(jnp.int32, (tr, 1), 0)
    t_col = jnp.where(r_col < l, r_col, 2 * l - r_col).astype(F32) / (l - 1.0)
    k = jnp.where(r_col == l, 0.0, k * jnp.exp(-t_col * dl_ref[...]))
    k_ref[...] = k.astype(BF16)

    @pl.when(t_idx == 0)
    def _():
        s_ref[...] = jnp.zeros_like(s_ref)

    s_ref[...] += jnp.sum(jnp.abs(k), axis=0, keepdims=True)


EMB_ROWS = 48


def _filter_embedding_table():
    fr = np.linspace(1e-4, FILTER_BANDS - 1, FILTER_BANDS)
    emb = np.zeros((EMB_ROWS, 4), np.float32)
    emb[1:1 + FILTER_BANDS, 0] = fr
    emb[1 + FILTER_BANDS:1 + 2 * FILTER_BANDS, 0] = fr
    emb[1 + FILTER_BANDS:1 + 2 * FILTER_BANDS, 1] = math.pi / 2
    emb[0, 2] = 1.0
    emb[1:1 + 2 * FILTER_BANDS, 3] = 1.0
    return emb


def _hyena_filter(l, fw1, fb1, fw2, fb2, fw3, fb3, fw4, freq, *, tr):
    width = fw1.shape[1]
    e = fw4.shape[1] // 2
    w1t = jnp.zeros((width, EMB_ROWS), F32).at[:, :fw1.shape[0]].set(fw1.T)
    cols = jnp.stack([fb1, fb2, fb3, freq], axis=1)
    deltas = jnp.abs(jnp.linspace(MIN_DECAY, MAX_DECAY, e, dtype=F32)).reshape(1, e)
    half = l // tr
    small = lambda shape: pl.BlockSpec(shape, lambda t: (0, 0))
    return pl.pallas_call(
        functools.partial(_filter_kernel, l=l, tr=tr),
        out_shape=(jax.ShapeDtypeStruct((2 * l, e), BF16), jax.ShapeDtypeStruct((1, e), F32)),
        grid=(2 * half,),
        in_specs=[
            small((EMB_ROWS, 4)), small((width, EMB_ROWS)), small((width, width)), small((width, width)),
            small((width, 4)),
            pl.BlockSpec((width, e), lambda t: (0, t // half)),
            small((1, e)),
        ],
        out_specs=(pl.BlockSpec((tr, e), lambda t: (t, 0)), pl.BlockSpec((1, e), lambda t: (0, 0))),
        compiler_params=pltpu.CompilerParams(dimension_semantics=("arbitrary",)),
        name="hyena_filter",
    )(jnp.asarray(_filter_embedding_table()), w1t, fw2.T, fw3.T, cols, fw4, deltas)


@functools.lru_cache(maxsize=None)
def _four_step_consts(n1, n2):
    n = n1 * n2
    hh = n1 // 2
    k1 = np.arange(n1)
    c2 = np.arange(n2)

    def angles(sign, n1_count):
        nn = n2 * np.arange(n1_count)[None, None, :] + c2[:, None, None]
        return sign * 2.0 * np.pi * ((k1[None, :, None] * nn) % n) / n

    a = angles(-1.0, hh)
    c, s = np.cos(a), np.sin(a)
    g = np.zeros((n2, n1, 2, 2, hh))
    g[:, :, 0, 0], g[:, :, 0, 1], g[:, :, 1, 0], g[:, :, 1, 1] = c, -s, s, c
    g = g.reshape(n2, 2 * n1, n1)

    a = angles(-1.0, n1)
    gk = np.stack([np.cos(a), np.sin(a)], axis=2).reshape(n2, 2 * n1, n1)

    a = np.transpose(angles(1.0, hh), (0, 2, 1))
    c, s = np.cos(a) / n, np.sin(a) / n
    h = np.zeros((n2, 2, hh, n1, 2))
    h[:, 0, :, :, 0], h[:, 0, :, :, 1], h[:, 1, :, :, 0], h[:, 1, :, :, 1] = c, -s, s, c
    h = h.reshape(n2, n1, 2 * n1)

    phi = -2.0 * np.pi * ((c2[:, None] * c2[None, :]) % n2) / n2
    c, s = np.cos(phi), np.sin(phi)
    fb = np.block([[c, -s], [s, c]])
    fbi = np.block([[c, s], [-s, c]])
    f = lambda x: np.asarray(x, np.float32)
    return f(g), f(gk), f(h), f(fb), f(fbi)


@functools.lru_cache(maxsize=None)
def _direct_dft_consts(nc):
    hh = nc // 2
    k = np.arange(nc)
    a = -2.0 * np.pi * ((k[:, None] * np.arange(hh)[None, :]) % nc) / nc
    c, s = np.cos(a), np.sin(a)
    fc = np.block([[c, -s], [s, c]])
    a = -2.0 * np.pi * ((k[:, None] * k[None, :]) % nc) / nc
    fck = np.concatenate([np.cos(a), np.sin(a)], axis=0)
    a = 2.0 * np.pi * ((np.arange(hh)[:, None] * k[None, :]) % nc) / nc
    c, s = np.cos(a) / nc, np.sin(a) / nc
    fci = np.block([[c, -s], [s, c]])
    f = lambda x: np.asarray(x, np.float32)
    return f(fc), f(fck), f(fci)


def _colmm_kernel(g_ref, x_ref, o_ref, xs_ref, rs_ref, *, nb):
    xs_ref[...] = pltpu.einshape("abc->bac", x_ref[...])
    for q in range(nb):
        rs_ref[q] = _mm(g_ref[q].astype(BF16), xs_ref[q]).astype(rs_ref.dtype)
    o_ref[...] = pltpu.einshape("abc->bac", rs_ref[...])


def _colmm(gm, x3, *, nb, cb, name):
    n2, r_out, r_in = gm.shape
    e = x3.shape[-1]
    return pl.pallas_call(
        functools.partial(_colmm_kernel, nb=nb),
        out_shape=jax.ShapeDtypeStruct((r_out, n2, e), BF16),
        grid=(n2 // nb, e // cb),
        in_specs=[
            pl.BlockSpec((nb, r_out, r_in), lambda s, j: (s, 0, 0)),
            pl.BlockSpec((r_in, nb, cb), lambda s, j: (0, s, j)),
        ],
        out_specs=pl.BlockSpec((r_out, nb, cb), lambda s, j: (0, s, j)),
        scratch_shapes=[pltpu.VMEM((nb, r_in, cb), BF16), pltpu.VMEM((nb, r_out, cb), BF16)],
        compiler_params=pltpu.CompilerParams(dimension_semantics=("parallel", "parallel")),
        name=name,
    )(gm, x3)


def _cmul(x, k, half):
    xr, xi = x[:half], x[half:]
    kr, ki = k[:half], k[half:]
    return jnp.concatenate([xr * kr - xi * ki, xr * ki + xi * kr], axis=0)


def _spectral_mul_kernel(fb_ref, fbi_ref, a_ref, ak_ref, s_ref, z_ref, *, kb):
    fb, fbi = fb_ref[...].astype(BF16), fbi_ref[...].astype(BF16)
    half = fb.shape[0] // 2
    inv = 1.0 / s_ref[...]
    xs = [_mm(fb, a_ref[q]) for q in range(kb)]
    ks = [_mm(fb, ak_ref[q]) * inv for q in range(kb)]
    ys = [_cmul(xs[q], ks[q], half).astype(BF16) for q in range(kb)]
    for q in range(kb):
        z_ref[q] = _mm(fbi, ys[q]).astype(BF16)


def _spectral_mul(fb, fbi, a3, ak3, s, *, kb, cb):
    n1, r, e = a3.shape
    slab = pl.BlockSpec((kb, r, cb), lambda i, j: (i, 0, j))
    mat = pl.BlockSpec((r, r), lambda i, j: (0, 0))
    return pl.pallas_call(
        functools.partial(_spectral_mul_kernel, kb=kb),
        out_shape=jax.ShapeDtypeStruct((n1, r, e), BF16),
        grid=(n1 // kb, e // cb),
        in_specs=[mat, mat, slab, slab, pl.BlockSpec((1, cb), lambda i, j: (0, j))],
        out_specs=slab,
        compiler_params=pltpu.CompilerParams(dimension_semantics=("parallel", "parallel")),
        name="hyena_spectral_mul",
    )(fb, fbi, a3, ak3, s)


def _ctx_conv_kernel(fc_ref, fck_ref, fci_ref, u_ref, k_ref, s_ref, y_ref):
    x = _mm(fc_ref[...].astype(BF16), u_ref[...])
    ks = _mm(fck_ref[...].astype(BF16), k_ref[...]) * (1.0 / s_ref[...])
    y = _cmul(x, ks, x.shape[0] // 2).astype(BF16)
    y_ref[...] = _mm(fci_ref[...].astype(BF16), y).astype(BF16)


def _ctx_conv(fc, fck, fci, u2, k2, s, *, cb):
    nc, e = u2.shape
    full = lambda a: pl.BlockSpec(a.shape, lambda j: (0, 0))
    col = pl.BlockSpec((nc, cb), lambda j: (0, j))
    return pl.pallas_call(
        _ctx_conv_kernel,
        out_shape=jax.ShapeDtypeStruct((nc, e), BF16),
        grid=(e // cb,),
        in_specs=[full(fc), full(fck), full(fci), col, col, pl.BlockSpec((1, cb), lambda j: (0, j))],
        out_specs=col,
        compiler_params=pltpu.CompilerParams(dimension_semantics=("parallel",)),
        name="hyena_ctx_conv",
    )(fc, fck, fci, u2, k2, s)


def _long_conv_latent(u, k_raw, s, *, nb, cb, kb):
    b, l, e = u.shape
    n = 2 * l
    n1 = 1 << ((n.bit_length() - 1 + 1) // 2)
    n2 = n // n1
    g, gk, h, fb, fbi = (jnp.asarray(m) for m in _four_step_consts(n1, n2))
    a = _colmm(g, u.reshape(n1, n2, e), nb=nb, cb=cb, name="hyena_dft_rows")
    ak = _colmm(gk, k_raw.reshape(n1, n2, e), nb=nb, cb=cb, name="hyena_filter_dft_rows")
    z = _spectral_mul(fb, fbi, a.reshape(n1, 2 * n2, e), ak.reshape(n1, 2 * n2, e), s, kb=kb, cb=cb)
    y = _colmm(h, z.reshape(2 * n1, n2, e), nb=nb, cb=cb, name="hyena_idft_rows")
    return y.reshape(b, l, e)


def _long_conv_ctx(u, k_raw, s, *, cb):
    b, c, e = u.shape
    fc, fck, fci = (jnp.asarray(m) for m in _direct_dft_consts(2 * c))
    return _ctx_conv(fc, fck, fci, u.reshape(b * c, e), k_raw, s, cb=cb).reshape(b, c, e)


def _hy_out_kernel(y_ref, u_ref, g_ref, d_ref, w_ref, x_ref, gt_ref, o_ref):
    m = (y_ref[...].astype(F32) + u_ref[...].astype(F32) * d_ref[...]) * g_ref[...].astype(F32)
    o_ref[...] = x_ref[...] + gt_ref[...] * _mm(m.astype(BF16), w_ref[...])


def _hy_out(y, u, g, d_skip, w_bf, x, gt, *, tm):
    b, l, e = y.shape
    d = x.shape[-1]
    act = pl.BlockSpec((None, tm, e), lambda bb, i: (bb, i, 0))
    res = pl.BlockSpec((None, tm, d), lambda bb, i: (bb, i, 0))
    return pl.pallas_call(
        _hy_out_kernel,
        out_shape=jax.ShapeDtypeStruct(x.shape, F32),
        grid=(b, l // tm),
        in_specs=[act, act, act, pl.BlockSpec((1, e), lambda bb, i: (0, 0)),
                  pl.BlockSpec((e, d), lambda bb, i: (0, 0)), res,
                  pl.BlockSpec((None, 1, d), lambda bb, i: (bb, 0, 0))],
        out_specs=res,
        compiler_params=pltpu.CompilerParams(dimension_semantics=("parallel", "parallel")),
        name="hyena_out_proj",
    )(y, u, g, d_skip.reshape(1, e), w_bf, x, gt)


def _at_out_kernel(m_ref, w_ref, x_ref, gt_ref, o_ref):
    o_ref[...] = x_ref[...] + gt_ref[...] * _mm(m_ref[...], w_ref[...])


def _at_out(m, w_bf, x, gt, *, tm):
    b, l, e = m.shape
    d = x.shape[-1]
    res = pl.BlockSpec((None, tm, d), lambda bb, i: (bb, i, 0))
    return pl.pallas_call(
        _at_out_kernel,
        out_shape=jax.ShapeDtypeStruct(x.shape, F32),
        grid=(b, l // tm),
        in_specs=[pl.BlockSpec((None, tm, e), lambda bb, i: (bb, i, 0)),
                  pl.BlockSpec((e, d), lambda bb, i: (0, 0)), res,
                  pl.BlockSpec((None, 1, d), lambda bb, i: (bb, 0, 0))],
        out_specs=res,
        compiler_params=pltpu.CompilerParams(dimension_semantics=("parallel", "parallel")),
        name="attn_out_proj",
    )(m, w_bf, x, gt)


def _head_norm_rope(p, gain, cos, sin, scale):
    lane = lax.broadcasted_iota(jnp.int32, (1, HEAD_DIM), 1)
    first = (lane % (HEAD_DIM // 2)) < (HEAD_DIM // 4)
    outs = []
    for hd in range(p.shape[1] // HEAD_DIM):
        xh = p[:, hd * HEAD_DIM:(hd + 1) * HEAD_DIM]
        ms = jnp.mean(xh * xh, axis=-1, keepdims=True)
        yh = xh * lax.rsqrt(ms + EPS) * gain
        partner = jnp.where(first, pltpu.roll(yh, HEAD_DIM - HEAD_DIM // 4, 1),
                            pltpu.roll(yh, HEAD_DIM // 4, 1))
        outs.append((yh * cos + partner * sin) * scale)
    return jnp.concatenate(outs, axis=1)


def _at_proj_kernel(x_ref, g_ref, sc_ref, sh_ref, w_ref, qg_ref, kg_ref, cos_ref, sin_ref,
                    q_ref, k_ref, v_ref, sz_ref, hbuf):
    qd = q_ref.shape[-1]
    hbuf[...] = _norm_mod(x_ref[...], g_ref[...], sc_ref[...], sh_ref[...]).astype(BF16)

    def proj(lo):
        return _mm(hbuf[...], w_ref[:, lo:lo + GROUP_W])

    for lo in range(0, qd, GROUP_W):
        q_ref[:, lo:lo + GROUP_W] = _head_norm_rope(proj(lo), qg_ref[...], cos_ref[...], sin_ref[...],
                                                    HEAD_DIM ** -0.5 * LOG2E).astype(BF16)
    k_ref[...] = _head_norm_rope(proj(qd), kg_ref[...], cos_ref[...], sin_ref[...], 1.0).astype(BF16)
    v_ref[...] = proj(qd + GROUP_W).astype(BF16)
    for lo in range(0, qd, GROUP_W):
        p = proj(qd + 2 * GROUP_W + lo)
        sz_ref[:, lo:lo + GROUP_W] = (p * _sigmoid(p)).astype(BF16)


def _at_proj(x, g, sc, sh, w_bf, q_g, k_g, cos, sin, *, tm):
    b, l, d = x.shape
    kvd = GROUP_W
    qd = (w_bf.shape[1] - 2 * kvd) // 2
    const = lambda shape: pl.BlockSpec(shape, lambda bb, i: (0, 0), pipeline_mode=pl.Buffered(1))
    vec = pl.BlockSpec((None, 1, d), lambda bb, i: (bb, 0, 0))
    tab = pl.BlockSpec((tm, HEAD_DIM), lambda bb, i: (i, 0))
    wide = pl.BlockSpec((None, tm, qd), lambda bb, i: (bb, i, 0))
    narrow = pl.BlockSpec((None, tm, kvd), lambda bb, i: (bb, i, 0))
    return pl.pallas_call(
        _at_proj_kernel,
        out_shape=(jax.ShapeDtypeStruct((b, l, qd), BF16), jax.ShapeDtypeStruct((b, l, kvd), BF16),
                   jax.ShapeDtypeStruct((b, l, kvd), BF16), jax.ShapeDtypeStruct((b, l, qd), BF16)),
        grid=(b, l // tm),
        in_specs=[
            pl.BlockSpec((None, tm, d), lambda bb, i: (bb, i, 0)),
            const((1, d)), vec, vec, const(w_bf.shape),
            const((1, HEAD_DIM)), const((1, HEAD_DIM)), tab, tab,
        ],
        out_specs=(wide, narrow, narrow, wide),
        scratch_shapes=[pltpu.VMEM((tm, d), BF16)],
        compiler_params=pltpu.CompilerParams(dimension_semantics=("parallel", "parallel")),
        name="attn_in_proj",
    )(x, g.reshape(1, d), sc, sh, w_bf, q_g.reshape(1, -1), k_g.reshape(1, -1), cos, sin)


def _at_proj_kv_kernel(x_ref, g_ref, sc_ref, sh_ref, wk_ref, wv_ref, kg_ref, k_ref, v_ref):
    h = _norm_mod(x_ref[...], g_ref[...], sc_ref[...], sh_ref[...]).astype(BF16)
    p = _mm(h, wk_ref[...])
    gain = kg_ref[...]
    outs = []
    for hd in range(p.shape[1] // HEAD_DIM):
        xh = p[:, hd * HEAD_DIM:(hd + 1) * HEAD_DIM]
        ms = jnp.mean(xh * xh, axis=-1, keepdims=True)
        outs.append(xh * lax.rsqrt(ms + EPS) * gain)
    k_ref[...] = jnp.concatenate(outs, axis=1).astype(BF16)
    v_ref[...] = _mm(h, wv_ref[...]).astype(BF16)


def _at_proj_kv(x, g, sc, sh, w_bf, k_g, *, qd):
    b, c, d = x.shape
    kb = qd // GROUP_W
    vec = pl.BlockSpec((None, 1, d), lambda bb: (bb, 0, 0))
    out = pl.BlockSpec((None, c, GROUP_W), lambda bb: (bb, 0, 0))
    return pl.pallas_call(
        _at_proj_kv_kernel,
        out_shape=(jax.ShapeDtypeStruct((b, c, GROUP_W), BF16),) * 2,
        grid=(b,),
        in_specs=[
            pl.BlockSpec((None, c, d), lambda bb: (bb, 0, 0)),
            pl.BlockSpec((1, d), lambda bb: (0, 0)), vec, vec,
            pl.BlockSpec((d, GROUP_W), lambda bb: (0, kb)),
            pl.BlockSpec((d, GROUP_W), lambda bb: (0, kb + 1)),
            pl.BlockSpec((1, HEAD_DIM), lambda bb: (0, 0)),
        ],
        out_specs=(out, out),
        compiler_params=pltpu.CompilerParams(dimension_semantics=("parallel",)),
        name="attn_ctx_kv_proj",
    )(x, g.reshape(1, d), sc, sh, w_bf, w_bf, k_g.reshape(1, -1))


def _attn_kernel(q_ref, k_ref, v_ref, kc_ref, vc_ref, sz_ref, o_ref, vt_ref, acc_ref, sbuf,
                 *, tq, tk, n_k, c, nsub):
    @pl.when(pl.program_id(2) == 0)
    def _():
        ones = jnp.ones((VT_ROWS - HEAD_DIM, tk), BF16)
        for t in range(n_k):
            vt_ref[t, 0:HEAD_DIM, :] = v_ref[t * tk:(t + 1) * tk, :].astype(F32).T.astype(BF16)
            vt_ref[t, HEAD_DIM:, :] = ones
        vt_ref[n_k, 0:HEAD_DIM, 0:c] = vc_ref[...].astype(F32).T.astype(BF16)
        vt_ref[n_k, HEAD_DIM:, :] = ones

    qts = []
    for sb in range(nsub):
        q = q_ref[sb * tq:(sb + 1) * tq, :]
        qts.append(jnp.concatenate([q[:, gi * HEAD_DIM:(gi + 1) * HEAD_DIM].astype(F32).T.astype(BF16)
                                    for gi in range(GROUP)], axis=1))
    acc_ref[...] = jnp.zeros_like(acc_ref)

    def scores(kt, slot):
        for sb in range(nsub):
            sbuf[slot, sb, 0:kt.shape[0]] = _mm(kt, qts[sb])

    def consume(slot, vt, ms):
        out = []
        for sb in range(nsub):
            s = sbuf[slot, sb, 0:vt.shape[1]]
            m_new = jnp.maximum(ms[sb], jnp.max(s, axis=0, keepdims=True))
            p = jnp.exp2(s - m_new).astype(BF16)
            acc_ref[sb] = jnp.exp2(ms[sb] - m_new) * acc_ref[sb] + _mm(vt, p)
            out.append(m_new)
        return tuple(out)

    def key_chunk(t):
        return k_ref[pl.ds(pl.multiple_of(t * tk, tk), tk), :]

    def body(i, ms):
        t = KV_UNROLL * i
        for u in range(KV_UNROLL):
            scores(key_chunk(t + u + 1), (u + 1) % 2)
            ms = consume(u % 2, vt_ref[t + u], ms)
        return ms

    scores(k_ref[0:tk, :], 0)
    ms = tuple(jnp.full((1, GROUP * tq), -jnp.inf, F32) for _ in range(nsub))
    n_loop = (n_k - 2) // KV_UNROLL
    ms = lax.fori_loop(0, n_loop, body, ms)
    for t in range(n_loop * KV_UNROLL, n_k):
        if t + 1 < n_k:
            scores(k_ref[(t + 1) * tk:(t + 2) * tk, :], (t + 1) % 2)
        else:
            scores(kc_ref[...], (t + 1) % 2)
        ms = consume(t % 2, vt_ref[t], ms)
    consume(n_k % 2, vt_ref[n_k, :, 0:c], ms)
    for sb in range(nsub):
        acc = acc_ref[sb]
        ot = acc[0:HEAD_DIM] * (1.0 / acc[HEAD_DIM:HEAD_DIM + 1])
        o = jnp.concatenate([ot[:, gi * tq:(gi + 1) * tq].T for gi in range(GROUP)], axis=1)
        rows = slice(sb * tq, (sb + 1) * tq)
        o_ref[rows, :] = (o * sz_ref[rows, :].astype(F32)).astype(BF16)


def _attention(q, k, v, kc, vc, sz, *, tq, tk, nsub):
    b, l, qd = q.shape
    c = kc.shape[1]
    n_kv = qd // GROUP_W
    n_k = l // tk
    assert n_k % 2 == 0 and n_k >= 2 and c <= tk
    qspec = pl.BlockSpec((None, nsub * tq, GROUP_W), lambda bb, hh, i: (bb, i, hh))
    kspec = pl.BlockSpec((None, l, HEAD_DIM), lambda bb, hh, i: (bb, 0, hh))
    cspec = pl.BlockSpec((None, c, HEAD_DIM), lambda bb, hh, i: (bb, 0, hh))
    return pl.pallas_call(
        functools.partial(_attn_kernel, tq=tq, tk=tk, n_k=n_k, c=c, nsub=nsub),
        out_shape=jax.ShapeDtypeStruct((b, l, qd), BF16),
        grid=(b, n_kv, l // (nsub * tq)),
        in_specs=[qspec, kspec, kspec, cspec, cspec, qspec],
        out_specs=qspec,
        scratch_shapes=[pltpu.VMEM((n_k + 1, VT_ROWS, tk), BF16),
                        pltpu.VMEM((nsub, VT_ROWS, GROUP * tq), F32),
                        pltpu.VMEM((2, nsub, tk, GROUP * tq), F32)],
        compiler_params=pltpu.CompilerParams(
            dimension_semantics=("parallel", "parallel", "arbitrary")),
        name="flash_attention",
    )(q, k, v, kc, vc, sz)


def _rope_tables(l):
    rows = l // GRID_W
    axis_dim = HEAD_DIM // 2
    row = np.repeat(np.arange(rows, dtype=np.float64), GRID_W)
    col = np.tile(np.arange(GRID_W, dtype=np.float64), rows)
    inv = 1.0 / (ROPE_THETA ** (np.arange(0, axis_dim, 2, dtype=np.float64) / axis_dim))
    ra, ca = row[:, None] * inv[None, :], col[:, None] * inv[None, :]
    cos = np.concatenate([np.cos(ra), np.cos(ra), np.cos(ca), np.cos(ca)], axis=-1)
    sin = np.concatenate([-np.sin(ra), np.sin(ra), -np.sin(ca), np.sin(ca)], axis=-1)
    return jnp.asarray(cos, F32), jnp.asarray(sin, F32)


def _pick(n, target):
    t = min(n, target)
    while n % t:
        t //= 2
    return t


def kernel(x, c, ctx, c_ctx, norm_g, ada_w, ada_b,
           hy_w_in, hy_conv_w, hy_conv_b, hy_fw1, hy_fb1, hy_fw2, hy_fb2, hy_fw3, hy_fb3,
           hy_fw4, hy_freq, hy_d, hy_w_out,
           at_w_in, at_q_g, at_k_g, at_w_out):
    b, l, d = x.shape
    n_ctx = ctx.shape[1]
    e = hy_d.shape[-1]

    cv = jnp.concatenate([c, c_ctx[None], jnp.zeros((8 - b - 1, d), F32)], axis=0)
    mod = _modulation(cv, ada_w, ada_b)

    def lat(v):
        return v[:b, None, :]

    def cx(v):
        return jnp.broadcast_to(v[b][None, None, :], (b, 1, d))

    sh0, sc0, gt0 = jnp.split(mod[0], 3, axis=-1)
    w_in = hy_w_in[0].astype(BF16)
    w_out = hy_w_out[0].astype(BF16)
    fargs = (hy_fw1[0], hy_fb1[0], hy_fw2[0], hy_fb2[0], hy_fw3[0], hy_fb3[0], hy_fw4[0], hy_freq[0])
    tn = _pick(e, 256)
    cb = _pick(e, 512)

    u_lat, g_lat = _hy_proj(x, norm_g[0], lat(sc0), lat(sh0), w_in, hy_conv_w[0], hy_conv_b[0],
                            tm=_pick(l, 1024), tn=tn)
    u_ctx, g_ctx = _hy_proj(ctx, norm_g[0], cx(sc0), cx(sh0), w_in, hy_conv_w[0], hy_conv_b[0],
                            tm=_pick(n_ctx, 1024), tn=tn)
    k_lat, s_lat = _hyena_filter(l, *fargs, tr=_pick(l, 256))
    k_ctx, s_ctx = _hyena_filter(n_ctx, *fargs, tr=_pick(n_ctx, 256))
    y_lat = _long_conv_latent(u_lat, k_lat, s_lat, nb=16, cb=cb, kb=4)
    y_ctx = _long_conv_ctx(u_ctx, k_ctx, s_ctx, cb=cb)
    x1 = _hy_out(y_lat, u_lat, g_lat, hy_d[0], w_out, x, lat(gt0), tm=_pick(l, 512))
    ctx1 = _hy_out(y_ctx, u_ctx, g_ctx, hy_d[0], w_out, ctx, cx(gt0), tm=_pick(n_ctx, 512))

    sh1, sc1, gt1 = jnp.split(mod[1], 3, axis=-1)
    aw_in = at_w_in[0].astype(BF16)
    aw_out = at_w_out[0].astype(BF16)
    cos, sin = _rope_tables(l)
    q, k, v, sz = _at_proj(x1, norm_g[1], lat(sc1), lat(sh1), aw_in, at_q_g[0], at_k_g[0], cos, sin,
                           tm=_pick(l, 512))
    kc, vc = _at_proj_kv(ctx1, norm_g[1], cx(sc1), cx(sh1), aw_in, at_k_g[0], qd=q.shape[-1])
    om = _attention(q, k, v, kc, vc, sz, tq=_pick(l // 2, 256), tk=_pick(l // 2, 512), nsub=2)
    return _at_out(om, aw_out, x1, lat(gt1), tm=_pick(l, 512))
```

```python
import functools
import math

import numpy as np
import jax
import jax.numpy as jnp
from jax import lax
from jax.experimental import pallas as pl
from jax.experimental.pallas import tpu as pltpu

F32 = jnp.float32
BF16 = jnp.bfloat16
HIGHEST = lax.Precision.HIGHEST

EPS = 1e-6
HEAD_DIM = 128
GROUP = 4
GROUP_W = GROUP * HEAD_DIM
GRID_W = 64
ROPE_THETA = 10000.0
FILTER_BANDS = 16
DECAY_TARGET = 1e-2
MAX_DECAY = math.log(DECAY_TARGET) / 0.3
MIN_DECAY = math.log(DECAY_TARGET) / 1.5
HALO = 16
LANES = 128
VT_ROWS = HEAD_DIM + 16
LOG2E = math.log2(math.e)
KV_UNROLL = 4
STAB_KEYS = 128
STAB_GAP = 90.0


def _sigmoid(x):
    return 1.0 / (1.0 + jnp.exp(-x))


def _mm(a, b):
    return jnp.dot(a, b, preferred_element_type=F32)


def _mm_f32(a, b):
    return jnp.dot(a, b, preferred_element_type=F32, precision=HIGHEST)


def _mod_kernel(c_ref, w_ref, b_ref, o_ref):
    cv = c_ref[...]
    o_ref[...] = _mm_f32(cv * _sigmoid(cv), w_ref[...]) + b_ref[...]


def _modulation(cv, ada_w, ada_b):
    depth, d, d3 = ada_w.shape
    rows = cv.shape[0]
    tn = d
    return pl.pallas_call(
        _mod_kernel,
        out_shape=jax.ShapeDtypeStruct((depth, rows, d3), F32),
        grid=(depth, d3 // tn),
        in_specs=[
            pl.BlockSpec((rows, d), lambda l, j: (0, 0)),
            pl.BlockSpec((None, d, tn), lambda l, j: (l, 0, j)),
            pl.BlockSpec((None, 1, tn), lambda l, j: (l, 0, j)),
        ],
        out_specs=pl.BlockSpec((None, rows, tn), lambda l, j: (l, 0, j)),
        name="modulation",
    )(cv, ada_w, ada_b.reshape(depth, 1, d3))


def _norm_mod(x, g, sc, sh):
    ms = jnp.mean(x * x, axis=-1, keepdims=True)
    return x * lax.rsqrt(ms + EPS) * (g * (1.0 + sc)) + sh


def _hy_proj_kernel(xm_ref, xp_ref, xn_ref, g_ref, sc_ref, sh_ref, w_ref, cw_ref, cb_ref,
                    u_ref, gate_ref, hbuf, *, tm, tn, n_i):
    i = pl.program_id(1)
    e = u_ref.shape[-1]
    g, sc, sh = g_ref[...], sc_ref[...], sh_ref[...]
    hbuf[HALO:HALO + tm, :] = _norm_mod(xm_ref[...], g, sc, sh).astype(BF16)
    hp = _norm_mod(xp_ref[...], g, sc, sh)
    hbuf[0:HALO, :] = jnp.where(i > 0, hp, 0.0).astype(BF16)
    hn = _norm_mod(xn_ref[...], g, sc, sh)
    hbuf[HALO + tm:, :] = jnp.where(i < n_i - 1, hn, 0.0).astype(BF16)

    def conv(lo):
        p = _mm(hbuf[...], w_ref[:, lo:lo + tn])
        cw = cw_ref[:, lo:lo + tn]
        rows = p.shape[0]
        prev = pltpu.roll(p, 1, 0)[HALO:HALO + tm]
        nxt = pltpu.roll(p, rows - 1, 0)[HALO:HALO + tm]
        return prev * cw[0:1] + p[HALO:HALO + tm] * cw[1:2] + nxt * cw[2:3] + cb_ref[:, lo:lo + tn]

    for j in range(e // tn):
        lo = j * tn
        u_ref[:, lo:lo + tn] = (conv(e + lo) * conv(2 * e + lo)).astype(BF16)
        z = _mm(hbuf[HALO:HALO + tm, :], w_ref[:, 3 * e + lo:3 * e + lo + tn])
        gate_ref[:, lo:lo + tn] = (conv(lo) * z * _sigmoid(z)).astype(BF16)


def _hy_proj(x, g, sc, sh, w_bf, conv_w, conv_b, *, tm, tn):
    b, l, d = x.shape
    e = w_bf.shape[1] // 4
    n_i = l // tm
    hb = tm // HALO
    last_hb = l // HALO - 1
    const = lambda shape: pl.BlockSpec(shape, lambda bb, i: (0, 0), pipeline_mode=pl.Buffered(1))
    vec = pl.BlockSpec((None, 1, d), lambda bb, i: (bb, 0, 0))
    out = pl.BlockSpec((None, tm, e), lambda bb, i: (bb, i, 0))
    return pl.pallas_call(
        functools.partial(_hy_proj_kernel, tm=tm, tn=tn, n_i=n_i),
        out_shape=(jax.ShapeDtypeStruct((b, l, e), BF16), jax.ShapeDtypeStruct((b, l, e), BF16)),
        grid=(b, n_i),
        in_specs=[
            pl.BlockSpec((None, tm, d), lambda bb, i: (bb, i, 0)),
            pl.BlockSpec((None, HALO, d), lambda bb, i: (bb, jnp.maximum(i * hb - 1, 0), 0)),
            pl.BlockSpec((None, HALO, d), lambda bb, i: (bb, jnp.minimum((i + 1) * hb, last_hb), 0)),
            const((1, d)), vec, vec,
            const((d, 4 * e)), const((3, 3 * e)), const((1, 3 * e)),
        ],
        out_specs=(out, out),
        scratch_shapes=[pltpu.VMEM((tm + 2 * HALO, d), BF16)],
        compiler_params=pltpu.CompilerParams(dimension_semantics=("parallel", "parallel")),
        name="hyena_in_proj",
    )(x, x, x, g.reshape(1, d), sc, sh, w_bf, conv_w, conv_b.reshape(1, -1))


def _split_bf16(a):
    hi = a.astype(BF16)
    return hi, (a - hi.astype(F32)).astype(BF16)


def _mm_split(w, a):
    w_hi, w_lo = _split_bf16(w)
    a_hi, a_lo = _split_bf16(a)
    return _mm(w_hi, a_hi) + _mm(w_lo, a_hi) + _mm(w_hi, a_lo)


def _filter_kernel(emb_ref, w1_ref, w2_ref, w3_ref, col_ref, w4_ref, dl_ref, k_ref, s_ref, *, l, tr):
    t_idx = pl.program_id(0)
    r_row = t_idx * tr + lax.broadcasted_iota(jnp.int32, (1, tr), 1)
    p_row = jnp.where(r_row < l, r_row, 2 * l - r_row).astype(F32)
    emb = emb_ref[...]
    z = (emb[:, 2:3] * (p_row / (l - 1.0))
         + emb[:, 3:4] * jnp.cos(emb[:, 0:1] * ((2.0 * math.pi) * p_row / l) + emb[:, 1:2]))
    col = col_ref[...]
    fq = col[:, 3:4]
    h = jnp.sin(fq * (_mm_split(w1_ref[...], z) + col[:, 0:1]))
    h = jnp.sin(fq * (_mm_split(w2_ref[...], h) + col[:, 1:2]))
    h = jnp.sin(fq * (_mm_split(w3_ref[...], h) + col[:, 2:3]))
    h_hi, h_lo = _split_bf16(h)
    w_hi, w_lo = _split_bf16(w4_ref[...])
    k = lax.dot_general(jnp.concatenate([h_hi, h_lo, h_hi], axis=0),
                        jnp.concatenate([w_hi, w_hi, w_lo], axis=0),
                        (((0,), (0,)), ((), ())), preferred_element_type=F32)
    r_col = t_idx * tr + lax.broadcasted_iota(jnp.int32, (tr, 1), 0)
    t_col = jnp.where(r_col < l, r_col, 2 * l - r_col).astype(F32) / (l - 1.0)
    k = jnp.where(r_col == l, 0.0, k * jnp.exp(-t_col * dl_ref[...]))
    k_ref[...] = k.astype(BF16)

    @pl.when(t_idx == 0)
    def _():
        s_ref[...] = jnp.zeros_like(s_ref)

    s_ref[...] += jnp.sum(jnp.abs(k), axis=0, keepdims=True)


EMB_ROWS = 48


def _filter_embedding_table():
    fr = np.linspace(1e-4, FILTER_BANDS - 1, FILTER_BANDS)
    emb = np.zeros((EMB_ROWS, 4), np.float32)
    emb[1:1 + FILTER_BANDS, 0] = fr
    emb[1 + FILTER_BANDS:1 + 2 * FILTER_BANDS, 0] = fr
    emb[1 + FILTER_BANDS:1 + 2 * FILTER_BANDS, 1] = math.pi / 2
    emb[0, 2] = 1.0
    emb[1:1 + 2 * FILTER_BANDS, 3] = 1.0
    return emb


def _hyena_filter(l, fw1, fb1, fw2, fb2, fw3, fb3, fw4, freq, *, tr):
    width = fw1.shape[1]
    e = fw4.shape[1] // 2
    w1t = jnp.zeros((width, EMB_ROWS), F32).at[:, :fw1.shape[0]].set(fw1.T)
    cols = jnp.stack([fb1, fb2, fb3, freq], axis=1)
    deltas = jnp.abs(jnp.linspace(MIN_DECAY, MAX_DECAY, e, dtype=F32)).reshape(1, e)
    half = l // tr
    small = lambda shape: pl.BlockSpec(shape, lambda t: (0, 0))
    return pl.pallas_call(
        functools.partial(_filter_kernel, l=l, tr=tr),
        out_shape=(jax.ShapeDtypeStruct((2 * l, e), BF16), jax.ShapeDtypeStruct((1, e), F32)),
        grid=(2 * half,),
        in_specs=[
            small((EMB_ROWS, 4)), small((width, EMB_ROWS)), small((width, width)), small((width, width)),
            small((width, 4)),
            pl.BlockSpec((width, e), lambda t: (0, t // half)),
            small((1, e)),
        ],
        out_specs=(pl.BlockSpec((tr, e), lambda t: (t, 0)), pl.BlockSpec((1, e), lambda t: (0, 0))),
        compiler_params=pltpu.CompilerParams(dimension_semantics=("arbitrary",)),
        name="hyena_filter",
    )(jnp.asarray(_filter_embedding_table()), w1t, fw2.T, fw3.T, cols, fw4, deltas)


@functools.lru_cache(maxsize=None)
def _four_step_consts(n1, n2):
    n = n1 * n2
    hh = n1 // 2
    k1 = np.arange(n1)
    c2 = np.arange(n2)

    def angles(sign, n1_count):
        nn = n2 * np.arange(n1_count)[None, None, :] + c2[:, None, None]
        return sign * 2.0 * np.pi * ((k1[None, :, None] * nn) % n) / n

    a = angles(-1.0, hh)
    c, s = np.cos(a), np.sin(a)
    g = np.zeros((n2, n1, 2, 2, hh))
    g[:, :, 0, 0], g[:, :, 0, 1], g[:, :, 1, 0], g[:, :, 1, 1] = c, -s, s, c
    g = g.reshape(n2, 2 * n1, n1)

    a = angles(-1.0, n1)
    gk = np.stack([np.cos(a), np.sin(a)], axis=2).reshape(n2, 2 * n1, n1)

    a = np.transpose(angles(1.0, hh), (0, 2, 1))
    c, s = np.cos(a) / n, np.sin(a) / n
    h = np.zeros((n2, 2, hh, n1, 2))
    h[:, 0, :, :, 0], h[:, 0, :, :, 1], h[:, 1, :, :, 0], h[:, 1, :, :, 1] = c, -s, s, c
    h = h.reshape(n2, n1, 2 * n1)

    phi = -2.0 * np.pi * ((c2[:, None] * c2[None, :]) % n2) / n2
    c, s = np.cos(phi), np.sin(phi)
    fb = np.block([[c, -s], [s, c]])
    fbi = np.block([[c, s], [-s, c]])
    f = lambda x: np.asarray(x, np.float32)
    return f(g), f(gk), f(h), f(fb), f(fbi)


@functools.lru_cache(maxsize=None)
def _direct_dft_consts(nc):
    hh = nc // 2
    k = np.arange(nc)
    a = -2.0 * np.pi * ((k[:, None] * np.arange(hh)[None, :]) % nc) / nc
    c, s = np.cos(a), np.sin(a)
    fc = np.block([[c, -s], [s, c]])
    a = -2.0 * np.pi * ((k[:, None] * k[None, :]) % nc) / nc
    fck = np.concatenate([np.cos(a), np.sin(a)], axis=0)
    a = 2.0 * np.pi * ((np.arange(hh)[:, None] * k[None, :]) % nc) / nc
    c, s = np.cos(a) / nc, np.sin(a) / nc
    fci = np.block([[c, -s], [s, c]])
    f = lambda x: np.asarray(x, np.float32)
    return f(fc), f(fck), f(fci)


def _colmm_kernel(g_ref, x_ref, o_ref, xs_ref, rs_ref, *, nb):
    xs_ref[...] = pltpu.einshape("abc->bac", x_ref[...])
    for q in range(nb):
        rs_ref[q] = _mm(g_ref[q].astype(BF16), xs_ref[q]).astype(rs_ref.dtype)
    o_ref[...] = pltpu.einshape("abc->bac", rs_ref[...])


def _colmm(gm, x3, *, nb, cb, name):
    n2, r_out, r_in = gm.shape
    e = x3.shape[-1]
    return pl.pallas_call(
        functools.partial(_colmm_kernel, nb=nb),
        out_shape=jax.ShapeDtypeStruct((r_out, n2, e), BF16),
        grid=(n2 // nb, e // cb),
        in_specs=[
            pl.BlockSpec((nb, r_out, r_in), lambda s, j: (s, 0, 0)),
            pl.BlockSpec((r_in, nb, cb), lambda s, j: (0, s, j)),
        ],
        out_specs=pl.BlockSpec((r_out, nb, cb), lambda s, j: (0, s, j)),
        scratch_shapes=[pltpu.VMEM((nb, r_in, cb), BF16), pltpu.VMEM((nb, r_out, cb), BF16)],
        compiler_params=pltpu.CompilerParams(dimension_semantics=("parallel", "parallel")),
        name=name,
    )(gm, x3)


def _cmul(x, k, half):
    xr, xi = x[:half], x[half:]
    kr, ki = k[:half], k[half:]
    return jnp.concatenate([xr * kr - xi * ki, xr * ki + xi * kr], axis=0)


def _spectral_mul_kernel(fb_ref, fbi_ref, a_ref, ak_ref, s_ref, z_ref, *, kb):
    fb, fbi = fb_ref[...].astype(BF16), fbi_ref[...].astype(BF16)
    half = fb.shape[0] // 2
    inv = 1.0 / s_ref[...]
    xs = [_mm(fb, a_ref[q]) for q in range(kb)]
    ks = [_mm(fb, ak_ref[q]) * inv for q in range(kb)]
    ys = [_cmul(xs[q], ks[q], half).astype(BF16) for q in range(kb)]
    for q in range(kb):
        z_ref[q] = _mm(fbi, ys[q]).astype(BF16)


def _spectral_mul(fb, fbi, a3, ak3, s, *, kb, cb):
    n1, r, e = a3.shape
    slab = pl.BlockSpec((kb, r, cb), lambda i, j: (i, 0, j))
    mat = pl.BlockSpec((r, r), lambda i, j: (0, 0))
    return pl.pallas_call(
        functools.partial(_spectral_mul_kernel, kb=kb),
        out_shape=jax.ShapeDtypeStruct((n1, r, e), BF16),
        grid=(n1 // kb, e // cb),
        in_specs=[mat, mat, slab, slab, pl.BlockSpec((1, cb), lambda i, j: (0, j))],
        out_specs=slab,
        compiler_params=pltpu.CompilerParams(dimension_semantics=("parallel", "parallel")),
        name="hyena_spectral_mul",
    )(fb, fbi, a3, ak3, s)


def _ctx_conv_kernel(fc_ref, fck_ref, fci_ref, u_ref, k_ref, s_ref, y_ref):
    x = _mm(fc_ref[...].astype(BF16), u_ref[...])
    ks = _mm(fck_ref[...].astype(BF16), k_ref[...]) * (1.0 / s_ref[...])
    y = _cmul(x, ks, x.shape[0] // 2).astype(BF16)
    y_ref[...] = _mm(fci_ref[...].astype(BF16), y).astype(BF16)


def _ctx_conv(fc, fck, fci, u2, k2, s, *, cb):
    nc, e = u2.shape
    full = lambda a: pl.BlockSpec(a.shape, lambda j: (0, 0))
    col = pl.BlockSpec((nc, cb), lambda j: (0, j))
    return pl.pallas_call(
        _ctx_conv_kernel,
        out_shape=jax.ShapeDtypeStruct((nc, e), BF16),
        grid=(e // cb,),
        in_specs=[full(fc), full(fck), full(fci), col, col, pl.BlockSpec((1, cb), lambda j: (0, j))],
        out_specs=col,
        compiler_params=pltpu.CompilerParams(dimension_semantics=("parallel",)),
        name="hyena_ctx_conv",
    )(fc, fck, fci, u2, k2, s)


def _long_conv_latent(u, k_raw, s, *, nb, cb, kb):
    b, l, e = u.shape
    n = 2 * l
    n1 = 1 << ((n.bit_length() - 1 + 1) // 2)
    n2 = n // n1
    g, gk, h, fb, fbi = (jnp.asarray(m) for m in _four_step_consts(n1, n2))
    a = _colmm(g, u.reshape(n1, n2, e), nb=nb, cb=cb, name="hyena_dft_rows")
    ak = _colmm(gk, k_raw.reshape(n1, n2, e), nb=nb, cb=cb, name="hyena_filter_dft_rows")
    z = _spectral_mul(fb, fbi, a.reshape(n1, 2 * n2, e), ak.reshape(n1, 2 * n2, e), s, kb=kb, cb=cb)
    y = _colmm(h, z.reshape(2 * n1, n2, e), nb=nb, cb=cb, name="hyena_idft_rows")
    return y.reshape(b, l, e)


def _long_conv_ctx(u, k_raw, s, *, cb):
    b, c, e = u.shape
    fc, fck, fci = (jnp.asarray(m) for m in _direct_dft_consts(2 * c))
    return _ctx_conv(fc, fck, fci, u.reshape(b * c, e), k_raw, s, cb=cb).reshape(b, c, e)


def _hy_out_kernel(y_ref, u_ref, g_ref, d_ref, w_ref, x_ref, gt_ref, o_ref):
    m = (y_ref[...].astype(F32) + u_ref[...].astype(F32) * d_ref[...]) * g_ref[...].astype(F32)
    o_ref[...] = x_ref[...] + gt_ref[...] * _mm(m.astype(BF16), w_ref[...])


def _hy_out(y, u, g, d_skip, w_bf, x, gt, *, tm):
    b, l, e = y.shape
    d = x.shape[-1]
    act = pl.BlockSpec((None, tm, e), lambda bb, i: (bb, i, 0))
    res = pl.BlockSpec((None, tm, d), lambda bb, i: (bb, i, 0))
    return pl.pallas_call(
        _hy_out_kernel,
        out_shape=jax.ShapeDtypeStruct(x.shape, F32),
        grid=(b, l // tm),
        in_specs=[act, act, act, pl.BlockSpec((1, e), lambda bb, i: (0, 0)),
                  pl.BlockSpec((e, d), lambda bb, i: (0, 0)), res,
                  pl.BlockSpec((None, 1, d), lambda bb, i: (bb, 0, 0))],
        out_specs=res,
        compiler_params=pltpu.CompilerParams(dimension_semantics=("parallel", "parallel")),
        name="hyena_out_proj",
    )(y, u, g, d_skip.reshape(1, e), w_bf, x, gt)


def _at_out_kernel(m_ref, w_ref, x_ref, gt_ref, o_ref):
    o_ref[...] = x_ref[...] + gt_ref[...] * _mm(m_ref[...], w_ref[...])


def _at_out(m, w_bf, x, gt, *, tm):
    b, l, e = m.shape
    d = x.shape[-1]
    res = pl.BlockSpec((None, tm, d), lambda bb, i: (bb, i, 0))
    return pl.pallas_call(
        _at_out_kernel,
        out_shape=jax.ShapeDtypeStruct(x.shape, F32),
        grid=(b, l // tm),
        in_specs=[pl.BlockSpec((None, tm, e), lambda bb, i: (bb, i, 0)),
                  pl.BlockSpec((e, d), lambda bb, i: (0, 0)), res,
                  pl.BlockSpec((None, 1, d), lambda bb, i: (bb, 0, 0))],
        out_specs=res,
        compiler_params=pltpu.CompilerParams(dimension_semantics=("parallel", "parallel")),
        name="attn_out_proj",
    )(m, w_bf, x, gt)


def _head_norm_rope(p, gain, cos, sin, scale):
    lane = lax.broadcasted_iota(jnp.int32, (1, HEAD_DIM), 1)
    first = (lane % (HEAD_DIM // 2)) < (HEAD_DIM // 4)
    outs = []
    for hd in range(p.shape[1] // HEAD_DIM):
        xh = p[:, hd * HEAD_DIM:(hd + 1) * HEAD_DIM]
        ms = jnp.mean(xh * xh, axis=-1, keepdims=True)
        yh = xh * lax.rsqrt(ms + EPS) * gain
        partner = jnp.where(first, pltpu.roll(yh, HEAD_DIM - HEAD_DIM // 4, 1),
                            pltpu.roll(yh, HEAD_DIM // 4, 1))
        outs.append((yh * cos + partner * sin) * scale)
    return jnp.concatenate(outs, axis=1)


def _at_proj_kernel(x_ref, g_ref, sc_ref, sh_ref, w_ref, qg_ref, kg_ref, cos_ref, sin_ref,
                    q_ref, k_ref, v_ref, sz_ref, hbuf):
    qd = q_ref.shape[-1]
    hbuf[...] = _norm_mod(x_ref[...], g_ref[...], sc_ref[...], sh_ref[...]).astype(BF16)

    def proj(lo):
        return _mm(hbuf[...], w_ref[:, lo:lo + GROUP_W])

    for lo in range(0, qd, GROUP_W):
        q_ref[:, lo:lo + GROUP_W] = _head_norm_rope(proj(lo), qg_ref[...], cos_ref[...], sin_ref[...],
                                                    HEAD_DIM ** -0.5 * LOG2E).astype(BF16)
    k_ref[...] = _head_norm_rope(proj(qd), kg_ref[...], cos_ref[...], sin_ref[...], 1.0).astype(BF16)
    v_ref[...] = proj(qd + GROUP_W).astype(BF16)
    for lo in range(0, qd, GROUP_W):
        p = proj(qd + 2 * GROUP_W + lo)
        sz_ref[:, lo:lo + GROUP_W] = (p * _sigmoid(p)).astype(BF16)


def _at_proj(x, g, sc, sh, w_bf, q_g, k_g, cos, sin, *, tm):
    b, l, d = x.shape
    kvd = GROUP_W
    qd = (w_bf.shape[1] - 2 * kvd) // 2
    const = lambda shape: pl.BlockSpec(shape, lambda bb, i: (0, 0), pipeline_mode=pl.Buffered(1))
    vec = pl.BlockSpec((None, 1, d), lambda bb, i: (bb, 0, 0))
    tab = pl.BlockSpec((tm, HEAD_DIM), lambda bb, i: (i, 0))
    wide = pl.BlockSpec((None, tm, qd), lambda bb, i: (bb, i, 0))
    narrow = pl.BlockSpec((None, tm, kvd), lambda bb, i: (bb, i, 0))
    return pl.pallas_call(
        _at_proj_kernel,
        out_shape=(jax.ShapeDtypeStruct((b, l, qd), BF16), jax.ShapeDtypeStruct((b, l, kvd), BF16),
                   jax.ShapeDtypeStruct((b, l, kvd), BF16), jax.ShapeDtypeStruct((b, l, qd), BF16)),
        grid=(b, l // tm),
        in_specs=[
            pl.BlockSpec((None, tm, d), lambda bb, i: (bb, i, 0)),
            const((1, d)), vec, vec, const(w_bf.shape),
            const((1, HEAD_DIM)), const((1, HEAD_DIM)), tab, tab,
        ],
        out_specs=(wide, narrow, narrow, wide),
        scratch_shapes=[pltpu.VMEM((tm, d), BF16)],
        compiler_params=pltpu.CompilerParams(dimension_semantics=("parallel", "parallel")),
        name="attn_in_proj",
    )(x, g.reshape(1, d), sc, sh, w_bf, q_g.reshape(1, -1), k_g.reshape(1, -1), cos, sin)


def _at_proj_kv_kernel(x_ref, g_ref, sc_ref, sh_ref, wk_ref, wv_ref, kg_ref, k_ref, v_ref):
    h = _norm_mod(x_ref[...], g_ref[...], sc_ref[...], sh_ref[...]).astype(BF16)
    p = _mm(h, wk_ref[...])
    gain = kg_ref[...]
    outs = []
    for hd in range(p.shape[1] // HEAD_DIM):
        xh = p[:, hd * HEAD_DIM:(hd + 1) * HEAD_DIM]
        ms = jnp.mean(xh * xh, axis=-1, keepdims=True)
        outs.append(xh * lax.rsqrt(ms + EPS) * gain)
    k_ref[...] = jnp.concatenate(outs, axis=1).astype(BF16)
    v_ref[...] = _mm(h, wv_ref[...]).astype(BF16)


def _at_proj_kv(x, g, sc, sh, w_bf, k_g, *, qd):
    b, c, d = x.shape
    kb = qd // GROUP_W
    vec = pl.BlockSpec((None, 1, d), lambda bb: (bb, 0, 0))
    out = pl.BlockSpec((None, c, GROUP_W), lambda bb: (bb, 0, 0))
    return pl.pallas_call(
        _at_proj_kv_kernel,
        out_shape=(jax.ShapeDtypeStruct((b, c, GROUP_W), BF16),) * 2,
        grid=(b,),
        in_specs=[
            pl.BlockSpec((None, c, d), lambda bb: (bb, 0, 0)),
            pl.BlockSpec((1, d), lambda bb: (0, 0)), vec, vec,
            pl.BlockSpec((d, GROUP_W), lambda bb: (0, kb)),
            pl.BlockSpec((d, GROUP_W), lambda bb: (0, kb + 1)),
            pl.BlockSpec((1, HEAD_DIM), lambda bb: (0, 0)),
        ],
        out_specs=(out, out),
        compiler_params=pltpu.CompilerParams(dimension_semantics=("parallel",)),
        name="attn_ctx_kv_proj",
    )(x, g.reshape(1, d), sc, sh, w_bf, w_bf, k_g.reshape(1, -1))


def _attn_kernel(q_ref, k_ref, v_ref, kc_ref, vc_ref, sz_ref, o_ref, vt_ref, ka_ref, kmax_ref, acc_ref,
                 sbuf, *, tq, tk, n_k, c, nsub):
    width = GROUP * tq

    @pl.when(pl.program_id(2) == 0)
    def _():
        ones = jnp.ones((VT_ROWS - HEAD_DIM, tk), BF16)
        def aug(rows):
            lane = lax.broadcasted_iota(jnp.int32, (rows, HEAD_DIM), 1)
            return jnp.where(lane < 2, 1.0, 0.0).astype(BF16)

        kk = jnp.zeros((1, 1), F32)
        chunks = [(t, k_ref[t * tk:(t + 1) * tk, :], v_ref[t * tk:(t + 1) * tk, :]) for t in range(n_k)]
        chunks.append((n_k, kc_ref[...], vc_ref[...]))
        for t, kt, vt in chunks:
            rows = kt.shape[0]
            vt_ref[t, 0:HEAD_DIM, 0:rows] = vt.astype(F32).T.astype(BF16)
            vt_ref[t, HEAD_DIM:, :] = ones
            ka_ref[t, 0:rows, 0:HEAD_DIM] = kt
            ka_ref[t, 0:rows, HEAD_DIM:] = aug(rows)
            kf = kt.astype(F32)
            kk = jnp.maximum(kk, jnp.max(jnp.sum(kf * kf, axis=1, keepdims=True), axis=0, keepdims=True))
        kmax_ref[...] = jnp.broadcast_to(jnp.sqrt(kk), kmax_ref.shape)

    qts, stab, gap = [], [], None
    for sb in range(nsub):
        q = q_ref[sb * tq:(sb + 1) * tq, :]
        qf = [q[:, gi * HEAD_DIM:(gi + 1) * HEAD_DIM].astype(F32).T for gi in range(GROUP)]
        qts.append(jnp.concatenate([x.astype(BF16) for x in qf], axis=1))
        qnorm = jnp.sqrt(jnp.concatenate([jnp.sum(x * x, axis=0, keepdims=True) for x in qf], axis=1))
        low = jnp.max(_mm(k_ref[0:STAB_KEYS, :], qts[sb]), axis=0, keepdims=True)
        stab.append(low)
        g = jnp.max(qnorm * kmax_ref[0:1, 0:1] - low)
        gap = g if gap is None else jnp.maximum(gap, g)
    acc_ref[...] = jnp.zeros_like(acc_ref)
    n_loop = (n_k - 2) // KV_UNROLL

    def run(scores, consume, first, chunk, ctx, carry):
        def body(i, carry):
            t = KV_UNROLL * i
            for u in range(KV_UNROLL):
                scores(chunk(t + u + 1), (u + 1) % 2)
                carry = consume(u % 2, vt_ref[t + u], carry)
            return carry

        scores(first, 0)
        carry = lax.fori_loop(0, n_loop, body, carry)
        for t in range(n_loop * KV_UNROLL, n_k):
            scores(chunk(t + 1) if t + 1 < n_k else ctx, (t + 1) % 2)
            carry = consume(t % 2, vt_ref[t], carry)
        consume(n_k % 2, vt_ref[n_k, :, 0:c], carry)

    @pl.when(gap <= STAB_GAP)
    def _():
        qas = []
        for sb in range(nsub):
            bits = lax.bitcast_convert_type(stab[sb], jnp.uint32) & jnp.uint32(0xFFFF0000)
            m_hi = lax.bitcast_convert_type(bits, F32)
            m_lo = stab[sb] - m_hi
            row = lax.broadcasted_iota(jnp.int32, (16, width), 0)
            extra = jnp.where(row == 0, -m_hi, jnp.where(row == 1, -m_lo, 0.0))
            qas.append(jnp.concatenate([qts[sb], extra.astype(BF16),
                                        jnp.zeros((HEAD_DIM - 16, width), BF16)], axis=0))

        def scores(kt, slot):
            for sb in range(nsub):
                sbuf[slot, sb, 0:kt.shape[0]] = _mm(kt, qas[sb])

        def consume(slot, vt, carry):
            for sb in range(nsub):
                acc_ref[sb] += _mm(vt, jnp.exp2(sbuf[slot, sb, 0:vt.shape[1]]).astype(BF16))
            return carry

        run(scores, consume, ka_ref[0], lambda t: ka_ref[t], ka_ref[n_k, 0:c, :], 0)

    @pl.when(jnp.logical_not(gap <= STAB_GAP))
    def _():
        def scores(kt, slot):
            for sb in range(nsub):
                sbuf[slot, sb, 0:kt.shape[0]] = _mm(kt, qts[sb])

        def consume(slot, vt, ms):
            out = []
            for sb in range(nsub):
                s = sbuf[slot, sb, 0:vt.shape[1]]
                m_new = jnp.maximum(ms[sb], jnp.max(s, axis=0, keepdims=True))
                p = jnp.exp2(s - m_new).astype(BF16)
                acc_ref[sb] = jnp.exp2(ms[sb] - m_new) * acc_ref[sb] + _mm(vt, p)
                out.append(m_new)
            return tuple(out)

        def key_chunk(t):
            start = t * tk if isinstance(t, int) else pl.multiple_of(t * tk, tk)
            return k_ref[pl.ds(start, tk), :]

        ms = tuple(jnp.full((1, width), -jnp.inf, F32) for _ in range(nsub))
        run(scores, consume, k_ref[0:tk, :], key_chunk, kc_ref[...], ms)

    for sb in range(nsub):
        acc = acc_ref[sb]
        ot = acc[0:HEAD_DIM] * (1.0 / acc[HEAD_DIM:HEAD_DIM + 1])
        o = jnp.concatenate([ot[:, gi * tq:(gi + 1) * tq].T for gi in range(GROUP)], axis=1)
        rows = slice(sb * tq, (sb + 1) * tq)
        o_ref[rows, :] = (o * sz_ref[rows, :].astype(F32)).astype(BF16)


def _attention(q, k, v, kc, vc, sz, *, tq, tk, nsub):
    b, l, qd = q.shape
    c = kc.shape[1]
    n_kv = qd // GROUP_W
    n_k = l // tk
    assert n_k % 2 == 0 and n_k >= 2 and c <= tk
    qspec = pl.BlockSpec((None, nsub * tq, GROUP_W), lambda bb, hh, i: (bb, i, hh))
    kspec = pl.BlockSpec((None, l, HEAD_DIM), lambda bb, hh, i: (bb, 0, hh))
    cspec = pl.BlockSpec((None, c, HEAD_DIM), lambda bb, hh, i: (bb, 0, hh))
    return pl.pallas_call(
        functools.partial(_attn_kernel, tq=tq, tk=tk, n_k=n_k, c=c, nsub=nsub),
        out_shape=jax.ShapeDtypeStruct((b, l, qd), BF16),
        grid=(b, n_kv, l // (nsub * tq)),
        in_specs=[qspec, kspec, kspec, cspec, cspec, qspec],
        out_specs=qspec,
        scratch_shapes=[pltpu.VMEM((n_k + 1, VT_ROWS, tk), BF16),
                        pltpu.VMEM((n_k + 1, tk, 2 * HEAD_DIM), BF16),
                        pltpu.VMEM((8, LANES), F32),
                        pltpu.VMEM((nsub, VT_ROWS, GROUP * tq), F32),
                        pltpu.VMEM((2, nsub, tk, GROUP * tq), F32)],
        compiler_params=pltpu.CompilerParams(
            dimension_semantics=("parallel", "parallel", "arbitrary")),
        name="flash_attention",
    )(q, k, v, kc, vc, sz)


def _rope_tables(l):
    rows = l // GRID_W
    axis_dim = HEAD_DIM // 2
    row = np.repeat(np.arange(rows, dtype=np.float64), GRID_W)
    col = np.tile(np.arange(GRID_W, dtype=np.float64), rows)
    inv = 1.0 / (ROPE_THETA ** (np.arange(0, axis_dim, 2, dtype=np.float64) / axis_dim))
    ra, ca = row[:, None] * inv[None, :], col[:, None] * inv[None, :]
    cos = np.concatenate([np.cos(ra), np.cos(ra), np.cos(ca), np.cos(ca)], axis=-1)
    sin = np.concatenate([-np.sin(ra), np.sin(ra), -np.sin(ca), np.sin(ca)], axis=-1)
    return jnp.asarray(cos, F32), jnp.asarray(sin, F32)


def _pick(n, target):
    t = min(n, target)
    while n % t:
        t //= 2
    return t


def kernel(x, c, ctx, c_ctx, norm_g, ada_w, ada_b,
           hy_w_in, hy_conv_w, hy_conv_b, hy_fw1, hy_fb1, hy_fw2, hy_fb2, hy_fw3, hy_fb3,
           hy_fw4, hy_freq, hy_d, hy_w_out,
           at_w_in, at_q_g, at_k_g, at_w_out):
    b, l, d = x.shape
    n_ctx = ctx.shape[1]
    e = hy_d.shape[-1]

    cv = jnp.concatenate([c, c_ctx[None], jnp.zeros((8 - b - 1, d), F32)], axis=0)
    mod = _modulation(cv, ada_w, ada_b)

    def lat(v):
        return v[:b, None, :]

    def cx(v):
        return jnp.broadcast_to(v[b][None, None, :], (b, 1, d))

    sh0, sc0, gt0 = jnp.split(mod[0], 3, axis=-1)
    w_in = hy_w_in[0].astype(BF16)
    w_out = hy_w_out[0].astype(BF16)
    fargs = (hy_fw1[0], hy_fb1[0], hy_fw2[0], hy_fb2[0], hy_fw3[0], hy_fb3[0], hy_fw4[0], hy_freq[0])
    tn = _pick(e, 256)
    cb = _pick(e, 512)

    u_lat, g_lat = _hy_proj(x, norm_g[0], lat(sc0), lat(sh0), w_in, hy_conv_w[0], hy_conv_b[0],
                            tm=_pick(l, 1024), tn=tn)
    u_ctx, g_ctx = _hy_proj(ctx, norm_g[0], cx(sc0), cx(sh0), w_in, hy_conv_w[0], hy_conv_b[0],
                            tm=_pick(n_ctx, 1024), tn=tn)
    k_lat, s_lat = _hyena_filter(l, *fargs, tr=_pick(l, 256))
    k_ctx, s_ctx = _hyena_filter(n_ctx, *fargs, tr=_pick(n_ctx, 256))
    y_lat = _long_conv_latent(u_lat, k_lat, s_lat, nb=16, cb=cb, kb=4)
    y_ctx = _long_conv_ctx(u_ctx, k_ctx, s_ctx, cb=cb)
    x1 = _hy_out(y_lat, u_lat, g_lat, hy_d[0], w_out, x, lat(gt0), tm=_pick(l, 512))
    ctx1 = _hy_out(y_ctx, u_ctx, g_ctx, hy_d[0], w_out, ctx, cx(gt0), tm=_pick(n_ctx, 512))

    sh1, sc1, gt1 = jnp.split(mod[1], 3, axis=-1)
    aw_in = at_w_in[0].astype(BF16)
    aw_out = at_w_out[0].astype(BF16)
    cos, sin = _rope_tables(l)
    q, k, v, sz = _at_proj(x1, norm_g[1], lat(sc1), lat(sh1), aw_in, at_q_g[0], at_k_g[0], cos, sin,
                           tm=_pick(l, 512))
    kc, vc = _at_proj_kv(ctx1, norm_g[1], cx(sc1), cx(sh1), aw_in, at_k_g[0], qd=q.shape[-1])
    om = _attention(q, k, v, kc, vc, sz, tq=_pick(l // 2, 256), tk=_pick(l // 2, 512), nsub=2)
    return _at_out(om, aw_out, x1, lat(gt1), tm=_pick(l, 512))
```

```python
import functools
import math

import numpy as np
import jax
import jax.numpy as jnp
from jax import lax
from jax.experimental import pallas as pl
from jax.experimental.pallas import tpu as pltpu

F32 = jnp.float32
BF16 = jnp.bfloat16
HIGHEST = lax.Precision.HIGHEST

EPS = 1e-6
HEAD_DIM = 128
GROUP = 4
GROUP_W = GROUP * HEAD_DIM
GRID_W = 64
ROPE_THETA = 10000.0
FILTER_BANDS = 16
DECAY_TARGET = 1e-2
MAX_DECAY = math.log(DECAY_TARGET) / 0.3
MIN_DECAY = math.log(DECAY_TARGET) / 1.5
HALO = 16
LANES = 128
VT_ROWS = HEAD_DIM + 16
LOG2E = math.log2(math.e)
KV_UNROLL = 4
STAB_KEYS = 128
STAB_GAP = 90.0


def _sigmoid(x):
    return 1.0 / (1.0 + jnp.exp(-x))


def _mm(a, b):
    return jnp.dot(a, b, preferred_element_type=F32)


def _mm_f32(a, b):
    return jnp.dot(a, b, preferred_element_type=F32, precision=HIGHEST)


def _mod_kernel(c_ref, w_ref, b_ref, o_ref):
    cv = c_ref[...]
    o_ref[...] = _mm_f32(cv * _sigmoid(cv), w_ref[...]) + b_ref[...]


def _modulation(cv, ada_w, ada_b):
    depth, d, d3 = ada_w.shape
    rows = cv.shape[0]
    tn = d
    return pl.pallas_call(
        _mod_kernel,
        out_shape=jax.ShapeDtypeStruct((depth, rows, d3), F32),
        grid=(depth, d3 // tn),
        in_specs=[
            pl.BlockSpec((rows, d), lambda l, j: (0, 0)),
            pl.BlockSpec((None, d, tn), lambda l, j: (l, 0, j)),
            pl.BlockSpec((None, 1, tn), lambda l, j: (l, 0, j)),
        ],
        out_specs=pl.BlockSpec((None, rows, tn), lambda l, j: (l, 0, j)),
        name="modulation",
    )(cv, ada_w, ada_b.reshape(depth, 1, d3))


def _norm_mod(x, g, sc, sh):
    ms = jnp.mean(x * x, axis=-1, keepdims=True)
    return x * lax.rsqrt(ms + EPS) * (g * (1.0 + sc)) + sh


def _hy_proj_kernel(xm_ref, xp_ref, xn_ref, g_ref, sc_ref, sh_ref, w_ref, cw_ref, cb_ref,
                    u_ref, gate_ref, hbuf, *, tm, tn, n_i):
    i = pl.program_id(1)
    e = u_ref.shape[-1]
    g, sc, sh = g_ref[...], sc_ref[...], sh_ref[...]
    hbuf[HALO:HALO + tm, :] = _norm_mod(xm_ref[...], g, sc, sh).astype(BF16)
    hp = _norm_mod(xp_ref[...], g, sc, sh)
    hbuf[0:HALO, :] = jnp.where(i > 0, hp, 0.0).astype(BF16)
    hn = _norm_mod(xn_ref[...], g, sc, sh)
    hbuf[HALO + tm:, :] = jnp.where(i < n_i - 1, hn, 0.0).astype(BF16)

    def conv(lo):
        p = _mm(hbuf[...], w_ref[:, lo:lo + tn])
        cw = cw_ref[:, lo:lo + tn]
        rows = p.shape[0]
        prev = pltpu.roll(p, 1, 0)[HALO:HALO + tm]
        nxt = pltpu.roll(p, rows - 1, 0)[HALO:HALO + tm]
        return prev * cw[0:1] + p[HALO:HALO + tm] * cw[1:2] + nxt * cw[2:3] + cb_ref[:, lo:lo + tn]

    for j in range(e // tn):
        lo = j * tn
        u_ref[:, lo:lo + tn] = (conv(e + lo) * conv(2 * e + lo)).astype(BF16)
        z = _mm(hbuf[HALO:HALO + tm, :], w_ref[:, 3 * e + lo:3 * e + lo + tn])
        gate_ref[:, lo:lo + tn] = (conv(lo) * z * _sigmoid(z)).astype(BF16)


def _hy_proj(x, g, sc, sh, w_bf, conv_w, conv_b, *, tm, tn):
    b, l, d = x.shape
    e = w_bf.shape[1] // 4
    n_i = l // tm
    hb = tm // HALO
    last_hb = l // HALO - 1
    const = lambda shape: pl.BlockSpec(shape, lambda bb, i: (0, 0), pipeline_mode=pl.Buffered(1))
    vec = pl.BlockSpec((None, 1, d), lambda bb, i: (bb, 0, 0))
    out = pl.BlockSpec((None, tm, e), lambda bb, i: (bb, i, 0))
    return pl.pallas_call(
        functools.partial(_hy_proj_kernel, tm=tm, tn=tn, n_i=n_i),
        out_shape=(jax.ShapeDtypeStruct((b, l, e), BF16), jax.ShapeDtypeStruct((b, l, e), BF16)),
        grid=(b, n_i),
        in_specs=[
            pl.BlockSpec((None, tm, d), lambda bb, i: (bb, i, 0)),
            pl.BlockSpec((None, HALO, d), lambda bb, i: (bb, jnp.maximum(i * hb - 1, 0), 0)),
            pl.BlockSpec((None, HALO, d), lambda bb, i: (bb, jnp.minimum((i + 1) * hb, last_hb), 0)),
            const((1, d)), vec, vec,
            const((d, 4 * e)), const((3, 3 * e)), const((1, 3 * e)),
        ],
        out_specs=(out, out),
        scratch_shapes=[pltpu.VMEM((tm + 2 * HALO, d), BF16)],
        compiler_params=pltpu.CompilerParams(dimension_semantics=("parallel", "parallel")),
        name="hyena_in_proj",
    )(x, x, x, g.reshape(1, d), sc, sh, w_bf, conv_w, conv_b.reshape(1, -1))


def _split_bf16(a):
    hi = a.astype(BF16)
    return hi, (a - hi.astype(F32)).astype(BF16)


def _mm_split(w, a):
    w_hi, w_lo = _split_bf16(w)
    a_hi, a_lo = _split_bf16(a)
    return _mm(w_hi, a_hi) + _mm(w_lo, a_hi) + _mm(w_hi, a_lo)


def _filter_kernel(emb_ref, w1_ref, w2_ref, w3_ref, col_ref, w4_ref, dl_ref, k_ref, s_ref, *, l, tr):
    t_idx = pl.program_id(0)
    r_row = t_idx * tr + lax.broadcasted_iota(jnp.int32, (1, tr), 1)
    p_row = jnp.where(r_row < l, r_row, 2 * l - r_row).astype(F32)
    emb = emb_ref[...]
    z = (emb[:, 2:3] * (p_row / (l - 1.0))
         + emb[:, 3:4] * jnp.cos(emb[:, 0:1] * ((2.0 * math.pi) * p_row / l) + emb[:, 1:2]))
    col = col_ref[...]
    fq = col[:, 3:4]
    h = jnp.sin(fq * (_mm_split(w1_ref[...], z) + col[:, 0:1]))
    h = jnp.sin(fq * (_mm_split(w2_ref[...], h) + col[:, 1:2]))
    h = jnp.sin(fq * (_mm_split(w3_ref[...], h) + col[:, 2:3]))
    h_hi, h_lo = _split_bf16(h)
    w_hi, w_lo = _split_bf16(w4_ref[...])
    k = lax.dot_general(jnp.concatenate([h_hi, h_lo, h_hi], axis=0),
                        jnp.concatenate([w_hi, w_hi, w_lo], axis=0),
                        (((0,), (0,)), ((), ())), preferred_element_type=F32)
    r_col = t_idx * tr + lax.broadcasted_iota(jnp.int32, (tr, 1), 0)
    t_col = jnp.where(r_col < l, r_col, 2 * l - r_col).astype(F32) / (l - 1.0)
    k = jnp.where(r_col == l, 0.0, k * jnp.exp(-t_col * dl_ref[...]))
    k_ref[...] = k.astype(BF16)

    @pl.when(t_idx == 0)
    def _():
        s_ref[...] = jnp.zeros_like(s_ref)

    s_ref[...] += jnp.sum(jnp.abs(k), axis=0, keepdims=True)


EMB_ROWS = 48


def _filter_embedding_table():
    fr = np.linspace(1e-4, FILTER_BANDS - 1, FILTER_BANDS)
    emb = np.zeros((EMB_ROWS, 4), np.float32)
    emb[1:1 + FILTER_BANDS, 0] = fr
    emb[1 + FILTER_BANDS:1 + 2 * FILTER_BANDS, 0] = fr
    emb[1 + FILTER_BANDS:1 + 2 * FILTER_BANDS, 1] = math.pi / 2
    emb[0, 2] = 1.0
    emb[1:1 + 2 * FILTER_BANDS, 3] = 1.0
    return emb


def _hyena_filter(l, fw1, fb1, fw2, fb2, fw3, fb3, fw4, freq, *, tr):
    width = fw1.shape[1]
    e = fw4.shape[1] // 2
    w1t = jnp.zeros((width, EMB_ROWS), F32).at[:, :fw1.shape[0]].set(fw1.T)
    cols = jnp.stack([fb1, fb2, fb3, freq], axis=1)
    deltas = jnp.abs(jnp.linspace(MIN_DECAY, MAX_DECAY, e, dtype=F32)).reshape(1, e)
    half = l // tr
    small = lambda shape: pl.BlockSpec(shape, lambda t: (0, 0))
    return pl.pallas_call(
        functools.partial(_filter_kernel, l=l, tr=tr),
        out_shape=(jax.ShapeDtypeStruct((2 * l, e), BF16), jax.ShapeDtypeStruct((1, e), F32)),
        grid=(2 * half,),
        in_specs=[
            small((EMB_ROWS, 4)), small((width, EMB_ROWS)), small((width, width)), small((width, width)),
            small((width, 4)),
            pl.BlockSpec((width, e), lambda t: (0, t // half)),
            small((1, e)),
        ],
        out_specs=(pl.BlockSpec((tr, e), lambda t: (t, 0)), pl.BlockSpec((1, e), lambda t: (0, 0))),
        compiler_params=pltpu.CompilerParams(dimension_semantics=("arbitrary",)),
        name="hyena_filter",
    )(jnp.asarray(_filter_embedding_table()), w1t, fw2.T, fw3.T, cols, fw4, deltas)


@functools.lru_cache(maxsize=None)
def _four_step_consts(n1, n2):
    n = n1 * n2
    hh = n1 // 2
    k1 = np.arange(n1)
    c2 = np.arange(n2)

    def angles(sign, n1_count):
        nn = n2 * np.arange(n1_count)[None, None, :] + c2[:, None, None]
        return sign * 2.0 * np.pi * ((k1[None, :, None] * nn) % n) / n

    a = angles(-1.0, hh)
    c, s = np.cos(a), np.sin(a)
    g = np.zeros((n2, n1, 2, 2, hh))
    g[:, :, 0, 0], g[:, :, 0, 1], g[:, :, 1, 0], g[:, :, 1, 1] = c, -s, s, c
    g = g.reshape(n2, 2 * n1, n1)

    a = angles(-1.0, n1)
    gk = np.stack([np.cos(a), np.sin(a)], axis=2).reshape(n2, 2 * n1, n1)

    a = np.transpose(angles(1.0, hh), (0, 2, 1))
    c, s = np.cos(a) / n, np.sin(a) / n
    h = np.zeros((n2, 2, hh, n1, 2))
    h[:, 0, :, :, 0], h[:, 0, :, :, 1], h[:, 1, :, :, 0], h[:, 1, :, :, 1] = c, -s, s, c
    h = h.reshape(n2, n1, 2 * n1)

    phi = -2.0 * np.pi * ((c2[:, None] * c2[None, :]) % n2) / n2
    c, s = np.cos(phi), np.sin(phi)
    fb = np.block([[c, -s], [s, c]])
    fbi = np.block([[c, s], [-s, c]])
    f = lambda x: np.asarray(x, np.float32)
    return f(g), f(gk), f(h), f(fb), f(fbi)


@functools.lru_cache(maxsize=None)
def _direct_dft_consts(nc):
    hh = nc // 2
    k = np.arange(nc)
    a = -2.0 * np.pi * ((k[:, None] * np.arange(hh)[None, :]) % nc) / nc
    c, s = np.cos(a), np.sin(a)
    fc = np.block([[c, -s], [s, c]])
    a = -2.0 * np.pi * ((k[:, None] * k[None, :]) % nc) / nc
    fck = np.concatenate([np.cos(a), np.sin(a)], axis=0)
    a = 2.0 * np.pi * ((np.arange(hh)[:, None] * k[None, :]) % nc) / nc
    c, s = np.cos(a) / nc, np.sin(a) / nc
    fci = np.block([[c, -s], [s, c]])
    f = lambda x: np.asarray(x, np.float32)
    return f(fc), f(fck), f(fci)


def _colmm_kernel(g_ref, x_ref, o_ref, xs_ref, rs_ref, *, nb):
    xs_ref[...] = pltpu.einshape("abc->bac", x_ref[...])
    for q in range(nb):
        rs_ref[q] = _mm(g_ref[q].astype(BF16), xs_ref[q]).astype(rs_ref.dtype)
    o_ref[...] = pltpu.einshape("abc->bac", rs_ref[...])


def _colmm(gm, x3, *, nb, cb, name):
    n2, r_out, r_in = gm.shape
    e = x3.shape[-1]
    return pl.pallas_call(
        functools.partial(_colmm_kernel, nb=nb),
        out_shape=jax.ShapeDtypeStruct((r_out, n2, e), BF16),
        grid=(n2 // nb, e // cb),
        in_specs=[
            pl.BlockSpec((nb, r_out, r_in), lambda s, j: (s, 0, 0)),
            pl.BlockSpec((r_in, nb, cb), lambda s, j: (0, s, j)),
        ],
        out_specs=pl.BlockSpec((r_out, nb, cb), lambda s, j: (0, s, j)),
        scratch_shapes=[pltpu.VMEM((nb, r_in, cb), BF16), pltpu.VMEM((nb, r_out, cb), BF16)],
        compiler_params=pltpu.CompilerParams(dimension_semantics=("parallel", "parallel")),
        name=name,
    )(gm, x3)


def _cmul(x, k, half):
    xr, xi = x[:half], x[half:]
    kr, ki = k[:half], k[half:]
    return jnp.concatenate([xr * kr - xi * ki, xr * ki + xi * kr], axis=0)


def _spectral_mul_kernel(fb_ref, fbi_ref, a_ref, ak_ref, s_ref, z_ref, *, kb):
    fb, fbi = fb_ref[...].astype(BF16), fbi_ref[...].astype(BF16)
    half = fb.shape[0] // 2
    inv = 1.0 / s_ref[...]
    xs = [_mm(fb, a_ref[q]) for q in range(kb)]
    ks = [_mm(fb, ak_ref[q]) * inv for q in range(kb)]
    ys = [_cmul(xs[q], ks[q], half).astype(BF16) for q in range(kb)]
    for q in range(kb):
        z_ref[q] = _mm(fbi, ys[q]).astype(BF16)


def _spectral_mul(fb, fbi, a3, ak3, s, *, kb, cb):
    n1, r, e = a3.shape
    slab = pl.BlockSpec((kb, r, cb), lambda i, j: (i, 0, j))
    mat = pl.BlockSpec((r, r), lambda i, j: (0, 0))
    return pl.pallas_call(
        functools.partial(_spectral_mul_kernel, kb=kb),
        out_shape=jax.ShapeDtypeStruct((n1, r, e), BF16),
        grid=(n1 // kb, e // cb),
        in_specs=[mat, mat, slab, slab, pl.BlockSpec((1, cb), lambda i, j: (0, j))],
        out_specs=slab,
        compiler_params=pltpu.CompilerParams(dimension_semantics=("parallel", "parallel")),
        name="hyena_spectral_mul",
    )(fb, fbi, a3, ak3, s)


def _ctx_conv_kernel(fc_ref, fck_ref, fci_ref, u_ref, k_ref, s_ref, y_ref):
    x = _mm(fc_ref[...].astype(BF16), u_ref[...])
    ks = _mm(fck_ref[...].astype(BF16), k_ref[...]) * (1.0 / s_ref[...])
    y = _cmul(x, ks, x.shape[0] // 2).astype(BF16)
    y_ref[...] = _mm(fci_ref[...].astype(BF16), y).astype(BF16)


def _ctx_conv(fc, fck, fci, u2, k2, s, *, cb):
    nc, e = u2.shape
    full = lambda a: pl.BlockSpec(a.shape, lambda j: (0, 0))
    col = pl.BlockSpec((nc, cb), lambda j: (0, j))
    return pl.pallas_call(
        _ctx_conv_kernel,
        out_shape=jax.ShapeDtypeStruct((nc, e), BF16),
        grid=(e // cb,),
        in_specs=[full(fc), full(fck), full(fci), col, col, pl.BlockSpec((1, cb), lambda j: (0, j))],
        out_specs=col,
        compiler_params=pltpu.CompilerParams(dimension_semantics=("parallel",)),
        name="hyena_ctx_conv",
    )(fc, fck, fci, u2, k2, s)


def _long_conv_latent(u, k_raw, s, *, nb, cb, kb):
    b, l, e = u.shape
    n = 2 * l
    n1 = 1 << ((n.bit_length() - 1 + 1) // 2)
    n2 = n // n1
    g, gk, h, fb, fbi = (jnp.asarray(m) for m in _four_step_consts(n1, n2))
    a = _colmm(g, u.reshape(n1, n2, e), nb=nb, cb=cb, name="hyena_dft_rows")
    ak = _colmm(gk, k_raw.reshape(n1, n2, e), nb=nb, cb=cb, name="hyena_filter_dft_rows")
    z = _spectral_mul(fb, fbi, a.reshape(n1, 2 * n2, e), ak.reshape(n1, 2 * n2, e), s, kb=kb, cb=cb)
    y = _colmm(h, z.reshape(2 * n1, n2, e), nb=nb, cb=cb, name="hyena_idft_rows")
    return y.reshape(b, l, e)


def _long_conv_ctx(u, k_raw, s, *, cb):
    b, c, e = u.shape
    fc, fck, fci = (jnp.asarray(m) for m in _direct_dft_consts(2 * c))
    return _ctx_conv(fc, fck, fci, u.reshape(b * c, e), k_raw, s, cb=cb).reshape(b, c, e)


def _hy_out_kernel(y_ref, u_ref, g_ref, d_ref, w_ref, x_ref, gt_ref, o_ref):
    m = (y_ref[...].astype(F32) + u_ref[...].astype(F32) * d_ref[...]) * g_ref[...].astype(F32)
    o_ref[...] = x_ref[...] + gt_ref[...] * _mm(m.astype(BF16), w_ref[...])


def _hy_out(y, u, g, d_skip, w_bf, x, gt, *, tm):
    b, l, e = y.shape
    d = x.shape[-1]
    act = pl.BlockSpec((None, tm, e), lambda bb, i: (bb, i, 0))
    res = pl.BlockSpec((None, tm, d), lambda bb, i: (bb, i, 0))
    return pl.pallas_call(
        _hy_out_kernel,
        out_shape=jax.ShapeDtypeStruct(x.shape, F32),
        grid=(b, l // tm),
        in_specs=[act, act, act, pl.BlockSpec((1, e), lambda bb, i: (0, 0)),
                  pl.BlockSpec((e, d), lambda bb, i: (0, 0)), res,
                  pl.BlockSpec((None, 1, d), lambda bb, i: (bb, 0, 0))],
        out_specs=res,
        compiler_params=pltpu.CompilerParams(dimension_semantics=("parallel", "parallel")),
        name="hyena_out_proj",
    )(y, u, g, d_skip.reshape(1, e), w_bf, x, gt)


def _at_out_kernel(m_ref, w_ref, x_ref, gt_ref, o_ref):
    o_ref[...] = x_ref[...] + gt_ref[...] * _mm(m_ref[...], w_ref[...])


def _at_out(m, w_bf, x, gt, *, tm):
    b, l, e = m.shape
    d = x.shape[-1]
    res = pl.BlockSpec((None, tm, d), lambda bb, i: (bb, i, 0))
    return pl.pallas_call(
        _at_out_kernel,
        out_shape=jax.ShapeDtypeStruct(x.shape, F32),
        grid=(b, l // tm),
        in_specs=[pl.BlockSpec((None, tm, e), lambda bb, i: (bb, i, 0)),
                  pl.BlockSpec((e, d), lambda bb, i: (0, 0)), res,
                  pl.BlockSpec((None, 1, d), lambda bb, i: (bb, 0, 0))],
        out_specs=res,
        compiler_params=pltpu.CompilerParams(dimension_semantics=("parallel", "parallel")),
        name="attn_out_proj",
    )(m, w_bf, x, gt)


def _head_norm_rope(p, gain, cos, sin, scale):
    lane = lax.broadcasted_iota(jnp.int32, (1, HEAD_DIM), 1)
    first = (lane % (HEAD_DIM // 2)) < (HEAD_DIM // 4)
    outs = []
    for hd in range(p.shape[1] // HEAD_DIM):
        xh = p[:, hd * HEAD_DIM:(hd + 1) * HEAD_DIM]
        ms = jnp.mean(xh * xh, axis=-1, keepdims=True)
        yh = xh * lax.rsqrt(ms + EPS) * gain
        partner = jnp.where(first, pltpu.roll(yh, HEAD_DIM - HEAD_DIM // 4, 1),
                            pltpu.roll(yh, HEAD_DIM // 4, 1))
        outs.append((yh * cos + partner * sin) * scale)
    return jnp.concatenate(outs, axis=1)


def _at_proj_kernel(x_ref, g_ref, sc_ref, sh_ref, w_ref, qg_ref, kg_ref, cos_ref, sin_ref,
                    q_ref, k_ref, v_ref, sz_ref, hbuf):
    qd = q_ref.shape[-1]
    hbuf[...] = _norm_mod(x_ref[...], g_ref[...], sc_ref[...], sh_ref[...]).astype(BF16)

    def proj(lo):
        return _mm(hbuf[...], w_ref[:, lo:lo + GROUP_W])

    for lo in range(0, qd, GROUP_W):
        q_ref[:, lo:lo + GROUP_W] = _head_norm_rope(proj(lo), qg_ref[...], cos_ref[...], sin_ref[...],
                                                    HEAD_DIM ** -0.5 * LOG2E).astype(BF16)
    k_ref[...] = _head_norm_rope(proj(qd), kg_ref[...], cos_ref[...], sin_ref[...], 1.0).astype(BF16)
    v_ref[...] = proj(qd + GROUP_W).astype(BF16)
    for lo in range(0, qd, GROUP_W):
        p = proj(qd + 2 * GROUP_W + lo)
        sz_ref[:, lo:lo + GROUP_W] = (p * _sigmoid(p)).astype(BF16)


def _at_proj(x, g, sc, sh, w_bf, q_g, k_g, cos, sin, *, tm):
    b, l, d = x.shape
    kvd = GROUP_W
    qd = (w_bf.shape[1] - 2 * kvd) // 2
    const = lambda shape: pl.BlockSpec(shape, lambda bb, i: (0, 0), pipeline_mode=pl.Buffered(1))
    vec = pl.BlockSpec((None, 1, d), lambda bb, i: (bb, 0, 0))
    tab = pl.BlockSpec((tm, HEAD_DIM), lambda bb, i: (i, 0))
    wide = pl.BlockSpec((None, tm, qd), lambda bb, i: (bb, i, 0))
    narrow = pl.BlockSpec((None, tm, kvd), lambda bb, i: (bb, i, 0))
    return pl.pallas_call(
        _at_proj_kernel,
        out_shape=(jax.ShapeDtypeStruct((b, l, qd), BF16), jax.ShapeDtypeStruct((b, l, kvd), BF16),
                   jax.ShapeDtypeStruct((b, l, kvd), BF16), jax.ShapeDtypeStruct((b, l, qd), BF16)),
        grid=(b, l // tm),
        in_specs=[
            pl.BlockSpec((None, tm, d), lambda bb, i: (bb, i, 0)),
            const((1, d)), vec, vec, const(w_bf.shape),
            const((1, HEAD_DIM)), const((1, HEAD_DIM)), tab, tab,
        ],
        out_specs=(wide, narrow, narrow, wide),
        scratch_shapes=[pltpu.VMEM((tm, d), BF16)],
        compiler_params=pltpu.CompilerParams(dimension_semantics=("parallel", "parallel")),
        name="attn_in_proj",
    )(x, g.reshape(1, d), sc, sh, w_bf, q_g.reshape(1, -1), k_g.reshape(1, -1), cos, sin)


def _at_proj_kv_kernel(x_ref, g_ref, sc_ref, sh_ref, wk_ref, wv_ref, kg_ref, k_ref, v_ref):
    h = _norm_mod(x_ref[...], g_ref[...], sc_ref[...], sh_ref[...]).astype(BF16)
    p = _mm(h, wk_ref[...])
    gain = kg_ref[...]
    outs = []
    for hd in range(p.shape[1] // HEAD_DIM):
        xh = p[:, hd * HEAD_DIM:(hd + 1) * HEAD_DIM]
        ms = jnp.mean(xh * xh, axis=-1, keepdims=True)
        outs.append(xh * lax.rsqrt(ms + EPS) * gain)
    k_ref[...] = jnp.concatenate(outs, axis=1).astype(BF16)
    v_ref[...] = _mm(h, wv_ref[...]).astype(BF16)


def _at_proj_kv(x, g, sc, sh, w_bf, k_g, *, qd):
    b, c, d = x.shape
    kb = qd // GROUP_W
    vec = pl.BlockSpec((None, 1, d), lambda bb: (bb, 0, 0))
    out = pl.BlockSpec((None, c, GROUP_W), lambda bb: (bb, 0, 0))
    return pl.pallas_call(
        _at_proj_kv_kernel,
        out_shape=(jax.ShapeDtypeStruct((b, c, GROUP_W), BF16),) * 2,
        grid=(b,),
        in_specs=[
            pl.BlockSpec((None, c, d), lambda bb: (bb, 0, 0)),
            pl.BlockSpec((1, d), lambda bb: (0, 0)), vec, vec,
            pl.BlockSpec((d, GROUP_W), lambda bb: (0, kb)),
            pl.BlockSpec((d, GROUP_W), lambda bb: (0, kb + 1)),
            pl.BlockSpec((1, HEAD_DIM), lambda bb: (0, 0)),
        ],
        out_specs=(out, out),
        compiler_params=pltpu.CompilerParams(dimension_semantics=("parallel",)),
        name="attn_ctx_kv_proj",
    )(x, g.reshape(1, d), sc, sh, w_bf, w_bf, k_g.reshape(1, -1))


def _attn_kernel(q_ref, k_ref, v_ref, kc_ref, vc_ref, sz_ref, o_ref, vt_ref, ka_ref, kmax_ref, acc_ref,
                 sbuf, *, tq, tk, n_k, c, nsub):
    width = GROUP * tq

    @pl.when(pl.program_id(2) == 0)
    def _():
        ones = jnp.ones((VT_ROWS - HEAD_DIM, tk), BF16)
        def aug(rows):
            lane = lax.broadcasted_iota(jnp.int32, (rows, HEAD_DIM), 1)
            return jnp.where(lane < 2, 1.0, 0.0).astype(BF16)

        kk = jnp.zeros((1, 1), F32)
        chunks = [(t, k_ref[t * tk:(t + 1) * tk, :], v_ref[t * tk:(t + 1) * tk, :]) for t in range(n_k)]
        chunks.append((n_k, kc_ref[...], vc_ref[...]))
        for t, kt, vt in chunks:
            rows = kt.shape[0]
            vt_ref[t, 0:HEAD_DIM, 0:rows] = vt.astype(F32).T.astype(BF16)
            vt_ref[t, HEAD_DIM:, :] = ones
            ka_ref[t, 0:rows, 0:HEAD_DIM] = kt
            ka_ref[t, 0:rows, HEAD_DIM:] = aug(rows)
            kf = kt.astype(F32)
            kk = jnp.maximum(kk, jnp.max(jnp.sum(kf * kf, axis=1, keepdims=True), axis=0, keepdims=True))
        kmax_ref[...] = jnp.broadcast_to(jnp.sqrt(kk), kmax_ref.shape)

    qts, stab, gap = [], [], None
    for sb in range(nsub):
        q = q_ref[sb * tq:(sb + 1) * tq, :]
        qf = [q[:, gi * HEAD_DIM:(gi + 1) * HEAD_DIM].astype(F32).T for gi in range(GROUP)]
        qts.append(jnp.concatenate([x.astype(BF16) for x in qf], axis=1))
        qnorm = jnp.sqrt(jnp.concatenate([jnp.sum(x * x, axis=0, keepdims=True) for x in qf], axis=1))
        low = jnp.max(_mm(k_ref[0:STAB_KEYS, :], qts[sb]), axis=0, keepdims=True)
        stab.append(low)
        g = jnp.max(qnorm * kmax_ref[0:1, 0:1] - low)
        gap = g if gap is None else jnp.maximum(gap, g)
    acc_ref[...] = jnp.zeros_like(acc_ref)
    n_loop = (n_k - 2) // KV_UNROLL

    def run(scores, consume, first, chunk, ctx, carry):
        def body(i, carry):
            t = KV_UNROLL * i
            for u in range(KV_UNROLL):
                scores(chunk(t + u + 1), (u + 1) % 2)
                carry = consume(u % 2, vt_ref[t + u], carry)
            return carry

        scores(first, 0)
        carry = lax.fori_loop(0, n_loop, body, carry)
        for t in range(n_loop * KV_UNROLL, n_k):
            scores(chunk(t + 1) if t + 1 < n_k else ctx, (t + 1) % 2)
            carry = consume(t % 2, vt_ref[t], carry)
        consume(n_k % 2, vt_ref[n_k, :, 0:c], carry)

    def finish(denominator):
        for sb in range(nsub):
            acc = acc_ref[sb]
            ot = acc[0:HEAD_DIM] * (1.0 / denominator(acc))
            o = jnp.concatenate([ot[:, gi * tq:(gi + 1) * tq].T for gi in range(GROUP)], axis=1)
            rows = slice(sb * tq, (sb + 1) * tq)
            o_ref[rows, :] = (o * sz_ref[rows, :].astype(F32)).astype(BF16)

    @pl.when(gap <= STAB_GAP)
    def _():
        qas = []
        for sb in range(nsub):
            bits = lax.bitcast_convert_type(stab[sb], jnp.uint32) & jnp.uint32(0xFFFF0000)
            m_hi = lax.bitcast_convert_type(bits, F32)
            m_lo = stab[sb] - m_hi
            row = lax.broadcasted_iota(jnp.int32, (16, width), 0)
            extra = jnp.where(row == 0, -m_hi, jnp.where(row == 1, -m_lo, 0.0))
            qas.append(jnp.concatenate([qts[sb], extra.astype(BF16),
                                        jnp.zeros((HEAD_DIM - 16, width), BF16)], axis=0))

        def scores(kt, slot):
            for sb in range(nsub):
                sbuf[slot, sb, 0:kt.shape[0]] = _mm(kt, qas[sb])

        def consume(slot, vt, carry):
            for sb in range(nsub):
                p = jnp.exp2(sbuf[slot, sb, 0:vt.shape[1]])
                acc_ref[sb, HEAD_DIM:HEAD_DIM + 8] += jnp.sum(p.reshape(p.shape[0] // 8, 8, width), axis=0)
                acc_ref[sb, 0:HEAD_DIM] += _mm(vt[0:HEAD_DIM], p.astype(BF16))
            return carry

        run(scores, consume, ka_ref[0], lambda t: ka_ref[t], ka_ref[n_k, 0:c, :], 0)
        finish(lambda acc: jnp.sum(acc[HEAD_DIM:HEAD_DIM + 8], axis=0, keepdims=True))

    @pl.when(jnp.logical_not(gap <= STAB_GAP))
    def _():
        def scores(kt, slot):
            for sb in range(nsub):
                sbuf[slot, sb, 0:kt.shape[0]] = _mm(kt, qts[sb])

        def consume(slot, vt, ms):
            out = []
            for sb in range(nsub):
                s = sbuf[slot, sb, 0:vt.shape[1]]
                m_new = jnp.maximum(ms[sb], jnp.max(s, axis=0, keepdims=True))
                p = jnp.exp2(s - m_new).astype(BF16)
                acc_ref[sb] = jnp.exp2(ms[sb] - m_new) * acc_ref[sb] + _mm(vt, p)
                out.append(m_new)
            return tuple(out)

        def key_chunk(t):
            start = t * tk if isinstance(t, int) else pl.multiple_of(t * tk, tk)
            return k_ref[pl.ds(start, tk), :]

        ms = tuple(jnp.full((1, width), -jnp.inf, F32) for _ in range(nsub))
        run(scores, consume, k_ref[0:tk, :], key_chunk, kc_ref[...], ms)
        finish(lambda acc: acc[HEAD_DIM:HEAD_DIM + 1])


def _attention(q, k, v, kc, vc, sz, *, tq, tk, nsub):
    b, l, qd = q.shape
    c = kc.shape[1]
    n_kv = qd // GROUP_W
    n_k = l // tk
    assert n_k % 2 == 0 and n_k >= 2 and c <= tk
    qspec = pl.BlockSpec((None, nsub * tq, GROUP_W), lambda bb, hh, i: (bb, i, hh))
    kspec = pl.BlockSpec((None, l, HEAD_DIM), lambda bb, hh, i: (bb, 0, hh))
    cspec = pl.BlockSpec((None, c, HEAD_DIM), lambda bb, hh, i: (bb, 0, hh))
    return pl.pallas_call(
        functools.partial(_attn_kernel, tq=tq, tk=tk, n_k=n_k, c=c, nsub=nsub),
        out_shape=jax.ShapeDtypeStruct((b, l, qd), BF16),
        grid=(b, n_kv, l // (nsub * tq)),
        in_specs=[qspec, kspec, kspec, cspec, cspec, qspec],
        out_specs=qspec,
        scratch_shapes=[pltpu.VMEM((n_k + 1, VT_ROWS, tk), BF16),
                        pltpu.VMEM((n_k + 1, tk, 2 * HEAD_DIM), BF16),
                        pltpu.VMEM((8, LANES), F32),
                        pltpu.VMEM((nsub, VT_ROWS, GROUP * tq), F32),
                        pltpu.VMEM((2, nsub, tk, GROUP * tq), F32)],
        compiler_params=pltpu.CompilerParams(
            dimension_semantics=("parallel", "parallel", "arbitrary")),
        name="flash_attention",
    )(q, k, v, kc, vc, sz)


def _rope_tables(l):
    rows = l // GRID_W
    axis_dim = HEAD_DIM // 2
    row = np.repeat(np.arange(rows, dtype=np.float64), GRID_W)
    col = np.tile(np.arange(GRID_W, dtype=np.float64), rows)
    inv = 1.0 / (ROPE_THETA ** (np.arange(0, axis_dim, 2, dtype=np.float64) / axis_dim))
    ra, ca = row[:, None] * inv[None, :], col[:, None] * inv[None, :]
    cos = np.concatenate([np.cos(ra), np.cos(ra), np.cos(ca), np.cos(ca)], axis=-1)
    sin = np.concatenate([-np.sin(ra), np.sin(ra), -np.sin(ca), np.sin(ca)], axis=-1)
    return jnp.asarray(cos, F32), jnp.asarray(sin, F32)


def _pick(n, target):
    t = min(n, target)
    while n % t:
        t //= 2
    return t


def kernel(x, c, ctx, c_ctx, norm_g, ada_w, ada_b,
           hy_w_in, hy_conv_w, hy_conv_b, hy_fw1, hy_fb1, hy_fw2, hy_fb2, hy_fw3, hy_fb3,
           hy_fw4, hy_freq, hy_d, hy_w_out,
           at_w_in, at_q_g, at_k_g, at_w_out):
    b, l, d = x.shape
    n_ctx = ctx.shape[1]
    e = hy_d.shape[-1]

    cv = jnp.concatenate([c, c_ctx[None], jnp.zeros((8 - b - 1, d), F32)], axis=0)
    mod = _modulation(cv, ada_w, ada_b)

    def lat(v):
        return v[:b, None, :]

    def cx(v):
        return jnp.broadcast_to(v[b][None, None, :], (b, 1, d))

    sh0, sc0, gt0 = jnp.split(mod[0], 3, axis=-1)
    w_in = hy_w_in[0].astype(BF16)
    w_out = hy_w_out[0].astype(BF16)
    fargs = (hy_fw1[0], hy_fb1[0], hy_fw2[0], hy_fb2[0], hy_fw3[0], hy_fb3[0], hy_fw4[0], hy_freq[0])
    tn = _pick(e, 256)
    cb = _pick(e, 512)

    u_lat, g_lat = _hy_proj(x, norm_g[0], lat(sc0), lat(sh0), w_in, hy_conv_w[0], hy_conv_b[0],
                            tm=_pick(l, 1024), tn=tn)
    u_ctx, g_ctx = _hy_proj(ctx, norm_g[0], cx(sc0), cx(sh0), w_in, hy_conv_w[0], hy_conv_b[0],
                            tm=_pick(n_ctx, 1024), tn=tn)
    k_lat, s_lat = _hyena_filter(l, *fargs, tr=_pick(l, 256))
    k_ctx, s_ctx = _hyena_filter(n_ctx, *fargs, tr=_pick(n_ctx, 256))
    y_lat = _long_conv_latent(u_lat, k_lat, s_lat, nb=16, cb=cb, kb=4)
    y_ctx = _long_conv_ctx(u_ctx, k_ctx, s_ctx, cb=cb)
    x1 = _hy_out(y_lat, u_lat, g_lat, hy_d[0], w_out, x, lat(gt0), tm=_pick(l, 512))
    ctx1 = _hy_out(y_ctx, u_ctx, g_ctx, hy_d[0], w_out, ctx, cx(gt0), tm=_pick(n_ctx, 512))

    sh1, sc1, gt1 = jnp.split(mod[1], 3, axis=-1)
    aw_in = at_w_in[0].astype(BF16)
    aw_out = at_w_out[0].astype(BF16)
    cos, sin = _rope_tables(l)
    q, k, v, sz = _at_proj(x1, norm_g[1], lat(sc1), lat(sh1), aw_in, at_q_g[0], at_k_g[0], cos, sin,
                           tm=_pick(l, 512))
    kc, vc = _at_proj_kv(ctx1, norm_g[1], cx(sc1), cx(sh1), aw_in, at_k_g[0], qd=q.shape[-1])
    om = _attention(q, k, v, kc, vc, sz, tq=_pick(l // 2, 256), tk=_pick(l // 2, 512), nsub=2)
    return _at_out(om, aw_out, x1, lat(gt1), tm=_pick(l, 512))
```

```python
import functools
import math
from typing import NamedTuple

import numpy as np
import jax
import jax.numpy as jnp
from jax import lax
from jax.experimental import pallas as pl
from jax.experimental.pallas import tpu as pltpu

F32 = jnp.float32
BF16 = jnp.bfloat16
HIGHEST = lax.Precision.HIGHEST

EPS = 1e-6
HEAD_DIM = 128
GROUP = 4
GROUP_W = GROUP * HEAD_DIM
GRID_W = 64
ROPE_THETA = 10000.0
FILTER_BANDS = 16
DECAY_TARGET = 1e-2
MAX_DECAY = math.log(DECAY_TARGET) / 0.3
MIN_DECAY = math.log(DECAY_TARGET) / 1.5
HALO = 16
LANES = 128
VT_ROWS = HEAD_DIM + 16
LOG2E = math.log2(math.e)
KV_UNROLL = 4
STAB_KEYS = 128
STAB_GAP = 90.0


def _sigmoid(x):
    return 1.0 / (1.0 + jnp.exp(-x))


def _mm(a, b):
    return jnp.dot(a, b, preferred_element_type=F32)


def _mm_f32(a, b):
    return jnp.dot(a, b, preferred_element_type=F32, precision=HIGHEST)


def _mod_kernel(c_ref, w_ref, b_ref, o_ref):
    cv = c_ref[...]
    o_ref[...] = _mm_f32(cv * _sigmoid(cv), w_ref[...]) + b_ref[...]


def _modulation(cv, ada_w, ada_b):
    depth, d, d3 = ada_w.shape
    rows = cv.shape[0]
    tn = d
    return pl.pallas_call(
        _mod_kernel,
        out_shape=jax.ShapeDtypeStruct((depth, rows, d3), F32),
        grid=(depth, d3 // tn),
        in_specs=[
            pl.BlockSpec((rows, d), lambda l, j: (0, 0)),
            pl.BlockSpec((None, d, tn), lambda l, j: (l, 0, j)),
            pl.BlockSpec((None, 1, tn), lambda l, j: (l, 0, j)),
        ],
        out_specs=pl.BlockSpec((None, rows, tn), lambda l, j: (l, 0, j)),
        name="modulation",
    )(cv, ada_w, ada_b.reshape(depth, 1, d3))


def _norm_mod(x, g, sc, sh):
    ms = jnp.mean(x * x, axis=-1, keepdims=True)
    return x * lax.rsqrt(ms + EPS) * (g * (1.0 + sc)) + sh


def _hy_proj_kernel(xm_ref, xp_ref, xn_ref, g_ref, sc_ref, sh_ref, w_ref, cw_ref, cb_ref,
                    u_ref, gate_ref, hbuf, *, tm, tn, n_i):
    i = pl.program_id(1)
    e = u_ref.shape[-1]
    g, sc, sh = g_ref[...], sc_ref[...], sh_ref[...]
    hbuf[HALO:HALO + tm, :] = _norm_mod(xm_ref[...], g, sc, sh).astype(BF16)
    hp = _norm_mod(xp_ref[...], g, sc, sh)
    hbuf[0:HALO, :] = jnp.where(i > 0, hp, 0.0).astype(BF16)
    hn = _norm_mod(xn_ref[...], g, sc, sh)
    hbuf[HALO + tm:, :] = jnp.where(i < n_i - 1, hn, 0.0).astype(BF16)

    def conv(lo):
        p = _mm(hbuf[...], w_ref[:, lo:lo + tn])
        cw = cw_ref[:, lo:lo + tn]
        rows = p.shape[0]
        prev = pltpu.roll(p, 1, 0)[HALO:HALO + tm]
        nxt = pltpu.roll(p, rows - 1, 0)[HALO:HALO + tm]
        return prev * cw[0:1] + p[HALO:HALO + tm] * cw[1:2] + nxt * cw[2:3] + cb_ref[:, lo:lo + tn]

    for j in range(e // tn):
        lo = j * tn
        u_ref[:, lo:lo + tn] = (conv(e + lo) * conv(2 * e + lo)).astype(BF16)
        z = _mm(hbuf[HALO:HALO + tm, :], w_ref[:, 3 * e + lo:3 * e + lo + tn])
        gate_ref[:, lo:lo + tn] = (conv(lo) * z * _sigmoid(z)).astype(BF16)


def _hy_proj(x, g, sc, sh, w_bf, conv_w, conv_b, *, tm, tn):
    b, l, d = x.shape
    e = w_bf.shape[1] // 4
    n_i = l // tm
    hb = tm // HALO
    last_hb = l // HALO - 1
    const = lambda shape: pl.BlockSpec(shape, lambda bb, i: (0, 0), pipeline_mode=pl.Buffered(1))
    vec = pl.BlockSpec((None, 1, d), lambda bb, i: (bb, 0, 0))
    out = pl.BlockSpec((None, tm, e), lambda bb, i: (bb, i, 0))
    return pl.pallas_call(
        functools.partial(_hy_proj_kernel, tm=tm, tn=tn, n_i=n_i),
        out_shape=(jax.ShapeDtypeStruct((b, l, e), BF16), jax.ShapeDtypeStruct((b, l, e), BF16)),
        grid=(b, n_i),
        in_specs=[
            pl.BlockSpec((None, tm, d), lambda bb, i: (bb, i, 0)),
            pl.BlockSpec((None, HALO, d), lambda bb, i: (bb, jnp.maximum(i * hb - 1, 0), 0)),
            pl.BlockSpec((None, HALO, d), lambda bb, i: (bb, jnp.minimum((i + 1) * hb, last_hb), 0)),
            const((1, d)), vec, vec,
            const((d, 4 * e)), const((3, 3 * e)), const((1, 3 * e)),
        ],
        out_specs=(out, out),
        scratch_shapes=[pltpu.VMEM((tm + 2 * HALO, d), BF16)],
        compiler_params=pltpu.CompilerParams(dimension_semantics=("parallel", "parallel")),
        name="hyena_in_proj",
    )(x, x, x, g.reshape(1, d), sc, sh, w_bf, conv_w, conv_b.reshape(1, -1))


def _split_bf16(a):
    hi = a.astype(BF16)
    return hi, (a - hi.astype(F32)).astype(BF16)


def _mm_split(w, a):
    w_hi, w_lo = _split_bf16(w)
    a_hi, a_lo = _split_bf16(a)
    return _mm(w_hi, a_hi) + _mm(w_lo, a_hi) + _mm(w_hi, a_lo)


def _filter_kernel(emb_ref, w1_ref, w2_ref, w3_ref, col_ref, w4_ref, dl_ref, k_ref, s_ref, *, l, tr):
    t_idx = pl.program_id(0)
    r_row = t_idx * tr + lax.broadcasted_iota(jnp.int32, (1, tr), 1)
    p_row = jnp.where(r_row < l, r_row, 2 * l - r_row).astype(F32)
    emb = emb_ref[...]
    z = (emb[:, 2:3] * (p_row / (l - 1.0))
         + emb[:, 3:4] * jnp.cos(emb[:, 0:1] * ((2.0 * math.pi) * p_row / l) + emb[:, 1:2]))
    col = col_ref[...]
    fq = col[:, 3:4]
    h = jnp.sin(fq * (_mm_split(w1_ref[...], z) + col[:, 0:1]))
    h = jnp.sin(fq * (_mm_split(w2_ref[...], h) + col[:, 1:2]))
    h = jnp.sin(fq * (_mm_split(w3_ref[...], h) + col[:, 2:3]))
    h_hi, h_lo = _split_bf16(h)
    w_hi, w_lo = _split_bf16(w4_ref[...])
    k = lax.dot_general(jnp.concatenate([h_hi, h_lo, h_hi], axis=0),
                        jnp.concatenate([w_hi, w_hi, w_lo], axis=0),
                        (((0,), (0,)), ((), ())), preferred_element_type=F32)
    r_col = t_idx * tr + lax.broadcasted_iota(jnp.int32, (tr, 1), 0)
    t_col = jnp.where(r_col < l, r_col, 2 * l - r_col).astype(F32) / (l - 1.0)
    k = jnp.where(r_col == l, 0.0, k * jnp.exp(-t_col * dl_ref[...]))
    k_ref[...] = k.astype(BF16)

    @pl.when(t_idx == 0)
    def _():
        s_ref[...] = jnp.zeros_like(s_ref)

    s_ref[...] += jnp.sum(jnp.abs(k), axis=0, keepdims=True)


EMB_ROWS = 48


def _filter_embedding_table():
    fr = np.linspace(1e-4, FILTER_BANDS - 1, FILTER_BANDS)
    emb = np.zeros((EMB_ROWS, 4), np.float32)
    emb[1:1 + FILTER_BANDS, 0] = fr
    emb[1 + FILTER_BANDS:1 + 2 * FILTER_BANDS, 0] = fr
    emb[1 + FILTER_BANDS:1 + 2 * FILTER_BANDS, 1] = math.pi / 2
    emb[0, 2] = 1.0
    emb[1:1 + 2 * FILTER_BANDS, 3] = 1.0
    return emb


def _hyena_filter(l, fw1, fb1, fw2, fb2, fw3, fb3, fw4, freq, *, tr):
    width = fw1.shape[1]
    e = fw4.shape[1] // 2
    w1t = jnp.zeros((width, EMB_ROWS), F32).at[:, :fw1.shape[0]].set(fw1.T)
    cols = jnp.stack([fb1, fb2, fb3, freq], axis=1)
    deltas = jnp.abs(jnp.linspace(MIN_DECAY, MAX_DECAY, e, dtype=F32)).reshape(1, e)
    half = l // tr
    small = lambda shape: pl.BlockSpec(shape, lambda t: (0, 0))
    return pl.pallas_call(
        functools.partial(_filter_kernel, l=l, tr=tr),
        out_shape=(jax.ShapeDtypeStruct((2 * l, e), BF16), jax.ShapeDtypeStruct((1, e), F32)),
        grid=(2 * half,),
        in_specs=[
            small((EMB_ROWS, 4)), small((width, EMB_ROWS)), small((width, width)), small((width, width)),
            small((width, 4)),
            pl.BlockSpec((width, e), lambda t: (0, t // half)),
            small((1, e)),
        ],
        out_specs=(pl.BlockSpec((tr, e), lambda t: (t, 0)), pl.BlockSpec((1, e), lambda t: (0, 0))),
        compiler_params=pltpu.CompilerParams(dimension_semantics=("arbitrary",)),
        name="hyena_filter",
    )(jnp.asarray(_filter_embedding_table()), w1t, fw2.T, fw3.T, cols, fw4, deltas)


@functools.lru_cache(maxsize=None)
def _four_step_consts(n1, n2):
    n = n1 * n2
    hh = n1 // 2
    k1 = np.arange(n1)
    c2 = np.arange(n2)

    def angles(sign, n1_count):
        nn = n2 * np.arange(n1_count)[None, None, :] + c2[:, None, None]
        return sign * 2.0 * np.pi * ((k1[None, :, None] * nn) % n) / n

    a = angles(-1.0, hh)
    c, s = np.cos(a), np.sin(a)
    g = np.zeros((n2, n1, 2, 2, hh))
    g[:, :, 0, 0], g[:, :, 0, 1], g[:, :, 1, 0], g[:, :, 1, 1] = c, -s, s, c
    g = g.reshape(n2, 2 * n1, n1)

    a = angles(-1.0, n1)
    gk = np.stack([np.cos(a), np.sin(a)], axis=2).reshape(n2, 2 * n1, n1)

    a = np.transpose(angles(1.0, hh), (0, 2, 1))
    c, s = np.cos(a) / n, np.sin(a) / n
    h = np.zeros((n2, 2, hh, n1, 2))
    h[:, 0, :, :, 0], h[:, 0, :, :, 1], h[:, 1, :, :, 0], h[:, 1, :, :, 1] = c, -s, s, c
    h = h.reshape(n2, n1, 2 * n1)

    phi = -2.0 * np.pi * ((c2[:, None] * c2[None, :]) % n2) / n2
    c, s = np.cos(phi), np.sin(phi)
    fb = np.block([[c, -s], [s, c]])
    fbi = np.block([[c, s], [-s, c]])
    f = lambda x: np.asarray(x, np.float32)
    return f(g), f(gk), f(h), f(fb), f(fbi)


@functools.lru_cache(maxsize=None)
def _direct_dft_consts(nc):
    hh = nc // 2
    k = np.arange(nc)
    a = -2.0 * np.pi * ((k[:, None] * np.arange(hh)[None, :]) % nc) / nc
    c, s = np.cos(a), np.sin(a)
    fc = np.block([[c, -s], [s, c]])
    a = -2.0 * np.pi * ((k[:, None] * k[None, :]) % nc) / nc
    fck = np.concatenate([np.cos(a), np.sin(a)], axis=0)
    a = 2.0 * np.pi * ((np.arange(hh)[:, None] * k[None, :]) % nc) / nc
    c, s = np.cos(a) / nc, np.sin(a) / nc
    fci = np.block([[c, -s], [s, c]])
    f = lambda x: np.asarray(x, np.float32)
    return f(fc), f(fck), f(fci)


def _colmm_kernel(g_ref, x_ref, o_ref, xs_ref, rs_ref, *, nb):
    xs_ref[...] = pltpu.einshape("abc->bac", x_ref[...])
    for q in range(nb):
        rs_ref[q] = _mm(g_ref[q].astype(BF16), xs_ref[q]).astype(rs_ref.dtype)
    o_ref[...] = pltpu.einshape("abc->bac", rs_ref[...])


def _colmm(gm, x3, *, nb, cb, name):
    n2, r_out, r_in = gm.shape
    e = x3.shape[-1]
    return pl.pallas_call(
        functools.partial(_colmm_kernel, nb=nb),
        out_shape=jax.ShapeDtypeStruct((r_out, n2, e), BF16),
        grid=(n2 // nb, e // cb),
        in_specs=[
            pl.BlockSpec((nb, r_out, r_in), lambda s, j: (s, 0, 0)),
            pl.BlockSpec((r_in, nb, cb), lambda s, j: (0, s, j)),
        ],
        out_specs=pl.BlockSpec((r_out, nb, cb), lambda s, j: (0, s, j)),
        scratch_shapes=[pltpu.VMEM((nb, r_in, cb), BF16), pltpu.VMEM((nb, r_out, cb), BF16)],
        compiler_params=pltpu.CompilerParams(dimension_semantics=("parallel", "parallel")),
        name=name,
    )(gm, x3)


def _cmul(x, k, half):
    xr, xi = x[:half], x[half:]
    kr, ki = k[:half], k[half:]
    return jnp.concatenate([xr * kr - xi * ki, xr * ki + xi * kr], axis=0)


def _spectral_mul_kernel(fb_ref, fbi_ref, a_ref, ak_ref, s_ref, z_ref, *, kb):
    fb, fbi = fb_ref[...].astype(BF16), fbi_ref[...].astype(BF16)
    half = fb.shape[0] // 2
    inv = 1.0 / s_ref[...]
    xs = [_mm(fb, a_ref[q]) for q in range(kb)]
    ks = [_mm(fb, ak_ref[q]) * inv for q in range(kb)]
    ys = [_cmul(xs[q], ks[q], half).astype(BF16) for q in range(kb)]
    for q in range(kb):
        z_ref[q] = _mm(fbi, ys[q]).astype(BF16)


def _spectral_mul(fb, fbi, a3, ak3, s, *, kb, cb):
    n1, r, e = a3.shape
    slab = pl.BlockSpec((kb, r, cb), lambda i, j: (i, 0, j))
    mat = pl.BlockSpec((r, r), lambda i, j: (0, 0))
    return pl.pallas_call(
        functools.partial(_spectral_mul_kernel, kb=kb),
        out_shape=jax.ShapeDtypeStruct((n1, r, e), BF16),
        grid=(n1 // kb, e // cb),
        in_specs=[mat, mat, slab, slab, pl.BlockSpec((1, cb), lambda i, j: (0, j))],
        out_specs=slab,
        compiler_params=pltpu.CompilerParams(dimension_semantics=("parallel", "parallel")),
        name="hyena_spectral_mul",
    )(fb, fbi, a3, ak3, s)


def _ctx_conv_kernel(fc_ref, fck_ref, fci_ref, u_ref, k_ref, s_ref, y_ref):
    x = _mm(fc_ref[...].astype(BF16), u_ref[...])
    ks = _mm(fck_ref[...].astype(BF16), k_ref[...]) * (1.0 / s_ref[...])
    y = _cmul(x, ks, x.shape[0] // 2).astype(BF16)
    y_ref[...] = _mm(fci_ref[...].astype(BF16), y).astype(BF16)


def _ctx_conv(fc, fck, fci, u2, k2, s, *, cb):
    nc, e = u2.shape
    full = lambda a: pl.BlockSpec(a.shape, lambda j: (0, 0))
    col = pl.BlockSpec((nc, cb), lambda j: (0, j))
    return pl.pallas_call(
        _ctx_conv_kernel,
        out_shape=jax.ShapeDtypeStruct((nc, e), BF16),
        grid=(e // cb,),
        in_specs=[full(fc), full(fck), full(fci), col, col, pl.BlockSpec((1, cb), lambda j: (0, j))],
        out_specs=col,
        compiler_params=pltpu.CompilerParams(dimension_semantics=("parallel",)),
        name="hyena_ctx_conv",
    )(fc, fck, fci, u2, k2, s)


def _long_conv_latent(u, k_raw, s, *, nb, cb, kb):
    b, l, e = u.shape
    n = 2 * l
    n1 = 1 << ((n.bit_length() - 1 + 1) // 2)
    n2 = n // n1
    g, gk, h, fb, fbi = (jnp.asarray(m) for m in _four_step_consts(n1, n2))
    a = _colmm(g, u.reshape(n1, n2, e), nb=nb, cb=cb, name="hyena_dft_rows")
    ak = _colmm(gk, k_raw.reshape(n1, n2, e), nb=nb, cb=cb, name="hyena_filter_dft_rows")
    z = _spectral_mul(fb, fbi, a.reshape(n1, 2 * n2, e), ak.reshape(n1, 2 * n2, e), s, kb=kb, cb=cb)
    y = _colmm(h, z.reshape(2 * n1, n2, e), nb=nb, cb=cb, name="hyena_idft_rows")
    return y.reshape(b, l, e)


def _long_conv_ctx(u, k_raw, s, *, cb):
    b, c, e = u.shape
    fc, fck, fci = (jnp.asarray(m) for m in _direct_dft_consts(2 * c))
    return _ctx_conv(fc, fck, fci, u.reshape(b * c, e), k_raw, s, cb=cb).reshape(b, c, e)


def _hy_out_kernel(y_ref, u_ref, g_ref, d_ref, w_ref, x_ref, gt_ref, o_ref):
    m = (y_ref[...].astype(F32) + u_ref[...].astype(F32) * d_ref[...]) * g_ref[...].astype(F32)
    o_ref[...] = x_ref[...] + gt_ref[...] * _mm(m.astype(BF16), w_ref[...])


def _hy_out(y, u, g, d_skip, w_bf, x, gt, *, tm):
    b, l, e = y.shape
    d = x.shape[-1]
    act = pl.BlockSpec((None, tm, e), lambda bb, i: (bb, i, 0))
    res = pl.BlockSpec((None, tm, d), lambda bb, i: (bb, i, 0))
    return pl.pallas_call(
        _hy_out_kernel,
        out_shape=jax.ShapeDtypeStruct(x.shape, F32),
        grid=(b, l // tm),
        in_specs=[act, act, act, pl.BlockSpec((1, e), lambda bb, i: (0, 0)),
                  pl.BlockSpec((e, d), lambda bb, i: (0, 0)), res,
                  pl.BlockSpec((None, 1, d), lambda bb, i: (bb, 0, 0))],
        out_specs=res,
        compiler_params=pltpu.CompilerParams(dimension_semantics=("parallel", "parallel")),
        name="hyena_out_proj",
    )(y, u, g, d_skip.reshape(1, e), w_bf, x, gt)


def _at_out_kernel(m_ref, w_ref, x_ref, gt_ref, o_ref):
    o_ref[...] = x_ref[...] + gt_ref[...] * _mm(m_ref[...], w_ref[...])


def _at_out(m, w_bf, x, gt, *, tm):
    b, l, e = m.shape
    d = x.shape[-1]
    res = pl.BlockSpec((None, tm, d), lambda bb, i: (bb, i, 0))
    return pl.pallas_call(
        _at_out_kernel,
        out_shape=jax.ShapeDtypeStruct(x.shape, F32),
        grid=(b, l // tm),
        in_specs=[pl.BlockSpec((None, tm, e), lambda bb, i: (bb, i, 0)),
                  pl.BlockSpec((e, d), lambda bb, i: (0, 0)), res,
                  pl.BlockSpec((None, 1, d), lambda bb, i: (bb, 0, 0))],
        out_specs=res,
        compiler_params=pltpu.CompilerParams(dimension_semantics=("parallel", "parallel")),
        name="attn_out_proj",
    )(m, w_bf, x, gt)


def _head_norm_rope(p, gain, cos, sin, scale):
    lane = lax.broadcasted_iota(jnp.int32, (1, HEAD_DIM), 1)
    first = (lane % (HEAD_DIM // 2)) < (HEAD_DIM // 4)
    outs = []
    for hd in range(p.shape[1] // HEAD_DIM):
        xh = p[:, hd * HEAD_DIM:(hd + 1) * HEAD_DIM]
        ms = jnp.mean(xh * xh, axis=-1, keepdims=True)
        yh = xh * lax.rsqrt(ms + EPS) * gain
        partner = jnp.where(first, pltpu.roll(yh, HEAD_DIM - HEAD_DIM // 4, 1),
                            pltpu.roll(yh, HEAD_DIM // 4, 1))
        outs.append((yh * cos + partner * sin) * scale)
    return jnp.concatenate(outs, axis=1)


def _at_proj_kernel(x_ref, g_ref, sc_ref, sh_ref, w_ref, qg_ref, kg_ref, cos_ref, sin_ref,
                    q_ref, k_ref, v_ref, sz_ref, hbuf):
    qd = q_ref.shape[-1]
    hbuf[...] = _norm_mod(x_ref[...], g_ref[...], sc_ref[...], sh_ref[...]).astype(BF16)

    def proj(lo):
        return _mm(hbuf[...], w_ref[:, lo:lo + GROUP_W])

    for lo in range(0, qd, GROUP_W):
        q_ref[:, lo:lo + GROUP_W] = _head_norm_rope(proj(lo), qg_ref[...], cos_ref[...], sin_ref[...],
                                                    HEAD_DIM ** -0.5 * LOG2E).astype(BF16)
    k_ref[...] = _head_norm_rope(proj(qd), kg_ref[...], cos_ref[...], sin_ref[...], 1.0).astype(BF16)
    v_ref[...] = proj(qd + GROUP_W).astype(BF16)
    for lo in range(0, qd, GROUP_W):
        p = proj(qd + 2 * GROUP_W + lo)
        sz_ref[:, lo:lo + GROUP_W] = (p * _sigmoid(p)).astype(BF16)


def _at_proj(x, g, sc, sh, w_bf, q_g, k_g, cos, sin, *, tm):
    b, l, d = x.shape
    kvd = GROUP_W
    qd = (w_bf.shape[1] - 2 * kvd) // 2
    const = lambda shape: pl.BlockSpec(shape, lambda bb, i: (0, 0), pipeline_mode=pl.Buffered(1))
    vec = pl.BlockSpec((None, 1, d), lambda bb, i: (bb, 0, 0))
    tab = pl.BlockSpec((tm, HEAD_DIM), lambda bb, i: (i, 0))
    wide = pl.BlockSpec((None, tm, qd), lambda bb, i: (bb, i, 0))
    narrow = pl.BlockSpec((None, tm, kvd), lambda bb, i: (bb, i, 0))
    return pl.pallas_call(
        _at_proj_kernel,
        out_shape=(jax.ShapeDtypeStruct((b, l, qd), BF16), jax.ShapeDtypeStruct((b, l, kvd), BF16),
                   jax.ShapeDtypeStruct((b, l, kvd), BF16), jax.ShapeDtypeStruct((b, l, qd), BF16)),
        grid=(b, l // tm),
        in_specs=[
            pl.BlockSpec((None, tm, d), lambda bb, i: (bb, i, 0)),
            const((1, d)), vec, vec, const(w_bf.shape),
            const((1, HEAD_DIM)), const((1, HEAD_DIM)), tab, tab,
        ],
        out_specs=(wide, narrow, narrow, wide),
        scratch_shapes=[pltpu.VMEM((tm, d), BF16)],
        compiler_params=pltpu.CompilerParams(dimension_semantics=("parallel", "parallel")),
        name="attn_in_proj",
    )(x, g.reshape(1, d), sc, sh, w_bf, q_g.reshape(1, -1), k_g.reshape(1, -1), cos, sin)


def _at_proj_kv_kernel(x_ref, g_ref, sc_ref, sh_ref, wk_ref, wv_ref, kg_ref, k_ref, v_ref):
    h = _norm_mod(x_ref[...], g_ref[...], sc_ref[...], sh_ref[...]).astype(BF16)
    p = _mm(h, wk_ref[...])
    gain = kg_ref[...]
    outs = []
    for hd in range(p.shape[1] // HEAD_DIM):
        xh = p[:, hd * HEAD_DIM:(hd + 1) * HEAD_DIM]
        ms = jnp.mean(xh * xh, axis=-1, keepdims=True)
        outs.append(xh * lax.rsqrt(ms + EPS) * gain)
    k_ref[...] = jnp.concatenate(outs, axis=1).astype(BF16)
    v_ref[...] = _mm(h, wv_ref[...]).astype(BF16)


def _at_proj_kv(x, g, sc, sh, w_bf, k_g, *, qd):
    b, c, d = x.shape
    kb = qd // GROUP_W
    vec = pl.BlockSpec((None, 1, d), lambda bb: (bb, 0, 0))
    out = pl.BlockSpec((None, c, GROUP_W), lambda bb: (bb, 0, 0))
    return pl.pallas_call(
        _at_proj_kv_kernel,
        out_shape=(jax.ShapeDtypeStruct((b, c, GROUP_W), BF16),) * 2,
        grid=(b,),
        in_specs=[
            pl.BlockSpec((None, c, d), lambda bb: (bb, 0, 0)),
            pl.BlockSpec((1, d), lambda bb: (0, 0)), vec, vec,
            pl.BlockSpec((d, GROUP_W), lambda bb: (0, kb)),
            pl.BlockSpec((d, GROUP_W), lambda bb: (0, kb + 1)),
            pl.BlockSpec((1, HEAD_DIM), lambda bb: (0, 0)),
        ],
        out_specs=(out, out),
        compiler_params=pltpu.CompilerParams(dimension_semantics=("parallel",)),
        name="attn_ctx_kv_proj",
    )(x, g.reshape(1, d), sc, sh, w_bf, w_bf, k_g.reshape(1, -1))


def _attn_kernel(q_ref, k_ref, v_ref, kc_ref, vc_ref, sz_ref, o_ref, vt_ref, ka_ref, kmax_ref, acc_ref,
                 sbuf, *, tq, tk, n_k, c, nsub):
    width = GROUP * tq

    @pl.when(pl.program_id(2) == 0)
    def _():
        ones = jnp.ones((VT_ROWS - HEAD_DIM, tk), BF16)
        def aug(rows):
            lane = lax.broadcasted_iota(jnp.int32, (rows, HEAD_DIM), 1)
            return jnp.where(lane < 2, 1.0, 0.0).astype(BF16)

        kk = jnp.zeros((1, 1), F32)
        chunks = [(t, k_ref[t * tk:(t + 1) * tk, :], v_ref[t * tk:(t + 1) * tk, :]) for t in range(n_k)]
        chunks.append((n_k, kc_ref[...], vc_ref[...]))
        for t, kt, vt in chunks:
            rows = kt.shape[0]
            vt_ref[t, 0:HEAD_DIM, 0:rows] = vt.astype(F32).T.astype(BF16)
            vt_ref[t, HEAD_DIM:, :] = ones
            ka_ref[t, 0:rows, 0:HEAD_DIM] = kt
            ka_ref[t, 0:rows, HEAD_DIM:] = aug(rows)
            kf = kt.astype(F32)
            kk = jnp.maximum(kk, jnp.max(jnp.sum(kf * kf, axis=1, keepdims=True), axis=0, keepdims=True))
        kmax_ref[...] = jnp.broadcast_to(jnp.sqrt(kk), kmax_ref.shape)

    qts, stab, gap = [], [], None
    for sb in range(nsub):
        q = q_ref[sb * tq:(sb + 1) * tq, :]
        qf = [q[:, gi * HEAD_DIM:(gi + 1) * HEAD_DIM].astype(F32).T for gi in range(GROUP)]
        qts.append(jnp.concatenate([x.astype(BF16) for x in qf], axis=1))
        qnorm = jnp.sqrt(jnp.concatenate([jnp.sum(x * x, axis=0, keepdims=True) for x in qf], axis=1))
        low = jnp.max(_mm(k_ref[0:STAB_KEYS, :], qts[sb]), axis=0, keepdims=True)
        stab.append(low)
        g = jnp.max(qnorm * kmax_ref[0:1, 0:1] - low)
        gap = g if gap is None else jnp.maximum(gap, g)
    acc_ref[...] = jnp.zeros_like(acc_ref)
    n_loop = (n_k - 2) // KV_UNROLL

    def run(scores, consume, first, chunk, ctx, carry):
        def body(i, carry):
            t = KV_UNROLL * i
            for u in range(KV_UNROLL):
                scores(chunk(t + u + 1), (u + 1) % 2)
                carry = consume(u % 2, vt_ref[t + u], carry)
            return carry

        scores(first, 0)
        carry = lax.fori_loop(0, n_loop, body, carry)
        for t in range(n_loop * KV_UNROLL, n_k):
            scores(chunk(t + 1) if t + 1 < n_k else ctx, (t + 1) % 2)
            carry = consume(t % 2, vt_ref[t], carry)
        consume(n_k % 2, vt_ref[n_k, :, 0:c], carry)

    def finish(denominator):
        for sb in range(nsub):
            acc = acc_ref[sb]
            ot = acc[0:HEAD_DIM] * (1.0 / denominator(acc))
            o = jnp.concatenate([ot[:, gi * tq:(gi + 1) * tq].T for gi in range(GROUP)], axis=1)
            rows = slice(sb * tq, (sb + 1) * tq)
            o_ref[rows, :] = (o * sz_ref[rows, :].astype(F32)).astype(BF16)

    @pl.when(gap <= STAB_GAP)
    def _():
        qas = []
        for sb in range(nsub):
            bits = lax.bitcast_convert_type(stab[sb], jnp.uint32) & jnp.uint32(0xFFFF0000)
            m_hi = lax.bitcast_convert_type(bits, F32)
            m_lo = stab[sb] - m_hi
            row = lax.broadcasted_iota(jnp.int32, (16, width), 0)
            extra = jnp.where(row == 0, -m_hi, jnp.where(row == 1, -m_lo, 0.0))
            qas.append(jnp.concatenate([qts[sb], extra.astype(BF16),
                                        jnp.zeros((HEAD_DIM - 16, width), BF16)], axis=0))

        def scores(kt, slot):
            for sb in range(nsub):
                sbuf[slot, sb, 0:kt.shape[0]] = _mm(kt, qas[sb])

        def consume(slot, vt, carry):
            for sb in range(nsub):
                p = jnp.exp2(sbuf[slot, sb, 0:vt.shape[1]])
                acc_ref[sb, HEAD_DIM:HEAD_DIM + 8] += jnp.sum(p.reshape(p.shape[0] // 8, 8, width), axis=0)
                acc_ref[sb, 0:HEAD_DIM] += _mm(vt[0:HEAD_DIM], p.astype(BF16))
            return carry

        run(scores, consume, ka_ref[0], lambda t: ka_ref[t], ka_ref[n_k, 0:c, :], 0)
        finish(lambda acc: jnp.sum(acc[HEAD_DIM:HEAD_DIM + 8], axis=0, keepdims=True))

    @pl.when(jnp.logical_not(gap <= STAB_GAP))
    def _():
        def scores(kt, slot):
            for sb in range(nsub):
                sbuf[slot, sb, 0:kt.shape[0]] = _mm(kt, qts[sb])

        def consume(slot, vt, ms):
            out = []
            for sb in range(nsub):
                s = sbuf[slot, sb, 0:vt.shape[1]]
                m_new = jnp.maximum(ms[sb], jnp.max(s, axis=0, keepdims=True))
                p = jnp.exp2(s - m_new).astype(BF16)
                acc_ref[sb] = jnp.exp2(ms[sb] - m_new) * acc_ref[sb] + _mm(vt, p)
                out.append(m_new)
            return tuple(out)

        def key_chunk(t):
            start = t * tk if isinstance(t, int) else pl.multiple_of(t * tk, tk)
            return k_ref[pl.ds(start, tk), :]

        ms = tuple(jnp.full((1, width), -jnp.inf, F32) for _ in range(nsub))
        run(scores, consume, k_ref[0:tk, :], key_chunk, kc_ref[...], ms)
        finish(lambda acc: acc[HEAD_DIM:HEAD_DIM + 1])


def _attention(q, k, v, kc, vc, sz, *, tq, tk, nsub):
    b, l, qd = q.shape
    c = kc.shape[1]
    n_kv = qd // GROUP_W
    n_k = l // tk
    assert n_k % 2 == 0 and n_k >= 2 and c <= tk
    qspec = pl.BlockSpec((None, nsub * tq, GROUP_W), lambda bb, hh, i: (bb, i, hh))
    kspec = pl.BlockSpec((None, l, HEAD_DIM), lambda bb, hh, i: (bb, 0, hh))
    cspec = pl.BlockSpec((None, c, HEAD_DIM), lambda bb, hh, i: (bb, 0, hh))
    return pl.pallas_call(
        functools.partial(_attn_kernel, tq=tq, tk=tk, n_k=n_k, c=c, nsub=nsub),
        out_shape=jax.ShapeDtypeStruct((b, l, qd), BF16),
        grid=(b, n_kv, l // (nsub * tq)),
        in_specs=[qspec, kspec, kspec, cspec, cspec, qspec],
        out_specs=qspec,
        scratch_shapes=[pltpu.VMEM((n_k + 1, VT_ROWS, tk), BF16),
                        pltpu.VMEM((n_k + 1, tk, 2 * HEAD_DIM), BF16),
                        pltpu.VMEM((8, LANES), F32),
                        pltpu.VMEM((nsub, VT_ROWS, GROUP * tq), F32),
                        pltpu.VMEM((2, nsub, tk, GROUP * tq), F32)],
        compiler_params=pltpu.CompilerParams(
            dimension_semantics=("parallel", "parallel", "arbitrary")),
        name="flash_attention",
    )(q, k, v, kc, vc, sz)


def _rope_tables(l):
    rows = l // GRID_W
    axis_dim = HEAD_DIM // 2
    row = np.repeat(np.arange(rows, dtype=np.float64), GRID_W)
    col = np.tile(np.arange(GRID_W, dtype=np.float64), rows)
    inv = 1.0 / (ROPE_THETA ** (np.arange(0, axis_dim, 2, dtype=np.float64) / axis_dim))
    ra, ca = row[:, None] * inv[None, :], col[:, None] * inv[None, :]
    cos = np.concatenate([np.cos(ra), np.cos(ra), np.cos(ca), np.cos(ca)], axis=-1)
    sin = np.concatenate([-np.sin(ra), np.sin(ra), -np.sin(ca), np.sin(ca)], axis=-1)
    return jnp.asarray(cos, F32), jnp.asarray(sin, F32)


def _pick(n, target):
    t = min(n, target)
    while n % t:
        t //= 2
    return t


class _Tiles(NamedTuple):
    proj_rows: int
    ctx_rows: int
    hy_cols: int
    out_rows: int
    ctx_out_rows: int
    filter_rows: int
    ctx_filter_rows: int
    dft_cols: int
    dft_group: int
    dft_slabs: int
    at_rows: int
    q_rows: int
    q_subtiles: int
    key_rows: int


def _tiles(l, n_ctx, e):
    return _Tiles(
        proj_rows=_pick(l, 1024), ctx_rows=_pick(n_ctx, 1024), hy_cols=_pick(e, 256),
        out_rows=_pick(l, 512), ctx_out_rows=_pick(n_ctx, 512),
        filter_rows=_pick(l, 256), ctx_filter_rows=_pick(n_ctx, 256),
        dft_cols=_pick(e, 512), dft_group=16, dft_slabs=4,
        at_rows=_pick(l, 512), q_rows=_pick(l // 2, 256), q_subtiles=2, key_rows=_pick(l // 2, 1024))


def kernel(x, c, ctx, c_ctx, norm_g, ada_w, ada_b,
           hy_w_in, hy_conv_w, hy_conv_b, hy_fw1, hy_fb1, hy_fw2, hy_fb2, hy_fw3, hy_fb3,
           hy_fw4, hy_freq, hy_d, hy_w_out,
           at_w_in, at_q_g, at_k_g, at_w_out):
    b, l, d = x.shape
    n_ctx = ctx.shape[1]
    e = hy_d.shape[-1]

    cv = jnp.concatenate([c, c_ctx[None], jnp.zeros((8 - b - 1, d), F32)], axis=0)
    mod = _modulation(cv, ada_w, ada_b)

    def lat(v):
        return v[:b, None, :]

    def cx(v):
        return jnp.broadcast_to(v[b][None, None, :], (b, 1, d))

    sh0, sc0, gt0 = jnp.split(mod[0], 3, axis=-1)
    w_in = hy_w_in[0].astype(BF16)
    w_out = hy_w_out[0].astype(BF16)
    fargs = (hy_fw1[0], hy_fb1[0], hy_fw2[0], hy_fb2[0], hy_fw3[0], hy_fb3[0], hy_fw4[0], hy_freq[0])
    t = _tiles(l, n_ctx, e)

    u_lat, g_lat = _hy_proj(x, norm_g[0], lat(sc0), lat(sh0), w_in, hy_conv_w[0], hy_conv_b[0],
                            tm=t.proj_rows, tn=t.hy_cols)
    u_ctx, g_ctx = _hy_proj(ctx, norm_g[0], cx(sc0), cx(sh0), w_in, hy_conv_w[0], hy_conv_b[0],
                            tm=t.ctx_rows, tn=t.hy_cols)
    k_lat, s_lat = _hyena_filter(l, *fargs, tr=t.filter_rows)
    k_ctx, s_ctx = _hyena_filter(n_ctx, *fargs, tr=t.ctx_filter_rows)
    y_lat = _long_conv_latent(u_lat, k_lat, s_lat, nb=t.dft_group, cb=t.dft_cols, kb=t.dft_slabs)
    y_ctx = _long_conv_ctx(u_ctx, k_ctx, s_ctx, cb=t.dft_cols)
    x1 = _hy_out(y_lat, u_lat, g_lat, hy_d[0], w_out, x, lat(gt0), tm=t.out_rows)
    ctx1 = _hy_out(y_ctx, u_ctx, g_ctx, hy_d[0], w_out, ctx, cx(gt0), tm=t.ctx_out_rows)

    sh1, sc1, gt1 = jnp.split(mod[1], 3, axis=-1)
    aw_in = at_w_in[0].astype(BF16)
    aw_out = at_w_out[0].astype(BF16)
    cos, sin = _rope_tables(l)
    q, k, v, sz = _at_proj(x1, norm_g[1], lat(sc1), lat(sh1), aw_in, at_q_g[0], at_k_g[0], cos, sin,
                           tm=t.at_rows)
    kc, vc = _at_proj_kv(ctx1, norm_g[1], cx(sc1), cx(sh1), aw_in, at_k_g[0], qd=q.shape[-1])
    om = _attention(q, k, v, kc, vc, sz, tq=t.q_rows, tk=t.key_rows, nsub=t.q_subtiles)
    return _at_out(om, aw_out, x1, lat(gt1), tm=t.out_rows)
```

```python
import functools
import math
from typing import NamedTuple

import numpy as np
import jax
import jax.numpy as jnp
from jax import lax
from jax.experimental import pallas as pl
from jax.experimental.pallas import tpu as pltpu

F32 = jnp.float32
BF16 = jnp.bfloat16
HIGHEST = lax.Precision.HIGHEST

EPS = 1e-6
HEAD_DIM = 128
GROUP = 4
GROUP_W = GROUP * HEAD_DIM
GRID_W = 64
ROPE_THETA = 10000.0
FILTER_BANDS = 16
DECAY_TARGET = 1e-2
MAX_DECAY = math.log(DECAY_TARGET) / 0.3
MIN_DECAY = math.log(DECAY_TARGET) / 1.5
HALO = 16
LANES = 128
VT_ROWS = HEAD_DIM + 16
LOG2E = math.log2(math.e)
KV_UNROLL = 4
STAB_KEYS = 128
STAB_GAP = 90.0


def _sigmoid(x):
    return 1.0 / (1.0 + jnp.exp(-x))


def _mm(a, b):
    return jnp.dot(a, b, preferred_element_type=F32)


def _mm_f32(a, b):
    return jnp.dot(a, b, preferred_element_type=F32, precision=HIGHEST)


def _mod_kernel(c_ref, w_ref, b_ref, o_ref):
    cv = c_ref[...]
    o_ref[...] = _mm_f32(cv * _sigmoid(cv), w_ref[...]) + b_ref[...]


def _modulation(cv, ada_w, ada_b):
    depth, d, d3 = ada_w.shape
    rows = cv.shape[0]
    tn = d
    return pl.pallas_call(
        _mod_kernel,
        out_shape=jax.ShapeDtypeStruct((depth, rows, d3), F32),
        grid=(depth, d3 // tn),
        in_specs=[
            pl.BlockSpec((rows, d), lambda l, j: (0, 0)),
            pl.BlockSpec((None, d, tn), lambda l, j: (l, 0, j)),
            pl.BlockSpec((None, 1, tn), lambda l, j: (l, 0, j)),
        ],
        out_specs=pl.BlockSpec((None, rows, tn), lambda l, j: (l, 0, j)),
        name="modulation",
    )(cv, ada_w, ada_b.reshape(depth, 1, d3))


def _norm_mod(x, g, sc, sh):
    ms = jnp.mean(x * x, axis=-1, keepdims=True)
    return x * lax.rsqrt(ms + EPS) * (g * (1.0 + sc)) + sh


def _hy_proj_kernel(xm_ref, xp_ref, xn_ref, g_ref, sc_ref, sh_ref, w_ref, cw_ref, cb_ref,
                    u_ref, gate_ref, hbuf, *, tm, tn, n_i):
    i = pl.program_id(1)
    e = u_ref.shape[-1]
    g, sc, sh = g_ref[...], sc_ref[...], sh_ref[...]
    hbuf[HALO:HALO + tm, :] = _norm_mod(xm_ref[...], g, sc, sh).astype(BF16)
    hp = _norm_mod(xp_ref[...], g, sc, sh)
    hbuf[0:HALO, :] = jnp.where(i > 0, hp, 0.0).astype(BF16)
    hn = _norm_mod(xn_ref[...], g, sc, sh)
    hbuf[HALO + tm:, :] = jnp.where(i < n_i - 1, hn, 0.0).astype(BF16)

    def conv(lo):
        p = _mm(hbuf[...], w_ref[:, lo:lo + tn])
        cw = cw_ref[:, lo:lo + tn]
        rows = p.shape[0]
        prev = pltpu.roll(p, 1, 0)[HALO:HALO + tm]
        nxt = pltpu.roll(p, rows - 1, 0)[HALO:HALO + tm]
        return prev * cw[0:1] + p[HALO:HALO + tm] * cw[1:2] + nxt * cw[2:3] + cb_ref[:, lo:lo + tn]

    for j in range(e // tn):
        lo = j * tn
        u_ref[:, lo:lo + tn] = (conv(e + lo) * conv(2 * e + lo)).astype(BF16)
        z = _mm(hbuf[HALO:HALO + tm, :], w_ref[:, 3 * e + lo:3 * e + lo + tn])
        gate_ref[:, lo:lo + tn] = (conv(lo) * z * _sigmoid(z)).astype(BF16)


def _hy_proj(x, g, sc, sh, w_bf, conv_w, conv_b, *, tm, tn):
    b, l, d = x.shape
    e = w_bf.shape[1] // 4
    n_i = l // tm
    hb = tm // HALO
    last_hb = l // HALO - 1
    const = lambda shape: pl.BlockSpec(shape, lambda bb, i: (0, 0), pipeline_mode=pl.Buffered(1))
    vec = pl.BlockSpec((None, 1, d), lambda bb, i: (bb, 0, 0))
    out = pl.BlockSpec((None, tm, e), lambda bb, i: (bb, i, 0))
    return pl.pallas_call(
        functools.partial(_hy_proj_kernel, tm=tm, tn=tn, n_i=n_i),
        out_shape=(jax.ShapeDtypeStruct((b, l, e), BF16), jax.ShapeDtypeStruct((b, l, e), BF16)),
        grid=(b, n_i),
        in_specs=[
            pl.BlockSpec((None, tm, d), lambda bb, i: (bb, i, 0)),
            pl.BlockSpec((None, HALO, d), lambda bb, i: (bb, jnp.maximum(i * hb - 1, 0), 0)),
            pl.BlockSpec((None, HALO, d), lambda bb, i: (bb, jnp.minimum((i + 1) * hb, last_hb), 0)),
            const((1, d)), vec, vec,
            const((d, 4 * e)), const((3, 3 * e)), const((1, 3 * e)),
        ],
        out_specs=(out, out),
        scratch_shapes=[pltpu.VMEM((tm + 2 * HALO, d), BF16)],
        compiler_params=pltpu.CompilerParams(dimension_semantics=("parallel", "parallel")),
        name="hyena_in_proj",
    )(x, x, x, g.reshape(1, d), sc, sh, w_bf, conv_w, conv_b.reshape(1, -1))


def _split_bf16(a):
    hi = a.astype(BF16)
    return hi, (a - hi.astype(F32)).astype(BF16)


def _mm_split(w, a):
    w_hi, w_lo = _split_bf16(w)
    a_hi, a_lo = _split_bf16(a)
    return _mm(w_hi, a_hi) + _mm(w_lo, a_hi) + _mm(w_hi, a_lo)


def _filter_kernel(emb_ref, w1_ref, w2_ref, w3_ref, col_ref, w4_ref, dl_ref, k_ref, s_ref, *, l, tr):
    t_idx = pl.program_id(0)
    r_row = t_idx * tr + lax.broadcasted_iota(jnp.int32, (1, tr), 1)
    p_row = jnp.where(r_row < l, r_row, 2 * l - r_row).astype(F32)
    emb = emb_ref[...]
    z = (emb[:, 2:3] * (p_row / (l - 1.0))
         + emb[:, 3:4] * jnp.cos(emb[:, 0:1] * ((2.0 * math.pi) * p_row / l) + emb[:, 1:2]))
    col = col_ref[...]
    fq = col[:, 3:4]
    h = jnp.sin(fq * (_mm_split(w1_ref[...], z) + col[:, 0:1]))
    h = jnp.sin(fq * (_mm_split(w2_ref[...], h) + col[:, 1:2]))
    h = jnp.sin(fq * (_mm_split(w3_ref[...], h) + col[:, 2:3]))
    h_hi, h_lo = _split_bf16(h)
    w_hi, w_lo = _split_bf16(w4_ref[...])
    k = lax.dot_general(jnp.concatenate([h_hi, h_lo, h_hi], axis=0),
                        jnp.concatenate([w_hi, w_hi, w_lo], axis=0),
                        (((0,), (0,)), ((), ())), preferred_element_type=F32)
    r_col = t_idx * tr + lax.broadcasted_iota(jnp.int32, (tr, 1), 0)
    t_col = jnp.where(r_col < l, r_col, 2 * l - r_col).astype(F32) / (l - 1.0)
    k = jnp.where(r_col == l, 0.0, k * jnp.exp(-t_col * dl_ref[...]))
    k_ref[...] = k.astype(BF16)

    @pl.when(t_idx == 0)
    def _():
        s_ref[...] = jnp.zeros_like(s_ref)

    s_ref[...] += jnp.sum(jnp.abs(k), axis=0, keepdims=True)


EMB_ROWS = 48


def _filter_embedding_table():
    fr = np.linspace(1e-4, FILTER_BANDS - 1, FILTER_BANDS)
    emb = np.zeros((EMB_ROWS, 4), np.float32)
    emb[1:1 + FILTER_BANDS, 0] = fr
    emb[1 + FILTER_BANDS:1 + 2 * FILTER_BANDS, 0] = fr
    emb[1 + FILTER_BANDS:1 + 2 * FILTER_BANDS, 1] = math.pi / 2
    emb[0, 2] = 1.0
    emb[1:1 + 2 * FILTER_BANDS, 3] = 1.0
    return emb


def _hyena_filter(l, fw1, fb1, fw2, fb2, fw3, fb3, fw4, freq, *, tr):
    width = fw1.shape[1]
    e = fw4.shape[1] // 2
    w1t = jnp.zeros((width, EMB_ROWS), F32).at[:, :fw1.shape[0]].set(fw1.T)
    cols = jnp.stack([fb1, fb2, fb3, freq], axis=1)
    deltas = jnp.abs(jnp.linspace(MIN_DECAY, MAX_DECAY, e, dtype=F32)).reshape(1, e)
    half = l // tr
    small = lambda shape: pl.BlockSpec(shape, lambda t: (0, 0))
    return pl.pallas_call(
        functools.partial(_filter_kernel, l=l, tr=tr),
        out_shape=(jax.ShapeDtypeStruct((2 * l, e), BF16), jax.ShapeDtypeStruct((1, e), F32)),
        grid=(2 * half,),
        in_specs=[
            small((EMB_ROWS, 4)), small((width, EMB_ROWS)), small((width, width)), small((width, width)),
            small((width, 4)),
            pl.BlockSpec((width, e), lambda t: (0, t // half)),
            small((1, e)),
        ],
        out_specs=(pl.BlockSpec((tr, e), lambda t: (t, 0)), pl.BlockSpec((1, e), lambda t: (0, 0))),
        compiler_params=pltpu.CompilerParams(dimension_semantics=("arbitrary",)),
        name="hyena_filter",
    )(jnp.asarray(_filter_embedding_table()), w1t, fw2.T, fw3.T, cols, fw4, deltas)


@functools.lru_cache(maxsize=None)
def _four_step_consts(n1, n2):
    n = n1 * n2
    hh = n1 // 2
    k1 = np.arange(n1)
    c2 = np.arange(n2)

    def angles(sign, n1_count):
        nn = n2 * np.arange(n1_count)[None, None, :] + c2[:, None, None]
        return sign * 2.0 * np.pi * ((k1[None, :, None] * nn) % n) / n

    a = angles(-1.0, hh)
    c, s = np.cos(a), np.sin(a)
    g = np.zeros((n2, n1, 2, 2, hh))
    g[:, :, 0, 0], g[:, :, 0, 1], g[:, :, 1, 0], g[:, :, 1, 1] = c, -s, s, c
    g = g.reshape(n2, 2 * n1, n1)

    a = angles(-1.0, n1)
    gk = np.stack([np.cos(a), np.sin(a)], axis=2).reshape(n2, 2 * n1, n1)

    a = np.transpose(angles(1.0, hh), (0, 2, 1))
    c, s = np.cos(a) / n, np.sin(a) / n
    h = np.zeros((n2, 2, hh, n1, 2))
    h[:, 0, :, :, 0], h[:, 0, :, :, 1], h[:, 1, :, :, 0], h[:, 1, :, :, 1] = c, -s, s, c
    h = h.reshape(n2, n1, 2 * n1)

    phi = -2.0 * np.pi * ((c2[:, None] * c2[None, :]) % n2) / n2
    c, s = np.cos(phi), np.sin(phi)
    fb = np.block([[c, -s], [s, c]])
    fbi = np.block([[c, s], [-s, c]])
    f = lambda x: np.asarray(x, np.float32)
    return f(g), f(gk), f(h), f(fb), f(fbi)


@functools.lru_cache(maxsize=None)
def _direct_dft_consts(nc):
    hh = nc // 2
    k = np.arange(nc)
    a = -2.0 * np.pi * ((k[:, None] * np.arange(hh)[None, :]) % nc) / nc
    c, s = np.cos(a), np.sin(a)
    fc = np.block([[c, -s], [s, c]])
    a = -2.0 * np.pi * ((k[:, None] * k[None, :]) % nc) / nc
    fck = np.concatenate([np.cos(a), np.sin(a)], axis=0)
    a = 2.0 * np.pi * ((np.arange(hh)[:, None] * k[None, :]) % nc) / nc
    c, s = np.cos(a) / nc, np.sin(a) / nc
    fci = np.block([[c, -s], [s, c]])
    f = lambda x: np.asarray(x, np.float32)
    return f(fc), f(fck), f(fci)


def _colmm_kernel(g_ref, x_ref, o_ref, xs_ref, rs_ref, *, nb):
    xs_ref[...] = pltpu.einshape("abc->bac", x_ref[...])
    for q in range(nb):
        rs_ref[q] = _mm(g_ref[q].astype(BF16), xs_ref[q]).astype(rs_ref.dtype)
    o_ref[...] = pltpu.einshape("abc->bac", rs_ref[...])


def _colmm(gm, x3, *, nb, cb, name):
    n2, r_out, r_in = gm.shape
    e = x3.shape[-1]
    return pl.pallas_call(
        functools.partial(_colmm_kernel, nb=nb),
        out_shape=jax.ShapeDtypeStruct((r_out, n2, e), BF16),
        grid=(n2 // nb, e // cb),
        in_specs=[
            pl.BlockSpec((nb, r_out, r_in), lambda s, j: (s, 0, 0)),
            pl.BlockSpec((r_in, nb, cb), lambda s, j: (0, s, j)),
        ],
        out_specs=pl.BlockSpec((r_out, nb, cb), lambda s, j: (0, s, j)),
        scratch_shapes=[pltpu.VMEM((nb, r_in, cb), BF16), pltpu.VMEM((nb, r_out, cb), BF16)],
        compiler_params=pltpu.CompilerParams(dimension_semantics=("parallel", "parallel")),
        name=name,
    )(gm, x3)


def _cmul(x, k, half):
    xr, xi = x[:half], x[half:]
    kr, ki = k[:half], k[half:]
    return jnp.concatenate([xr * kr - xi * ki, xr * ki + xi * kr], axis=0)


def _spectral_mul_kernel(fb_ref, fbi_ref, a_ref, ak_ref, s_ref, z_ref, *, kb):
    fb, fbi = fb_ref[...].astype(BF16), fbi_ref[...].astype(BF16)
    half = fb.shape[0] // 2
    inv = 1.0 / s_ref[...]
    xs = [_mm(fb, a_ref[q]) for q in range(kb)]
    ks = [_mm(fb, ak_ref[q]) * inv for q in range(kb)]
    ys = [_cmul(xs[q], ks[q], half).astype(BF16) for q in range(kb)]
    for q in range(kb):
        z_ref[q] = _mm(fbi, ys[q]).astype(BF16)


def _spectral_mul(fb, fbi, a3, ak3, s, *, kb, cb):
    n1, r, e = a3.shape
    slab = pl.BlockSpec((kb, r, cb), lambda i, j: (i, 0, j))
    mat = pl.BlockSpec((r, r), lambda i, j: (0, 0))
    return pl.pallas_call(
        functools.partial(_spectral_mul_kernel, kb=kb),
        out_shape=jax.ShapeDtypeStruct((n1, r, e), BF16),
        grid=(n1 // kb, e // cb),
        in_specs=[mat, mat, slab, slab, pl.BlockSpec((1, cb), lambda i, j: (0, j))],
        out_specs=slab,
        compiler_params=pltpu.CompilerParams(dimension_semantics=("parallel", "parallel")),
        name="hyena_spectral_mul",
    )(fb, fbi, a3, ak3, s)


def _ctx_conv_kernel(fc_ref, fck_ref, fci_ref, u_ref, k_ref, s_ref, y_ref):
    x = _mm(fc_ref[...].astype(BF16), u_ref[...])
    ks = _mm(fck_ref[...].astype(BF16), k_ref[...]) * (1.0 / s_ref[...])
    y = _cmul(x, ks, x.shape[0] // 2).astype(BF16)
    y_ref[...] = _mm(fci_ref[...].astype(BF16), y).astype(BF16)


def _ctx_conv(fc, fck, fci, u2, k2, s, *, cb):
    nc, e = u2.shape
    full = lambda a: pl.BlockSpec(a.shape, lambda j: (0, 0))
    col = pl.BlockSpec((nc, cb), lambda j: (0, j))
    return pl.pallas_call(
        _ctx_conv_kernel,
        out_shape=jax.ShapeDtypeStruct((nc, e), BF16),
        grid=(e // cb,),
        in_specs=[full(fc), full(fck), full(fci), col, col, pl.BlockSpec((1, cb), lambda j: (0, j))],
        out_specs=col,
        compiler_params=pltpu.CompilerParams(dimension_semantics=("parallel",)),
        name="hyena_ctx_conv",
    )(fc, fck, fci, u2, k2, s)


def _long_conv_latent(u, k_raw, s, *, nb, cb, kb):
    b, l, e = u.shape
    n = 2 * l
    n1 = 1 << ((n.bit_length() - 1 + 1) // 2)
    n2 = n // n1
    g, gk, h, fb, fbi = (jnp.asarray(m) for m in _four_step_consts(n1, n2))
    a = _colmm(g, u.reshape(n1, n2, e), nb=nb, cb=cb, name="hyena_dft_rows")
    ak = _colmm(gk, k_raw.reshape(n1, n2, e), nb=nb, cb=cb, name="hyena_filter_dft_rows")
    z = _spectral_mul(fb, fbi, a.reshape(n1, 2 * n2, e), ak.reshape(n1, 2 * n2, e), s, kb=kb, cb=cb)
    y = _colmm(h, z.reshape(2 * n1, n2, e), nb=nb, cb=cb, name="hyena_idft_rows")
    return y.reshape(b, l, e)


def _long_conv_ctx(u, k_raw, s, *, cb):
    b, c, e = u.shape
    fc, fck, fci = (jnp.asarray(m) for m in _direct_dft_consts(2 * c))
    return _ctx_conv(fc, fck, fci, u.reshape(b * c, e), k_raw, s, cb=cb).reshape(b, c, e)


def _hy_out_kernel(y_ref, u_ref, g_ref, d_ref, w_ref, x_ref, gt_ref, o_ref):
    m = (y_ref[...].astype(F32) + u_ref[...].astype(F32) * d_ref[...]) * g_ref[...].astype(F32)
    o_ref[...] = x_ref[...] + gt_ref[...] * _mm(m.astype(BF16), w_ref[...])


def _hy_out(y, u, g, d_skip, w_bf, x, gt, *, tm):
    b, l, e = y.shape
    d = x.shape[-1]
    act = pl.BlockSpec((None, tm, e), lambda bb, i: (bb, i, 0))
    res = pl.BlockSpec((None, tm, d), lambda bb, i: (bb, i, 0))
    return pl.pallas_call(
        _hy_out_kernel,
        out_shape=jax.ShapeDtypeStruct(x.shape, F32),
        grid=(b, l // tm),
        in_specs=[act, act, act, pl.BlockSpec((1, e), lambda bb, i: (0, 0)),
                  pl.BlockSpec((e, d), lambda bb, i: (0, 0)), res,
                  pl.BlockSpec((None, 1, d), lambda bb, i: (bb, 0, 0))],
        out_specs=res,
        compiler_params=pltpu.CompilerParams(dimension_semantics=("parallel", "parallel")),
        name="hyena_out_proj",
    )(y, u, g, d_skip.reshape(1, e), w_bf, x, gt)


def _at_out_kernel(m_ref, w_ref, x_ref, gt_ref, o_ref):
    o_ref[...] = x_ref[...] + gt_ref[...] * _mm(m_ref[...], w_ref[...])


def _at_out(m, w_bf, x, gt, *, tm):
    b, l, e = m.shape
    d = x.shape[-1]
    res = pl.BlockSpec((None, tm, d), lambda bb, i: (bb, i, 0))
    return pl.pallas_call(
        _at_out_kernel,
        out_shape=jax.ShapeDtypeStruct(x.shape, F32),
        grid=(b, l // tm),
        in_specs=[pl.BlockSpec((None, tm, e), lambda bb, i: (bb, i, 0)),
                  pl.BlockSpec((e, d), lambda bb, i: (0, 0)), res,
                  pl.BlockSpec((None, 1, d), lambda bb, i: (bb, 0, 0))],
        out_specs=res,
        compiler_params=pltpu.CompilerParams(dimension_semantics=("parallel", "parallel")),
        name="attn_out_proj",
    )(m, w_bf, x, gt)


def _head_norm_rope(p, gain, cos, sin, scale):
    lane = lax.broadcasted_iota(jnp.int32, (1, HEAD_DIM), 1)
    first = (lane % (HEAD_DIM // 2)) < (HEAD_DIM // 4)
    outs = []
    for hd in range(p.shape[1] // HEAD_DIM):
        xh = p[:, hd * HEAD_DIM:(hd + 1) * HEAD_DIM]
        ms = jnp.mean(xh * xh, axis=-1, keepdims=True)
        yh = xh * lax.rsqrt(ms + EPS) * gain
        partner = jnp.where(first, pltpu.roll(yh, HEAD_DIM - HEAD_DIM // 4, 1),
                            pltpu.roll(yh, HEAD_DIM // 4, 1))
        outs.append((yh * cos + partner * sin) * scale)
    return jnp.concatenate(outs, axis=1)


def _at_proj_kernel(x_ref, g_ref, sc_ref, sh_ref, w_ref, qg_ref, kg_ref, cos_ref, sin_ref,
                    q_ref, k_ref, v_ref, sz_ref, hbuf):
    qd = q_ref.shape[-1]
    hbuf[...] = _norm_mod(x_ref[...], g_ref[...], sc_ref[...], sh_ref[...]).astype(BF16)

    def proj(lo):
        return _mm(hbuf[...], w_ref[:, lo:lo + GROUP_W])

    for lo in range(0, qd, GROUP_W):
        q_ref[:, lo:lo + GROUP_W] = _head_norm_rope(proj(lo), qg_ref[...], cos_ref[...], sin_ref[...],
                                                    HEAD_DIM ** -0.5 * LOG2E).astype(BF16)
    k_ref[...] = _head_norm_rope(proj(qd), kg_ref[...], cos_ref[...], sin_ref[...], 1.0).astype(BF16)
    v_ref[...] = proj(qd + GROUP_W).astype(BF16)
    for lo in range(0, qd, GROUP_W):
        p = proj(qd + 2 * GROUP_W + lo)
        sz_ref[:, lo:lo + GROUP_W] = (p * _sigmoid(p)).astype(BF16)


def _at_proj(x, g, sc, sh, w_bf, q_g, k_g, cos, sin, *, tm):
    b, l, d = x.shape
    kvd = GROUP_W
    qd = (w_bf.shape[1] - 2 * kvd) // 2
    const = lambda shape: pl.BlockSpec(shape, lambda bb, i: (0, 0), pipeline_mode=pl.Buffered(1))
    vec = pl.BlockSpec((None, 1, d), lambda bb, i: (bb, 0, 0))
    tab = pl.BlockSpec((tm, HEAD_DIM), lambda bb, i: (i, 0))
    wide = pl.BlockSpec((None, tm, qd), lambda bb, i: (bb, i, 0))
    narrow = pl.BlockSpec((None, tm, kvd), lambda bb, i: (bb, i, 0))
    return pl.pallas_call(
        _at_proj_kernel,
        out_shape=(jax.ShapeDtypeStruct((b, l, qd), BF16), jax.ShapeDtypeStruct((b, l, kvd), BF16),
                   jax.ShapeDtypeStruct((b, l, kvd), BF16), jax.ShapeDtypeStruct((b, l, qd), BF16)),
        grid=(b, l // tm),
        in_specs=[
            pl.BlockSpec((None, tm, d), lambda bb, i: (bb, i, 0)),
            const((1, d)), vec, vec, const(w_bf.shape),
            const((1, HEAD_DIM)), const((1, HEAD_DIM)), tab, tab,
        ],
        out_specs=(wide, narrow, narrow, wide),
        scratch_shapes=[pltpu.VMEM((tm, d), BF16)],
        compiler_params=pltpu.CompilerParams(dimension_semantics=("parallel", "parallel")),
        name="attn_in_proj",
    )(x, g.reshape(1, d), sc, sh, w_bf, q_g.reshape(1, -1), k_g.reshape(1, -1), cos, sin)


def _at_proj_kv_kernel(x_ref, g_ref, sc_ref, sh_ref, wk_ref, wv_ref, kg_ref, k_ref, v_ref):
    h = _norm_mod(x_ref[...], g_ref[...], sc_ref[...], sh_ref[...]).astype(BF16)
    p = _mm(h, wk_ref[...])
    gain = kg_ref[...]
    outs = []
    for hd in range(p.shape[1] // HEAD_DIM):
        xh = p[:, hd * HEAD_DIM:(hd + 1) * HEAD_DIM]
        ms = jnp.mean(xh * xh, axis=-1, keepdims=True)
        outs.append(xh * lax.rsqrt(ms + EPS) * gain)
    k_ref[...] = jnp.concatenate(outs, axis=1).astype(BF16)
    v_ref[...] = _mm(h, wv_ref[...]).astype(BF16)


def _at_proj_kv(x, g, sc, sh, w_bf, k_g, *, qd):
    b, c, d = x.shape
    kb = qd // GROUP_W
    vec = pl.BlockSpec((None, 1, d), lambda bb: (bb, 0, 0))
    out = pl.BlockSpec((None, c, GROUP_W), lambda bb: (bb, 0, 0))
    return pl.pallas_call(
        _at_proj_kv_kernel,
        out_shape=(jax.ShapeDtypeStruct((b, c, GROUP_W), BF16),) * 2,
        grid=(b,),
        in_specs=[
            pl.BlockSpec((None, c, d), lambda bb: (bb, 0, 0)),
            pl.BlockSpec((1, d), lambda bb: (0, 0)), vec, vec,
            pl.BlockSpec((d, GROUP_W), lambda bb: (0, kb)),
            pl.BlockSpec((d, GROUP_W), lambda bb: (0, kb + 1)),
            pl.BlockSpec((1, HEAD_DIM), lambda bb: (0, 0)),
        ],
        out_specs=(out, out),
        compiler_params=pltpu.CompilerParams(dimension_semantics=("parallel",)),
        name="attn_ctx_kv_proj",
    )(x, g.reshape(1, d), sc, sh, w_bf, w_bf, k_g.reshape(1, -1))


def _attn_kernel(q_ref, k_ref, v_ref, kc_ref, vc_ref, sz_ref, o_ref, vt_ref, ka_ref, kmax_ref, acc_ref,
                 sbuf, *, tq, tk, n_k, c, nsub):
    width = GROUP * tq

    @pl.when(pl.program_id(2) == 0)
    def _():
        ones = jnp.ones((VT_ROWS - HEAD_DIM, tk), BF16)
        def aug(rows):
            lane = lax.broadcasted_iota(jnp.int32, (rows, HEAD_DIM), 1)
            return jnp.where(lane < 2, 1.0, 0.0).astype(BF16)

        kk = jnp.zeros((1, 1), F32)
        chunks = [(t, k_ref[t * tk:(t + 1) * tk, :], v_ref[t * tk:(t + 1) * tk, :]) for t in range(n_k)]
        chunks.append((n_k, kc_ref[...], vc_ref[...]))
        for t, kt, vt in chunks:
            rows = kt.shape[0]
            vt_ref[t, 0:HEAD_DIM, 0:rows] = vt.astype(F32).T.astype(BF16)
            vt_ref[t, HEAD_DIM:, :] = ones
            ka_ref[t, 0:rows, 0:HEAD_DIM] = kt
            ka_ref[t, 0:rows, HEAD_DIM:] = aug(rows)
            kf = kt.astype(F32)
            kk = jnp.maximum(kk, jnp.max(jnp.sum(kf * kf, axis=1, keepdims=True), axis=0, keepdims=True))
        kmax_ref[...] = jnp.broadcast_to(jnp.sqrt(kk), kmax_ref.shape)

    qts, stab, gap = [], [], None
    for sb in range(nsub):
        q = q_ref[sb * tq:(sb + 1) * tq, :]
        qf = [q[:, gi * HEAD_DIM:(gi + 1) * HEAD_DIM].astype(F32).T for gi in range(GROUP)]
        qts.append(jnp.concatenate([x.astype(BF16) for x in qf], axis=1))
        qnorm = jnp.sqrt(jnp.concatenate([jnp.sum(x * x, axis=0, keepdims=True) for x in qf], axis=1))
        low = jnp.max(_mm(k_ref[0:STAB_KEYS, :], qts[sb]), axis=0, keepdims=True)
        stab.append(low)
        g = jnp.max(qnorm * kmax_ref[0:1, 0:1] - low)
        gap = g if gap is None else jnp.maximum(gap, g)
    acc_ref[...] = jnp.zeros_like(acc_ref)
    n_loop = (n_k - 2) // KV_UNROLL

    def run(scores, consume, first, chunk, ctx, carry):
        def body(i, carry):
            t = KV_UNROLL * i
            for u in range(KV_UNROLL):
                scores(chunk(t + u + 1), (u + 1) % 2)
                carry = consume(u % 2, vt_ref[t + u], carry)
            return carry

        scores(first, 0)
        carry = lax.fori_loop(0, n_loop, body, carry)
        for t in range(n_loop * KV_UNROLL, n_k):
            scores(chunk(t + 1) if t + 1 < n_k else ctx, (t + 1) % 2)
            carry = consume(t % 2, vt_ref[t], carry)
        consume(n_k % 2, vt_ref[n_k, :, 0:c], carry)

    def finish(denominator):
        for sb in range(nsub):
            acc = acc_ref[sb]
            ot = acc[0:HEAD_DIM] * (1.0 / denominator(acc))
            o = jnp.concatenate([ot[:, gi * tq:(gi + 1) * tq].T for gi in range(GROUP)], axis=1)
            rows = slice(sb * tq, (sb + 1) * tq)
            o_ref[rows, :] = (o * sz_ref[rows, :].astype(F32)).astype(BF16)

    @pl.when(gap <= STAB_GAP)
    def _():
        qas = []
        for sb in range(nsub):
            bits = lax.bitcast_convert_type(stab[sb], jnp.uint32) & jnp.uint32(0xFFFF0000)
            m_hi = lax.bitcast_convert_type(bits, F32)
            m_lo = stab[sb] - m_hi
            row = lax.broadcasted_iota(jnp.int32, (16, width), 0)
            extra = jnp.where(row == 0, -m_hi, jnp.where(row == 1, -m_lo, 0.0))
            qas.append(jnp.concatenate([qts[sb], extra.astype(BF16),
                                        jnp.zeros((HEAD_DIM - 16, width), BF16)], axis=0))

        def scores(kt, slot):
            for sb in range(nsub):
                sbuf[slot, sb, 0:kt.shape[0]] = _mm(kt, qas[sb])

        def consume(slot, vt, carry):
            for sb in range(nsub):
                p = jnp.exp2(sbuf[slot, sb, 0:vt.shape[1]])
                acc_ref[sb, HEAD_DIM:HEAD_DIM + 8] += jnp.sum(p.reshape(p.shape[0] // 8, 8, width), axis=0)
                acc_ref[sb, 0:HEAD_DIM] += _mm(vt[0:HEAD_DIM], p.astype(BF16))
            return carry

        run(scores, consume, ka_ref[0], lambda t: ka_ref[t], ka_ref[n_k, 0:c, :], 0)
        finish(lambda acc: jnp.sum(acc[HEAD_DIM:HEAD_DIM + 8], axis=0, keepdims=True))

    @pl.when(jnp.logical_not(gap <= STAB_GAP))
    def _():
        def scores(kt, slot):
            for sb in range(nsub):
                sbuf[slot, sb, 0:kt.shape[0]] = _mm(kt, qts[sb])

        def consume(slot, vt, ms):
            out = []
            for sb in range(nsub):
                s = sbuf[slot, sb, 0:vt.shape[1]]
                m_new = jnp.maximum(ms[sb], jnp.max(s, axis=0, keepdims=True))
                p = jnp.exp2(s - m_new).astype(BF16)
                acc_ref[sb] = jnp.exp2(ms[sb] - m_new) * acc_ref[sb] + _mm(vt, p)
                out.append(m_new)
            return tuple(out)

        def key_chunk(t):
            start = t * tk if isinstance(t, int) else pl.multiple_of(t * tk, tk)
            return k_ref[pl.ds(start, tk), :]

        ms = tuple(jnp.full((1, width), -jnp.inf, F32) for _ in range(nsub))
        run(scores, consume, k_ref[0:tk, :], key_chunk, kc_ref[...], ms)
        finish(lambda acc: acc[HEAD_DIM:HEAD_DIM + 1])


def _attention(q, k, v, kc, vc, sz, *, tq, tk, nsub):
    b, l, qd = q.shape
    c = kc.shape[1]
    n_kv = qd // GROUP_W
    n_k = l // tk
    assert n_k % 2 == 0 and n_k >= 2 and c <= tk
    qspec = pl.BlockSpec((None, nsub * tq, GROUP_W), lambda bb, hh, i: (bb, i, hh))
    kspec = pl.BlockSpec((None, l, HEAD_DIM), lambda bb, hh, i: (bb, 0, hh))
    cspec = pl.BlockSpec((None, c, HEAD_DIM), lambda bb, hh, i: (bb, 0, hh))
    return pl.pallas_call(
        functools.partial(_attn_kernel, tq=tq, tk=tk, n_k=n_k, c=c, nsub=nsub),
        out_shape=jax.ShapeDtypeStruct((b, l, qd), BF16),
        grid=(b, n_kv, l // (nsub * tq)),
        in_specs=[qspec, kspec, kspec, cspec, cspec, qspec],
        out_specs=qspec,
        scratch_shapes=[pltpu.VMEM((n_k + 1, VT_ROWS, tk), BF16),
                        pltpu.VMEM((n_k + 1, tk, 2 * HEAD_DIM), BF16),
                        pltpu.VMEM((8, LANES), F32),
                        pltpu.VMEM((nsub, VT_ROWS, GROUP * tq), F32),
                        pltpu.VMEM((2, nsub, tk, GROUP * tq), F32)],
        compiler_params=pltpu.CompilerParams(
            dimension_semantics=("parallel", "parallel", "arbitrary")),
        name="flash_attention",
    )(q, k, v, kc, vc, sz)


def _rope_tables(l):
    rows = l // GRID_W
    axis_dim = HEAD_DIM // 2
    row = np.repeat(np.arange(rows, dtype=np.float64), GRID_W)
    col = np.tile(np.arange(GRID_W, dtype=np.float64), rows)
    inv = 1.0 / (ROPE_THETA ** (np.arange(0, axis_dim, 2, dtype=np.float64) / axis_dim))
    ra, ca = row[:, None] * inv[None, :], col[:, None] * inv[None, :]
    cos = np.concatenate([np.cos(ra), np.cos(ra), np.cos(ca), np.cos(ca)], axis=-1)
    sin = np.concatenate([-np.sin(ra), np.sin(ra), -np.sin(ca), np.sin(ca)], axis=-1)
    return jnp.asarray(cos, F32), jnp.asarray(sin, F32)


def _pick(n, target):
    t = min(n, target)
    while n % t:
        t //= 2
    return t


class _Tiles(NamedTuple):
    proj_rows: int
    ctx_rows: int
    hy_cols: int
    out_rows: int
    ctx_out_rows: int
    filter_rows: int
    ctx_filter_rows: int
    dft_cols: int
    dft_group: int
    dft_slabs: int
    at_rows: int
    q_rows: int
    q_subtiles: int
    key_rows: int


def _tiles(l, n_ctx, e):
    return _Tiles(
        proj_rows=_pick(l, 1024), ctx_rows=_pick(n_ctx, 1024), hy_cols=_pick(e, 256),
        out_rows=_pick(l, 512), ctx_out_rows=_pick(n_ctx, 512),
        filter_rows=_pick(l, 256), ctx_filter_rows=_pick(n_ctx, 256),
        dft_cols=_pick(e, 512), dft_group=16, dft_slabs=4,
        at_rows=_pick(l, 512), q_rows=_pick(l // 2, 256), q_subtiles=2, key_rows=_pick(l // 2, 512))


def kernel(x, c, ctx, c_ctx, norm_g, ada_w, ada_b,
           hy_w_in, hy_conv_w, hy_conv_b, hy_fw1, hy_fb1, hy_fw2, hy_fb2, hy_fw3, hy_fb3,
           hy_fw4, hy_freq, hy_d, hy_w_out,
           at_w_in, at_q_g, at_k_g, at_w_out):
    b, l, d = x.shape
    n_ctx = ctx.shape[1]
    e = hy_d.shape[-1]

    cv = jnp.concatenate([c, c_ctx[None], jnp.zeros((8 - b - 1, d), F32)], axis=0)
    mod = _modulation(cv, ada_w, ada_b)

    def lat(v):
        return v[:b, None, :]

    def cx(v):
        return jnp.broadcast_to(v[b][None, None, :], (b, 1, d))

    sh0, sc0, gt0 = jnp.split(mod[0], 3, axis=-1)
    w_in = hy_w_in[0].astype(BF16)
    w_out = hy_w_out[0].astype(BF16)
    fargs = (hy_fw1[0], hy_fb1[0], hy_fw2[0], hy_fb2[0], hy_fw3[0], hy_fb3[0], hy_fw4[0], hy_freq[0])
    t = _tiles(l, n_ctx, e)

    u_lat, g_lat = _hy_proj(x, norm_g[0], lat(sc0), lat(sh0), w_in, hy_conv_w[0], hy_conv_b[0],
                            tm=t.proj_rows, tn=t.hy_cols)
    u_ctx, g_ctx = _hy_proj(ctx, norm_g[0], cx(sc0), cx(sh0), w_in, hy_conv_w[0], hy_conv_b[0],
                            tm=t.ctx_rows, tn=t.hy_cols)
    k_lat, s_lat = _hyena_filter(l, *fargs, tr=t.filter_rows)
    k_ctx, s_ctx = _hyena_filter(n_ctx, *fargs, tr=t.ctx_filter_rows)
    y_lat = _long_conv_latent(u_lat, k_lat, s_lat, nb=t.dft_group, cb=t.dft_cols, kb=t.dft_slabs)
    y_ctx = _long_conv_ctx(u_ctx, k_ctx, s_ctx, cb=t.dft_cols)
    x1 = _hy_out(y_lat, u_lat, g_lat, hy_d[0], w_out, x, lat(gt0), tm=t.out_rows)
    ctx1 = _hy_out(y_ctx, u_ctx, g_ctx, hy_d[0], w_out, ctx, cx(gt0), tm=t.ctx_out_rows)

    sh1, sc1, gt1 = jnp.split(mod[1], 3, axis=-1)
    aw_in = at_w_in[0].astype(BF16)
    aw_out = at_w_out[0].astype(BF16)
    cos, sin = _rope_tables(l)
    q, k, v, sz = _at_proj(x1, norm_g[1], lat(sc1), lat(sh1), aw_in, at_q_g[0], at_k_g[0], cos, sin,
                           tm=t.at_rows)
    kc, vc = _at_proj_kv(ctx1, norm_g[1], cx(sc1), cx(sh1), aw_in, at_k_g[0], qd=q.shape[-1])
    om = _attention(q, k, v, kc, vc, sz, tq=t.q_rows, tk=t.key_rows, nsub=t.q_subtiles)
    return _at_out(om, aw_out, x1, lat(gt1), tm=t.out_rows)
```

```python
import functools
import math
from typing import NamedTuple

import numpy as np
import jax
import jax.numpy as jnp
from jax import lax
from jax.experimental import pallas as pl
from jax.experimental.pallas import tpu as pltpu

F32 = jnp.float32
BF16 = jnp.bfloat16
HIGHEST = lax.Precision.HIGHEST

EPS = 1e-6
HEAD_DIM = 128
GROUP = 4
GROUP_W = GROUP * HEAD_DIM
GRID_W = 64
ROPE_THETA = 10000.0
FILTER_BANDS = 16
DECAY_TARGET = 1e-2
MAX_DECAY = math.log(DECAY_TARGET) / 0.3
MIN_DECAY = math.log(DECAY_TARGET) / 1.5
HALO = 16
LANES = 128
VT_ROWS = HEAD_DIM + 16
LOG2E = math.log2(math.e)
KV_UNROLL = 2
STAB_KEYS = 128
STAB_GAP = 90.0


def _sigmoid(x):
    return 1.0 / (1.0 + jnp.exp(-x))


def _mm(a, b):
    return jnp.dot(a, b, preferred_element_type=F32)


def _mm_f32(a, b):
    return jnp.dot(a, b, preferred_element_type=F32, precision=HIGHEST)


def _mod_kernel(c_ref, w_ref, b_ref, o_ref):
    cv = c_ref[...]
    o_ref[...] = _mm_f32(cv * _sigmoid(cv), w_ref[...]) + b_ref[...]


def _modulation(cv, ada_w, ada_b):
    depth, d, d3 = ada_w.shape
    rows = cv.shape[0]
    tn = d
    return pl.pallas_call(
        _mod_kernel,
        out_shape=jax.ShapeDtypeStruct((depth, rows, d3), F32),
        grid=(depth, d3 // tn),
        in_specs=[
            pl.BlockSpec((rows, d), lambda l, j: (0, 0)),
            pl.BlockSpec((None, d, tn), lambda l, j: (l, 0, j)),
            pl.BlockSpec((None, 1, tn), lambda l, j: (l, 0, j)),
        ],
        out_specs=pl.BlockSpec((None, rows, tn), lambda l, j: (l, 0, j)),
        name="modulation",
    )(cv, ada_w, ada_b.reshape(depth, 1, d3))


def _norm_mod(x, g, sc, sh):
    ms = jnp.mean(x * x, axis=-1, keepdims=True)
    return x * lax.rsqrt(ms + EPS) * (g * (1.0 + sc)) + sh


def _hy_proj_kernel(xm_ref, xp_ref, xn_ref, g_ref, sc_ref, sh_ref, w_ref, cw_ref, cb_ref,
                    u_ref, gate_ref, hbuf, *, tm, tn, n_i):
    i = pl.program_id(1)
    e = u_ref.shape[-1]
    g, sc, sh = g_ref[...], sc_ref[...], sh_ref[...]
    hbuf[HALO:HALO + tm, :] = _norm_mod(xm_ref[...], g, sc, sh).astype(BF16)
    hp = _norm_mod(xp_ref[...], g, sc, sh)
    hbuf[0:HALO, :] = jnp.where(i > 0, hp, 0.0).astype(BF16)
    hn = _norm_mod(xn_ref[...], g, sc, sh)
    hbuf[HALO + tm:, :] = jnp.where(i < n_i - 1, hn, 0.0).astype(BF16)

    def conv(lo):
        p = _mm(hbuf[...], w_ref[:, lo:lo + tn])
        cw = cw_ref[:, lo:lo + tn]
        rows = p.shape[0]
        prev = pltpu.roll(p, 1, 0)[HALO:HALO + tm]
        nxt = pltpu.roll(p, rows - 1, 0)[HALO:HALO + tm]
        return prev * cw[0:1] + p[HALO:HALO + tm] * cw[1:2] + nxt * cw[2:3] + cb_ref[:, lo:lo + tn]

    for j in range(e // tn):
        lo = j * tn
        u_ref[:, lo:lo + tn] = (conv(e + lo) * conv(2 * e + lo)).astype(BF16)
        z = _mm(hbuf[HALO:HALO + tm, :], w_ref[:, 3 * e + lo:3 * e + lo + tn])
        gate_ref[:, lo:lo + tn] = (conv(lo) * z * _sigmoid(z)).astype(BF16)


def _hy_proj(x, g, sc, sh, w_bf, conv_w, conv_b, *, tm, tn):
    b, l, d = x.shape
    e = w_bf.shape[1] // 4
    n_i = l // tm
    hb = tm // HALO
    last_hb = l // HALO - 1
    const = lambda shape: pl.BlockSpec(shape, lambda bb, i: (0, 0), pipeline_mode=pl.Buffered(1))
    vec = pl.BlockSpec((None, 1, d), lambda bb, i: (bb, 0, 0))
    out = pl.BlockSpec((None, tm, e), lambda bb, i: (bb, i, 0))
    return pl.pallas_call(
        functools.partial(_hy_proj_kernel, tm=tm, tn=tn, n_i=n_i),
        out_shape=(jax.ShapeDtypeStruct((b, l, e), BF16), jax.ShapeDtypeStruct((b, l, e), BF16)),
        grid=(b, n_i),
        in_specs=[
            pl.BlockSpec((None, tm, d), lambda bb, i: (bb, i, 0)),
            pl.BlockSpec((None, HALO, d), lambda bb, i: (bb, jnp.maximum(i * hb - 1, 0), 0)),
            pl.BlockSpec((None, HALO, d), lambda bb, i: (bb, jnp.minimum((i + 1) * hb, last_hb), 0)),
            const((1, d)), vec, vec,
            const((d, 4 * e)), const((3, 3 * e)), const((1, 3 * e)),
        ],
        out_specs=(out, out),
        scratch_shapes=[pltpu.VMEM((tm + 2 * HALO, d), BF16)],
        compiler_params=pltpu.CompilerParams(dimension_semantics=("parallel", "parallel")),
        name="hyena_in_proj",
    )(x, x, x, g.reshape(1, d), sc, sh, w_bf, conv_w, conv_b.reshape(1, -1))


def _split_bf16(a):
    hi = a.astype(BF16)
    return hi, (a - hi.astype(F32)).astype(BF16)


def _mm_split(w, a):
    w_hi, w_lo = _split_bf16(w)
    a_hi, a_lo = _split_bf16(a)
    return _mm(w_hi, a_hi) + _mm(w_lo, a_hi) + _mm(w_hi, a_lo)


def _filter_kernel(emb_ref, w1_ref, w2_ref, w3_ref, col_ref, w4_ref, dl_ref, k_ref, s_ref, *, l, tr):
    t_idx = pl.program_id(0)
    r_row = t_idx * tr + lax.broadcasted_iota(jnp.int32, (1, tr), 1)
    p_row = jnp.where(r_row < l, r_row, 2 * l - r_row).astype(F32)
    emb = emb_ref[...]
    z = (emb[:, 2:3] * (p_row / (l - 1.0))
         + emb[:, 3:4] * jnp.cos(emb[:, 0:1] * ((2.0 * math.pi) * p_row / l) + emb[:, 1:2]))
    col = col_ref[...]
    fq = col[:, 3:4]
    h = jnp.sin(fq * (_mm_split(w1_ref[...], z) + col[:, 0:1]))
    h = jnp.sin(fq * (_mm_split(w2_ref[...], h) + col[:, 1:2]))
    h = jnp.sin(fq * (_mm_split(w3_ref[...], h) + col[:, 2:3]))
    h_hi, h_lo = _split_bf16(h)
    w_hi, w_lo = _split_bf16(w4_ref[...])
    k = lax.dot_general(jnp.concatenate([h_hi, h_lo, h_hi], axis=0),
                        jnp.concatenate([w_hi, w_hi, w_lo], axis=0),
                        (((0,), (0,)), ((), ())), preferred_element_type=F32)
    r_col = t_idx * tr + lax.broadcasted_iota(jnp.int32, (tr, 1), 0)
    t_col = jnp.where(r_col < l, r_col, 2 * l - r_col).astype(F32) / (l - 1.0)
    k = jnp.where(r_col == l, 0.0, k * jnp.exp(-t_col * dl_ref[...]))
    k_ref[...] = k.astype(BF16)

    @pl.when(t_idx == 0)
    def _():
        s_ref[...] = jnp.zeros_like(s_ref)

    s_ref[...] += jnp.sum(jnp.abs(k), axis=0, keepdims=True)


EMB_ROWS = 48


def _filter_embedding_table():
    fr = np.linspace(1e-4, FILTER_BANDS - 1, FILTER_BANDS)
    emb = np.zeros((EMB_ROWS, 4), np.float32)
    emb[1:1 + FILTER_BANDS, 0] = fr
    emb[1 + FILTER_BANDS:1 + 2 * FILTER_BANDS, 0] = fr
    emb[1 + FILTER_BANDS:1 + 2 * FILTER_BANDS, 1] = math.pi / 2
    emb[0, 2] = 1.0
    emb[1:1 + 2 * FILTER_BANDS, 3] = 1.0
    return emb


def _hyena_filter(l, fw1, fb1, fw2, fb2, fw3, fb3, fw4, freq, *, tr):
    width = fw1.shape[1]
    e = fw4.shape[1] // 2
    w1t = jnp.zeros((width, EMB_ROWS), F32).at[:, :fw1.shape[0]].set(fw1.T)
    cols = jnp.stack([fb1, fb2, fb3, freq], axis=1)
    deltas = jnp.abs(jnp.linspace(MIN_DECAY, MAX_DECAY, e, dtype=F32)).reshape(1, e)
    half = l // tr
    small = lambda shape: pl.BlockSpec(shape, lambda t: (0, 0))
    return pl.pallas_call(
        functools.partial(_filter_kernel, l=l, tr=tr),
        out_shape=(jax.ShapeDtypeStruct((2 * l, e), BF16), jax.ShapeDtypeStruct((1, e), F32)),
        grid=(2 * half,),
        in_specs=[
            small((EMB_ROWS, 4)), small((width, EMB_ROWS)), small((width, width)), small((width, width)),
            small((width, 4)),
            pl.BlockSpec((width, e), lambda t: (0, t // half)),
            small((1, e)),
        ],
        out_specs=(pl.BlockSpec((tr, e), lambda t: (t, 0)), pl.BlockSpec((1, e), lambda t: (0, 0))),
        compiler_params=pltpu.CompilerParams(dimension_semantics=("arbitrary",)),
        name="hyena_filter",
    )(jnp.asarray(_filter_embedding_table()), w1t, fw2.T, fw3.T, cols, fw4, deltas)


@functools.lru_cache(maxsize=None)
def _four_step_consts(n1, n2):
    n = n1 * n2
    hh = n1 // 2
    k1 = np.arange(n1)
    c2 = np.arange(n2)

    def angles(sign, n1_count):
        nn = n2 * np.arange(n1_count)[None, None, :] + c2[:, None, None]
        return sign * 2.0 * np.pi * ((k1[None, :, None] * nn) % n) / n

    a = angles(-1.0, hh)
    c, s = np.cos(a), np.sin(a)
    g = np.zeros((n2, n1, 2, 2, hh))
    g[:, :, 0, 0], g[:, :, 0, 1], g[:, :, 1, 0], g[:, :, 1, 1] = c, -s, s, c
    g = g.reshape(n2, 2 * n1, n1)

    a = angles(-1.0, n1)
    gk = np.stack([np.cos(a), np.sin(a)], axis=2).reshape(n2, 2 * n1, n1)

    a = np.transpose(angles(1.0, hh), (0, 2, 1))
    c, s = np.cos(a) / n, np.sin(a) / n
    h = np.zeros((n2, 2, hh, n1, 2))
    h[:, 0, :, :, 0], h[:, 0, :, :, 1], h[:, 1, :, :, 0], h[:, 1, :, :, 1] = c, -s, s, c
    h = h.reshape(n2, n1, 2 * n1)

    phi = -2.0 * np.pi * ((c2[:, None] * c2[None, :]) % n2) / n2
    c, s = np.cos(phi), np.sin(phi)
    fb = np.block([[c, -s], [s, c]])
    fbi = np.block([[c, s], [-s, c]])
    f = lambda x: np.asarray(x, np.float32)
    return f(g), f(gk), f(h), f(fb), f(fbi)


@functools.lru_cache(maxsize=None)
def _direct_dft_consts(nc):
    hh = nc // 2
    k = np.arange(nc)
    a = -2.0 * np.pi * ((k[:, None] * np.arange(hh)[None, :]) % nc) / nc
    c, s = np.cos(a), np.sin(a)
    fc = np.block([[c, -s], [s, c]])
    a = -2.0 * np.pi * ((k[:, None] * k[None, :]) % nc) / nc
    fck = np.concatenate([np.cos(a), np.sin(a)], axis=0)
    a = 2.0 * np.pi * ((np.arange(hh)[:, None] * k[None, :]) % nc) / nc
    c, s = np.cos(a) / nc, np.sin(a) / nc
    fci = np.block([[c, -s], [s, c]])
    f = lambda x: np.asarray(x, np.float32)
    return f(fc), f(fck), f(fci)


def _colmm_kernel(g_ref, x_ref, o_ref, xs_ref, rs_ref, *, nb):
    xs_ref[...] = pltpu.einshape("abc->bac", x_ref[...])
    for q in range(nb):
        rs_ref[q] = _mm(g_ref[q].astype(BF16), xs_ref[q]).astype(rs_ref.dtype)
    o_ref[...] = pltpu.einshape("abc->bac", rs_ref[...])


def _colmm(gm, x3, *, nb, cb, name):
    n2, r_out, r_in = gm.shape
    e = x3.shape[-1]
    return pl.pallas_call(
        functools.partial(_colmm_kernel, nb=nb),
        out_shape=jax.ShapeDtypeStruct((r_out, n2, e), BF16),
        grid=(n2 // nb, e // cb),
        in_specs=[
            pl.BlockSpec((nb, r_out, r_in), lambda s, j: (s, 0, 0)),
            pl.BlockSpec((r_in, nb, cb), lambda s, j: (0, s, j)),
        ],
        out_specs=pl.BlockSpec((r_out, nb, cb), lambda s, j: (0, s, j)),
        scratch_shapes=[pltpu.VMEM((nb, r_in, cb), BF16), pltpu.VMEM((nb, r_out, cb), BF16)],
        compiler_params=pltpu.CompilerParams(dimension_semantics=("parallel", "parallel")),
        name=name,
    )(gm, x3)


def _cmul(x, k, half):
    xr, xi = x[:half], x[half:]
    kr, ki = k[:half], k[half:]
    return jnp.concatenate([xr * kr - xi * ki, xr * ki + xi * kr], axis=0)


def _spectral_mul_kernel(fb_ref, fbi_ref, a_ref, ak_ref, s_ref, z_ref, *, kb):
    fb, fbi = fb_ref[...].astype(BF16), fbi_ref[...].astype(BF16)
    half = fb.shape[0] // 2
    inv = 1.0 / s_ref[...]
    xs = [_mm(fb, a_ref[q]) for q in range(kb)]
    ks = [_mm(fb, ak_ref[q]) * inv for q in range(kb)]
    ys = [_cmul(xs[q], ks[q], half).astype(BF16) for q in range(kb)]
    for q in range(kb):
        z_ref[q] = _mm(fbi, ys[q]).astype(BF16)


def _spectral_mul(fb, fbi, a3, ak3, s, *, kb, cb):
    n1, r, e = a3.shape
    slab = pl.BlockSpec((kb, r, cb), lambda i, j: (i, 0, j))
    mat = pl.BlockSpec((r, r), lambda i, j: (0, 0))
    return pl.pallas_call(
        functools.partial(_spectral_mul_kernel, kb=kb),
        out_shape=jax.ShapeDtypeStruct((n1, r, e), BF16),
        grid=(n1 // kb, e // cb),
        in_specs=[mat, mat, slab, slab, pl.BlockSpec((1, cb), lambda i, j: (0, j))],
        out_specs=slab,
        compiler_params=pltpu.CompilerParams(dimension_semantics=("parallel", "parallel")),
        name="hyena_spectral_mul",
    )(fb, fbi, a3, ak3, s)


def _ctx_conv_kernel(fc_ref, fck_ref, fci_ref, u_ref, k_ref, s_ref, y_ref):
    x = _mm(fc_ref[...].astype(BF16), u_ref[...])
    ks = _mm(fck_ref[...].astype(BF16), k_ref[...]) * (1.0 / s_ref[...])
    y = _cmul(x, ks, x.shape[0] // 2).astype(BF16)
    y_ref[...] = _mm(fci_ref[...].astype(BF16), y).astype(BF16)


def _ctx_conv(fc, fck, fci, u2, k2, s, *, cb):
    nc, e = u2.shape
    full = lambda a: pl.BlockSpec(a.shape, lambda j: (0, 0))
    col = pl.BlockSpec((nc, cb), lambda j: (0, j))
    return pl.pallas_call(
        _ctx_conv_kernel,
        out_shape=jax.ShapeDtypeStruct((nc, e), BF16),
        grid=(e // cb,),
        in_specs=[full(fc), full(fck), full(fci), col, col, pl.BlockSpec((1, cb), lambda j: (0, j))],
        out_specs=col,
        compiler_params=pltpu.CompilerParams(dimension_semantics=("parallel",)),
        name="hyena_ctx_conv",
    )(fc, fck, fci, u2, k2, s)


def _long_conv_latent(u, k_raw, s, *, nb, cb, kb):
    b, l, e = u.shape
    n = 2 * l
    n1 = 1 << ((n.bit_length() - 1 + 1) // 2)
    n2 = n // n1
    g, gk, h, fb, fbi = (jnp.asarray(m) for m in _four_step_consts(n1, n2))
    a = _colmm(g, u.reshape(n1, n2, e), nb=nb, cb=cb, name="hyena_dft_rows")
    ak = _colmm(gk, k_raw.reshape(n1, n2, e), nb=nb, cb=cb, name="hyena_filter_dft_rows")
    z = _spectral_mul(fb, fbi, a.reshape(n1, 2 * n2, e), ak.reshape(n1, 2 * n2, e), s, kb=kb, cb=cb)
    y = _colmm(h, z.reshape(2 * n1, n2, e), nb=nb, cb=cb, name="hyena_idft_rows")
    return y.reshape(b, l, e)


def _long_conv_ctx(u, k_raw, s, *, cb):
    b, c, e = u.shape
    fc, fck, fci = (jnp.asarray(m) for m in _direct_dft_consts(2 * c))
    return _ctx_conv(fc, fck, fci, u.reshape(b * c, e), k_raw, s, cb=cb).reshape(b, c, e)


def _hy_out_kernel(y_ref, u_ref, g_ref, d_ref, w_ref, x_ref, gt_ref, o_ref):
    m = (y_ref[...].astype(F32) + u_ref[...].astype(F32) * d_ref[...]) * g_ref[...].astype(F32)
    o_ref[...] = x_ref[...] + gt_ref[...] * _mm(m.astype(BF16), w_ref[...])


def _hy_out(y, u, g, d_skip, w_bf, x, gt, *, tm):
    b, l, e = y.shape
    d = x.shape[-1]
    act = pl.BlockSpec((None, tm, e), lambda bb, i: (bb, i, 0))
    res = pl.BlockSpec((None, tm, d), lambda bb, i: (bb, i, 0))
    return pl.pallas_call(
        _hy_out_kernel,
        out_shape=jax.ShapeDtypeStruct(x.shape, F32),
        grid=(b, l // tm),
        in_specs=[act, act, act, pl.BlockSpec((1, e), lambda bb, i: (0, 0)),
                  pl.BlockSpec((e, d), lambda bb, i: (0, 0)), res,
                  pl.BlockSpec((None, 1, d), lambda bb, i: (bb, 0, 0))],
        out_specs=res,
        compiler_params=pltpu.CompilerParams(dimension_semantics=("parallel", "parallel")),
        name="hyena_out_proj",
    )(y, u, g, d_skip.reshape(1, e), w_bf, x, gt)


def _at_out_kernel(m_ref, w_ref, x_ref, gt_ref, o_ref):
    o_ref[...] = x_ref[...] + gt_ref[...] * _mm(m_ref[...], w_ref[...])


def _at_out(m, w_bf, x, gt, *, tm):
    b, l, e = m.shape
    d = x.shape[-1]
    res = pl.BlockSpec((None, tm, d), lambda bb, i: (bb, i, 0))
    return pl.pallas_call(
        _at_out_kernel,
        out_shape=jax.ShapeDtypeStruct(x.shape, F32),
        grid=(b, l // tm),
        in_specs=[pl.BlockSpec((None, tm, e), lambda bb, i: (bb, i, 0)),
                  pl.BlockSpec((e, d), lambda bb, i: (0, 0)), res,
                  pl.BlockSpec((None, 1, d), lambda bb, i: (bb, 0, 0))],
        out_specs=res,
        compiler_params=pltpu.CompilerParams(dimension_semantics=("parallel", "parallel")),
        name="attn_out_proj",
    )(m, w_bf, x, gt)


def _head_norm_rope(p, gain, cos, sin, scale):
    lane = lax.broadcasted_iota(jnp.int32, (1, HEAD_DIM), 1)
    first = (lane % (HEAD_DIM // 2)) < (HEAD_DIM // 4)
    outs = []
    for hd in range(p.shape[1] // HEAD_DIM):
        xh = p[:, hd * HEAD_DIM:(hd + 1) * HEAD_DIM]
        ms = jnp.mean(xh * xh, axis=-1, keepdims=True)
        yh = xh * lax.rsqrt(ms + EPS) * gain
        partner = jnp.where(first, pltpu.roll(yh, HEAD_DIM - HEAD_DIM // 4, 1),
                            pltpu.roll(yh, HEAD_DIM // 4, 1))
        outs.append((yh * cos + partner * sin) * scale)
    return jnp.concatenate(outs, axis=1)


def _at_proj_kernel(x_ref, g_ref, sc_ref, sh_ref, w_ref, qg_ref, kg_ref, cos_ref, sin_ref,
                    q_ref, k_ref, v_ref, sz_ref, hbuf):
    qd = q_ref.shape[-1]
    hbuf[...] = _norm_mod(x_ref[...], g_ref[...], sc_ref[...], sh_ref[...]).astype(BF16)

    def proj(lo):
        return _mm(hbuf[...], w_ref[:, lo:lo + GROUP_W])

    for lo in range(0, qd, GROUP_W):
        q_ref[:, lo:lo + GROUP_W] = _head_norm_rope(proj(lo), qg_ref[...], cos_ref[...], sin_ref[...],
                                                    HEAD_DIM ** -0.5 * LOG2E).astype(BF16)
    k_ref[...] = _head_norm_rope(proj(qd), kg_ref[...], cos_ref[...], sin_ref[...], 1.0).astype(BF16)
    v_ref[...] = proj(qd + GROUP_W).astype(BF16)
    for lo in range(0, qd, GROUP_W):
        p = proj(qd + 2 * GROUP_W + lo)
        sz_ref[:, lo:lo + GROUP_W] = (p * _sigmoid(p)).astype(BF16)


def _at_proj(x, g, sc, sh, w_bf, q_g, k_g, cos, sin, *, tm):
    b, l, d = x.shape
    kvd = GROUP_W
    qd = (w_bf.shape[1] - 2 * kvd) // 2
    const = lambda shape: pl.BlockSpec(shape, lambda bb, i: (0, 0), pipeline_mode=pl.Buffered(1))
    vec = pl.BlockSpec((None, 1, d), lambda bb, i: (bb, 0, 0))
    tab = pl.BlockSpec((tm, HEAD_DIM), lambda bb, i: (i, 0))
    wide = pl.BlockSpec((None, tm, qd), lambda bb, i: (bb, i, 0))
    narrow = pl.BlockSpec((None, tm, kvd), lambda bb, i: (bb, i, 0))
    return pl.pallas_call(
        _at_proj_kernel,
        out_shape=(jax.ShapeDtypeStruct((b, l, qd), BF16), jax.ShapeDtypeStruct((b, l, kvd), BF16),
                   jax.ShapeDtypeStruct((b, l, kvd), BF16), jax.ShapeDtypeStruct((b, l, qd), BF16)),
        grid=(b, l // tm),
        in_specs=[
            pl.BlockSpec((None, tm, d), lambda bb, i: (bb, i, 0)),
            const((1, d)), vec, vec, const(w_bf.shape),
            const((1, HEAD_DIM)), const((1, HEAD_DIM)), tab, tab,
        ],
        out_specs=(wide, narrow, narrow, wide),
        scratch_shapes=[pltpu.VMEM((tm, d), BF16)],
        compiler_params=pltpu.CompilerParams(dimension_semantics=("parallel", "parallel")),
        name="attn_in_proj",
    )(x, g.reshape(1, d), sc, sh, w_bf, q_g.reshape(1, -1), k_g.reshape(1, -1), cos, sin)


def _at_proj_kv_kernel(x_ref, g_ref, sc_ref, sh_ref, wk_ref, wv_ref, kg_ref, k_ref, v_ref):
    h = _norm_mod(x_ref[...], g_ref[...], sc_ref[...], sh_ref[...]).astype(BF16)
    p = _mm(h, wk_ref[...])
    gain = kg_ref[...]
    outs = []
    for hd in range(p.shape[1] // HEAD_DIM):
        xh = p[:, hd * HEAD_DIM:(hd + 1) * HEAD_DIM]
        ms = jnp.mean(xh * xh, axis=-1, keepdims=True)
        outs.append(xh * lax.rsqrt(ms + EPS) * gain)
    k_ref[...] = jnp.concatenate(outs, axis=1).astype(BF16)
    v_ref[...] = _mm(h, wv_ref[...]).astype(BF16)


def _at_proj_kv(x, g, sc, sh, w_bf, k_g, *, qd):
    b, c, d = x.shape
    kb = qd // GROUP_W
    vec = pl.BlockSpec((None, 1, d), lambda bb: (bb, 0, 0))
    out = pl.BlockSpec((None, c, GROUP_W), lambda bb: (bb, 0, 0))
    return pl.pallas_call(
        _at_proj_kv_kernel,
        out_shape=(jax.ShapeDtypeStruct((b, c, GROUP_W), BF16),) * 2,
        grid=(b,),
        in_specs=[
            pl.BlockSpec((None, c, d), lambda bb: (bb, 0, 0)),
            pl.BlockSpec((1, d), lambda bb: (0, 0)), vec, vec,
            pl.BlockSpec((d, GROUP_W), lambda bb: (0, kb)),
            pl.BlockSpec((d, GROUP_W), lambda bb: (0, kb + 1)),
            pl.BlockSpec((1, HEAD_DIM), lambda bb: (0, 0)),
        ],
        out_specs=(out, out),
        compiler_params=pltpu.CompilerParams(dimension_semantics=("parallel",)),
        name="attn_ctx_kv_proj",
    )(x, g.reshape(1, d), sc, sh, w_bf, w_bf, k_g.reshape(1, -1))


def _attn_kernel(q_ref, k_ref, v_ref, kc_ref, vc_ref, sz_ref, o_ref, vt_ref, ka_ref, kmax_ref, acc_ref,
                 sbuf, *, tq, tk, n_k, c, nsub):
    width = GROUP * tq

    @pl.when(pl.program_id(2) == 0)
    def _():
        ones = jnp.ones((VT_ROWS - HEAD_DIM, tk), BF16)
        def aug(rows):
            lane = lax.broadcasted_iota(jnp.int32, (rows, HEAD_DIM), 1)
            return jnp.where(lane < 2, 1.0, 0.0).astype(BF16)

        kk = jnp.zeros((1, 1), F32)
        chunks = [(t, k_ref[t * tk:(t + 1) * tk, :], v_ref[t * tk:(t + 1) * tk, :]) for t in range(n_k)]
        chunks.append((n_k, kc_ref[...], vc_ref[...]))
        for t, kt, vt in chunks:
            rows = kt.shape[0]
            vt_ref[t, 0:HEAD_DIM, 0:rows] = vt.astype(F32).T.astype(BF16)
            vt_ref[t, HEAD_DIM:, :] = ones
            ka_ref[t, 0:rows, 0:HEAD_DIM] = kt
            ka_ref[t, 0:rows, HEAD_DIM:] = aug(rows)
            kf = kt.astype(F32)
            kk = jnp.maximum(kk, jnp.max(jnp.sum(kf * kf, axis=1, keepdims=True), axis=0, keepdims=True))
        kmax_ref[...] = jnp.broadcast_to(jnp.sqrt(kk), kmax_ref.shape)

    qts, stab, gap = [], [], None
    for sb in range(nsub):
        q = q_ref[sb * tq:(sb + 1) * tq, :]
        qf = [q[:, gi * HEAD_DIM:(gi + 1) * HEAD_DIM].astype(F32).T for gi in range(GROUP)]
        qts.append(jnp.concatenate([x.astype(BF16) for x in qf], axis=1))
        qnorm = jnp.sqrt(jnp.concatenate([jnp.sum(x * x, axis=0, keepdims=True) for x in qf], axis=1))
        low = jnp.max(_mm(k_ref[0:STAB_KEYS, :], qts[sb]), axis=0, keepdims=True)
        stab.append(low)
        g = jnp.max(qnorm * kmax_ref[0:1, 0:1] - low)
        gap = g if gap is None else jnp.maximum(gap, g)
    acc_ref[...] = jnp.zeros_like(acc_ref)
    n_loop = (n_k - 2) // KV_UNROLL

    def run(scores, consume, first, chunk, ctx, carry):
        def body(i, carry):
            t = KV_UNROLL * i
            for u in range(KV_UNROLL):
                scores(chunk(t + u + 1), (u + 1) % 2)
                carry = consume(u % 2, vt_ref[t + u], carry)
            return carry

        scores(first, 0)
        carry = lax.fori_loop(0, n_loop, body, carry)
        for t in range(n_loop * KV_UNROLL, n_k):
            scores(chunk(t + 1) if t + 1 < n_k else ctx, (t + 1) % 2)
            carry = consume(t % 2, vt_ref[t], carry)
        consume(n_k % 2, vt_ref[n_k, :, 0:c], carry)

    def finish(denominator):
        for sb in range(nsub):
            acc = acc_ref[sb]
            ot = acc[0:HEAD_DIM] * (1.0 / denominator(acc))
            o = jnp.concatenate([ot[:, gi * tq:(gi + 1) * tq].T for gi in range(GROUP)], axis=1)
            rows = slice(sb * tq, (sb + 1) * tq)
            o_ref[rows, :] = (o * sz_ref[rows, :].astype(F32)).astype(BF16)

    @pl.when(gap <= STAB_GAP)
    def _():
        qas = []
        for sb in range(nsub):
            bits = lax.bitcast_convert_type(stab[sb], jnp.uint32) & jnp.uint32(0xFFFF0000)
            m_hi = lax.bitcast_convert_type(bits, F32)
            m_lo = stab[sb] - m_hi
            row = lax.broadcasted_iota(jnp.int32, (16, width), 0)
            extra = jnp.where(row == 0, -m_hi, jnp.where(row == 1, -m_lo, 0.0))
            qas.append(jnp.concatenate([qts[sb], extra.astype(BF16),
                                        jnp.zeros((HEAD_DIM - 16, width), BF16)], axis=0))

        def scores(kt, slot):
            for sb in range(nsub):
                sbuf[slot, sb, 0:kt.shape[0]] = _mm(kt, qas[sb])

        def consume(slot, vt, carry):
            for sb in range(nsub):
                p = jnp.exp2(sbuf[slot, sb, 0:vt.shape[1]])
                acc_ref[sb, HEAD_DIM:HEAD_DIM + 8] += jnp.sum(p.reshape(p.shape[0] // 8, 8, width), axis=0)
                acc_ref[sb, 0:HEAD_DIM] += _mm(vt[0:HEAD_DIM], p.astype(BF16))
            return carry

        run(scores, consume, ka_ref[0], lambda t: ka_ref[t], ka_ref[n_k, 0:c, :], 0)
        finish(lambda acc: jnp.sum(acc[HEAD_DIM:HEAD_DIM + 8], axis=0, keepdims=True))

    @pl.when(jnp.logical_not(gap <= STAB_GAP))
    def _():
        def scores(kt, slot):
            for sb in range(nsub):
                sbuf[slot, sb, 0:kt.shape[0]] = _mm(kt, qts[sb])

        def consume(slot, vt, ms):
            out = []
            for sb in range(nsub):
                s = sbuf[slot, sb, 0:vt.shape[1]]
                m_new = jnp.maximum(ms[sb], jnp.max(s, axis=0, keepdims=True))
                p = jnp.exp2(s - m_new).astype(BF16)
                acc_ref[sb] = jnp.exp2(ms[sb] - m_new) * acc_ref[sb] + _mm(vt, p)
                out.append(m_new)
            return tuple(out)

        def key_chunk(t):
            start = t * tk if isinstance(t, int) else pl.multiple_of(t * tk, tk)
            return k_ref[pl.ds(start, tk), :]

        ms = tuple(jnp.full((1, width), -jnp.inf, F32) for _ in range(nsub))
        run(scores, consume, k_ref[0:tk, :], key_chunk, kc_ref[...], ms)
        finish(lambda acc: acc[HEAD_DIM:HEAD_DIM + 1])


def _attention(q, k, v, kc, vc, sz, *, tq, tk, nsub):
    b, l, qd = q.shape
    c = kc.shape[1]
    n_kv = qd // GROUP_W
    n_k = l // tk
    assert n_k % 2 == 0 and n_k >= 2 and c <= tk
    qspec = pl.BlockSpec((None, nsub * tq, GROUP_W), lambda bb, hh, i: (bb, i, hh))
    kspec = pl.BlockSpec((None, l, HEAD_DIM), lambda bb, hh, i: (bb, 0, hh))
    cspec = pl.BlockSpec((None, c, HEAD_DIM), lambda bb, hh, i: (bb, 0, hh))
    return pl.pallas_call(
        functools.partial(_attn_kernel, tq=tq, tk=tk, n_k=n_k, c=c, nsub=nsub),
        out_shape=jax.ShapeDtypeStruct((b, l, qd), BF16),
        grid=(b, n_kv, l // (nsub * tq)),
        in_specs=[qspec, kspec, kspec, cspec, cspec, qspec],
        out_specs=qspec,
        scratch_shapes=[pltpu.VMEM((n_k + 1, VT_ROWS, tk), BF16),
                        pltpu.VMEM((n_k + 1, tk, 2 * HEAD_DIM), BF16),
                        pltpu.VMEM((8, LANES), F32),
                        pltpu.VMEM((nsub, VT_ROWS, GROUP * tq), F32),
                        pltpu.VMEM((2, nsub, tk, GROUP * tq), F32)],
        compiler_params=pltpu.CompilerParams(
            dimension_semantics=("parallel", "parallel", "arbitrary")),
        name="flash_attention",
    )(q, k, v, kc, vc, sz)


def _rope_tables(l):
    rows = l // GRID_W
    axis_dim = HEAD_DIM // 2
    row = np.repeat(np.arange(rows, dtype=np.float64), GRID_W)
    col = np.tile(np.arange(GRID_W, dtype=np.float64), rows)
    inv = 1.0 / (ROPE_THETA ** (np.arange(0, axis_dim, 2, dtype=np.float64) / axis_dim))
    ra, ca = row[:, None] * inv[None, :], col[:, None] * inv[None, :]
    cos = np.concatenate([np.cos(ra), np.cos(ra), np.cos(ca), np.cos(ca)], axis=-1)
    sin = np.concatenate([-np.sin(ra), np.sin(ra), -np.sin(ca), np.sin(ca)], axis=-1)
    return jnp.asarray(cos, F32), jnp.asarray(sin, F32)


def _pick(n, target):
    t = min(n, target)
    while n % t:
        t //= 2
    return t


class _Tiles(NamedTuple):
    proj_rows: int
    ctx_rows: int
    hy_cols: int
    out_rows: int
    ctx_out_rows: int
    filter_rows: int
    ctx_filter_rows: int
    dft_cols: int
    dft_group: int
    dft_slabs: int
    at_rows: int
    q_rows: int
    q_subtiles: int
    key_rows: int


def _tiles(l, n_ctx, e):
    return _Tiles(
        proj_rows=_pick(l, 1024), ctx_rows=_pick(n_ctx, 1024), hy_cols=_pick(e, 256),
        out_rows=_pick(l, 512), ctx_out_rows=_pick(n_ctx, 512),
        filter_rows=_pick(l, 256), ctx_filter_rows=_pick(n_ctx, 256),
        dft_cols=_pick(e, 512), dft_group=16, dft_slabs=4,
        at_rows=_pick(l, 512), q_rows=_pick(l // 2, 256), q_subtiles=2, key_rows=_pick(l // 2, 512))


def kernel(x, c, ctx, c_ctx, norm_g, ada_w, ada_b,
           hy_w_in, hy_conv_w, hy_conv_b, hy_fw1, hy_fb1, hy_fw2, hy_fb2, hy_fw3, hy_fb3,
           hy_fw4, hy_freq, hy_d, hy_w_out,
           at_w_in, at_q_g, at_k_g, at_w_out):
    b, l, d = x.shape
    n_ctx = ctx.shape[1]
    e = hy_d.shape[-1]

    cv = jnp.concatenate([c, c_ctx[None], jnp.zeros((8 - b - 1, d), F32)], axis=0)
    mod = _modulation(cv, ada_w, ada_b)

    def lat(v):
        return v[:b, None, :]

    def cx(v):
        return jnp.broadcast_to(v[b][None, None, :], (b, 1, d))

    sh0, sc0, gt0 = jnp.split(mod[0], 3, axis=-1)
    w_in = hy_w_in[0].astype(BF16)
    w_out = hy_w_out[0].astype(BF16)
    fargs = (hy_fw1[0], hy_fb1[0], hy_fw2[0], hy_fb2[0], hy_fw3[0], hy_fb3[0], hy_fw4[0], hy_freq[0])
    t = _tiles(l, n_ctx, e)

    u_lat, g_lat = _hy_proj(x, norm_g[0], lat(sc0), lat(sh0), w_in, hy_conv_w[0], hy_conv_b[0],
                            tm=t.proj_rows, tn=t.hy_cols)
    u_ctx, g_ctx = _hy_proj(ctx, norm_g[0], cx(sc0), cx(sh0), w_in, hy_conv_w[0], hy_conv_b[0],
                            tm=t.ctx_rows, tn=t.hy_cols)
    k_lat, s_lat = _hyena_filter(l, *fargs, tr=t.filter_rows)
    k_ctx, s_ctx = _hyena_filter(n_ctx, *fargs, tr=t.ctx_filter_rows)
    y_lat = _long_conv_latent(u_lat, k_lat, s_lat, nb=t.dft_group, cb=t.dft_cols, kb=t.dft_slabs)
    y_ctx = _long_conv_ctx(u_ctx, k_ctx, s_ctx, cb=t.dft_cols)
    x1 = _hy_out(y_lat, u_lat, g_lat, hy_d[0], w_out, x, lat(gt0), tm=t.out_rows)
    ctx1 = _hy_out(y_ctx, u_ctx, g_ctx, hy_d[0], w_out, ctx, cx(gt0), tm=t.ctx_out_rows)

    sh1, sc1, gt1 = jnp.split(mod[1], 3, axis=-1)
    aw_in = at_w_in[0].astype(BF16)
    aw_out = at_w_out[0].astype(BF16)
    cos, sin = _rope_tables(l)
    q, k, v, sz = _at_proj(x1, norm_g[1], lat(sc1), lat(sh1), aw_in, at_q_g[0], at_k_g[0], cos, sin,
                           tm=t.at_rows)
    kc, vc = _at_proj_kv(ctx1, norm_g[1], cx(sc1), cx(sh1), aw_in, at_k_g[0], qd=q.shape[-1])
    om = _attention(q, k, v, kc, vc, sz, tq=t.q_rows, tk=t.key_rows, nsub=t.q_subtiles)
    return _at_out(om, aw_out, x1, lat(gt1), tm=t.out_rows)
```

```python
import functools
import math
from typing import NamedTuple

import numpy as np
import jax
import jax.numpy as jnp
from jax import lax
from jax.experimental import pallas as pl
from jax.experimental.pallas import tpu as pltpu

F32 = jnp.float32
BF16 = jnp.bfloat16
HIGHEST = lax.Precision.HIGHEST

EPS = 1e-6
HEAD_DIM = 128
GROUP = 4
GROUP_W = GROUP * HEAD_DIM
GRID_W = 64
ROPE_THETA = 10000.0
FILTER_BANDS = 16
DECAY_TARGET = 1e-2
MAX_DECAY = math.log(DECAY_TARGET) / 0.3
MIN_DECAY = math.log(DECAY_TARGET) / 1.5
HALO = 16
LANES = 128
VT_ROWS = HEAD_DIM + 16
LOG2E = math.log2(math.e)
KV_UNROLL = 4
STAB_KEYS = 128
STAB_GAP = 90.0


def _sigmoid(x):
    return 1.0 / (1.0 + jnp.exp(-x))


def _mm(a, b):
    return jnp.dot(a, b, preferred_element_type=F32)


def _mm_f32(a, b):
    return jnp.dot(a, b, preferred_element_type=F32, precision=HIGHEST)


def _mod_kernel(c_ref, w_ref, b_ref, o_ref):
    cv = c_ref[...]
    o_ref[...] = _mm_f32(cv * _sigmoid(cv), w_ref[...]) + b_ref[...]


def _modulation(cv, ada_w, ada_b):
    depth, d, d3 = ada_w.shape
    rows = cv.shape[0]
    tn = d
    return pl.pallas_call(
        _mod_kernel,
        out_shape=jax.ShapeDtypeStruct((depth, rows, d3), F32),
        grid=(depth, d3 // tn),
        in_specs=[
            pl.BlockSpec((rows, d), lambda l, j: (0, 0)),
            pl.BlockSpec((None, d, tn), lambda l, j: (l, 0, j)),
            pl.BlockSpec((None, 1, tn), lambda l, j: (l, 0, j)),
        ],
        out_specs=pl.BlockSpec((None, rows, tn), lambda l, j: (l, 0, j)),
        name="modulation",
    )(cv, ada_w, ada_b.reshape(depth, 1, d3))


def _norm_mod(x, g, sc, sh):
    ms = jnp.mean(x * x, axis=-1, keepdims=True)
    return x * lax.rsqrt(ms + EPS) * (g * (1.0 + sc)) + sh


def _hy_proj_kernel(xm_ref, xp_ref, xn_ref, g_ref, sc_ref, sh_ref, w_ref, cw_ref, cb_ref,
                    u_ref, gate_ref, hbuf, *, tm, tn, n_i):
    i = pl.program_id(1)
    e = u_ref.shape[-1]
    g, sc, sh = g_ref[...], sc_ref[...], sh_ref[...]
    hbuf[HALO:HALO + tm, :] = _norm_mod(xm_ref[...], g, sc, sh).astype(BF16)
    hp = _norm_mod(xp_ref[...], g, sc, sh)
    hbuf[0:HALO, :] = jnp.where(i > 0, hp, 0.0).astype(BF16)
    hn = _norm_mod(xn_ref[...], g, sc, sh)
    hbuf[HALO + tm:, :] = jnp.where(i < n_i - 1, hn, 0.0).astype(BF16)

    def conv(lo):
        p = _mm(hbuf[...], w_ref[:, lo:lo + tn])
        cw = cw_ref[:, lo:lo + tn]
        rows = p.shape[0]
        prev = pltpu.roll(p, 1, 0)[HALO:HALO + tm]
        nxt = pltpu.roll(p, rows - 1, 0)[HALO:HALO + tm]
        return prev * cw[0:1] + p[HALO:HALO + tm] * cw[1:2] + nxt * cw[2:3] + cb_ref[:, lo:lo + tn]

    for j in range(e // tn):
        lo = j * tn
        u_ref[:, lo:lo + tn] = (conv(e + lo) * conv(2 * e + lo)).astype(BF16)
        z = _mm(hbuf[HALO:HALO + tm, :], w_ref[:, 3 * e + lo:3 * e + lo + tn])
        gate_ref[:, lo:lo + tn] = (conv(lo) * z * _sigmoid(z)).astype(BF16)


def _hy_proj(x, g, sc, sh, w_bf, conv_w, conv_b, *, tm, tn):
    b, l, d = x.shape
    e = w_bf.shape[1] // 4
    n_i = l // tm
    hb = tm // HALO
    last_hb = l // HALO - 1
    const = lambda shape: pl.BlockSpec(shape, lambda bb, i: (0, 0), pipeline_mode=pl.Buffered(1))
    vec = pl.BlockSpec((None, 1, d), lambda bb, i: (bb, 0, 0))
    out = pl.BlockSpec((None, tm, e), lambda bb, i: (bb, i, 0))
    return pl.pallas_call(
        functools.partial(_hy_proj_kernel, tm=tm, tn=tn, n_i=n_i),
        out_shape=(jax.ShapeDtypeStruct((b, l, e), BF16), jax.ShapeDtypeStruct((b, l, e), BF16)),
        grid=(b, n_i),
        in_specs=[
            pl.BlockSpec((None, tm, d), lambda bb, i: (bb, i, 0)),
            pl.BlockSpec((None, HALO, d), lambda bb, i: (bb, jnp.maximum(i * hb - 1, 0), 0)),
            pl.BlockSpec((None, HALO, d), lambda bb, i: (bb, jnp.minimum((i + 1) * hb, last_hb), 0)),
            const((1, d)), vec, vec,
            const((d, 4 * e)), const((3, 3 * e)), const((1, 3 * e)),
        ],
        out_specs=(out, out),
        scratch_shapes=[pltpu.VMEM((tm + 2 * HALO, d), BF16)],
        compiler_params=pltpu.CompilerParams(dimension_semantics=("parallel", "parallel")),
        name="hyena_in_proj",
    )(x, x, x, g.reshape(1, d), sc, sh, w_bf, conv_w, conv_b.reshape(1, -1))


def _split_bf16(a):
    hi = a.astype(BF16)
    return hi, (a - hi.astype(F32)).astype(BF16)


def _mm_split(w, a):
    w_hi, w_lo = _split_bf16(w)
    a_hi, a_lo = _split_bf16(a)
    return _mm(w_hi, a_hi) + _mm(w_lo, a_hi) + _mm(w_hi, a_lo)


def _filter_kernel(emb_ref, w1_ref, w2_ref, w3_ref, col_ref, w4_ref, dl_ref, k_ref, s_ref, *, l, tr):
    t_idx = pl.program_id(0)
    r_row = t_idx * tr + lax.broadcasted_iota(jnp.int32, (1, tr), 1)
    p_row = jnp.where(r_row < l, r_row, 2 * l - r_row).astype(F32)
    emb = emb_ref[...]
    z = (emb[:, 2:3] * (p_row / (l - 1.0))
         + emb[:, 3:4] * jnp.cos(emb[:, 0:1] * ((2.0 * math.pi) * p_row / l) + emb[:, 1:2]))
    col = col_ref[...]
    fq = col[:, 3:4]
    h = jnp.sin(fq * (_mm_split(w1_ref[...], z) + col[:, 0:1]))
    h = jnp.sin(fq * (_mm_split(w2_ref[...], h) + col[:, 1:2]))
    h = jnp.sin(fq * (_mm_split(w3_ref[...], h) + col[:, 2:3]))
    h_hi, h_lo = _split_bf16(h)
    w_hi, w_lo = _split_bf16(w4_ref[...])
    k = lax.dot_general(jnp.concatenate([h_hi, h_lo, h_hi], axis=0),
                        jnp.concatenate([w_hi, w_hi, w_lo], axis=0),
                        (((0,), (0,)), ((), ())), preferred_element_type=F32)
    r_col = t_idx * tr + lax.broadcasted_iota(jnp.int32, (tr, 1), 0)
    t_col = jnp.where(r_col < l, r_col, 2 * l - r_col).astype(F32) / (l - 1.0)
    k = jnp.where(r_col == l, 0.0, k * jnp.exp(-t_col * dl_ref[...]))
    k_ref[...] = k.astype(BF16)

    @pl.when(t_idx == 0)
    def _():
        s_ref[...] = jnp.zeros_like(s_ref)

    s_ref[...] += jnp.sum(jnp.abs(k), axis=0, keepdims=True)


EMB_ROWS = 48


def _filter_embedding_table():
    fr = np.linspace(1e-4, FILTER_BANDS - 1, FILTER_BANDS)
    emb = np.zeros((EMB_ROWS, 4), np.float32)
    emb[1:1 + FILTER_BANDS, 0] = fr
    emb[1 + FILTER_BANDS:1 + 2 * FILTER_BANDS, 0] = fr
    emb[1 + FILTER_BANDS:1 + 2 * FILTER_BANDS, 1] = math.pi / 2
    emb[0, 2] = 1.0
    emb[1:1 + 2 * FILTER_BANDS, 3] = 1.0
    return emb


def _hyena_filter(l, fw1, fb1, fw2, fb2, fw3, fb3, fw4, freq, *, tr):
    width = fw1.shape[1]
    e = fw4.shape[1] // 2
    w1t = jnp.zeros((width, EMB_ROWS), F32).at[:, :fw1.shape[0]].set(fw1.T)
    cols = jnp.stack([fb1, fb2, fb3, freq], axis=1)
    deltas = jnp.abs(jnp.linspace(MIN_DECAY, MAX_DECAY, e, dtype=F32)).reshape(1, e)
    half = l // tr
    small = lambda shape: pl.BlockSpec(shape, lambda t: (0, 0))
    return pl.pallas_call(
        functools.partial(_filter_kernel, l=l, tr=tr),
        out_shape=(jax.ShapeDtypeStruct((2 * l, e), BF16), jax.ShapeDtypeStruct((1, e), F32)),
        grid=(2 * half,),
        in_specs=[
            small((EMB_ROWS, 4)), small((width, EMB_ROWS)), small((width, width)), small((width, width)),
            small((width, 4)),
            pl.BlockSpec((width, e), lambda t: (0, t // half)),
            small((1, e)),
        ],
        out_specs=(pl.BlockSpec((tr, e), lambda t: (t, 0)), pl.BlockSpec((1, e), lambda t: (0, 0))),
        compiler_params=pltpu.CompilerParams(dimension_semantics=("arbitrary",)),
        name="hyena_filter",
    )(jnp.asarray(_filter_embedding_table()), w1t, fw2.T, fw3.T, cols, fw4, deltas)


@functools.lru_cache(maxsize=None)
def _four_step_consts(n1, n2):
    n = n1 * n2
    hh = n1 // 2
    k1 = np.arange(n1)
    c2 = np.arange(n2)

    def angles(sign, n1_count):
        nn = n2 * np.arange(n1_count)[None, None, :] + c2[:, None, None]
        return sign * 2.0 * np.pi * ((k1[None, :, None] * nn) % n) / n

    a = angles(-1.0, hh)
    c, s = np.cos(a), np.sin(a)
    g = np.zeros((n2, n1, 2, 2, hh))
    g[:, :, 0, 0], g[:, :, 0, 1], g[:, :, 1, 0], g[:, :, 1, 1] = c, -s, s, c
    g = g.reshape(n2, 2 * n1, n1)

    a = angles(-1.0, n1)
    gk = np.stack([np.cos(a), np.sin(a)], axis=2).reshape(n2, 2 * n1, n1)

    a = np.transpose(angles(1.0, hh), (0, 2, 1))
    c, s = np.cos(a) / n, np.sin(a) / n
    h = np.zeros((n2, 2, hh, n1, 2))
    h[:, 0, :, :, 0], h[:, 0, :, :, 1], h[:, 1, :, :, 0], h[:, 1, :, :, 1] = c, -s, s, c
    h = h.reshape(n2, n1, 2 * n1)

    phi = -2.0 * np.pi * ((c2[:, None] * c2[None, :]) % n2) / n2
    c, s = np.cos(phi), np.sin(phi)
    fb = np.block([[c, -s], [s, c]])
    fbi = np.block([[c, s], [-s, c]])
    f = lambda x: np.asarray(x, np.float32)
    return f(g), f(gk), f(h), f(fb), f(fbi)


@functools.lru_cache(maxsize=None)
def _direct_dft_consts(nc):
    hh = nc // 2
    k = np.arange(nc)
    a = -2.0 * np.pi * ((k[:, None] * np.arange(hh)[None, :]) % nc) / nc
    c, s = np.cos(a), np.sin(a)
    fc = np.block([[c, -s], [s, c]])
    a = -2.0 * np.pi * ((k[:, None] * k[None, :]) % nc) / nc
    fck = np.concatenate([np.cos(a), np.sin(a)], axis=0)
    a = 2.0 * np.pi * ((np.arange(hh)[:, None] * k[None, :]) % nc) / nc
    c, s = np.cos(a) / nc, np.sin(a) / nc
    fci = np.block([[c, -s], [s, c]])
    f = lambda x: np.asarray(x, np.float32)
    return f(fc), f(fck), f(fci)


def _colmm_kernel(g_ref, x_ref, o_ref, xs_ref, rs_ref, *, nb):
    xs_ref[...] = pltpu.einshape("abc->bac", x_ref[...])
    for q in range(nb):
        rs_ref[q] = _mm(g_ref[q].astype(BF16), xs_ref[q]).astype(rs_ref.dtype)
    o_ref[...] = pltpu.einshape("abc->bac", rs_ref[...])


def _colmm(gm, x3, *, nb, cb, name):
    n2, r_out, r_in = gm.shape
    e = x3.shape[-1]
    return pl.pallas_call(
        functools.partial(_colmm_kernel, nb=nb),
        out_shape=jax.ShapeDtypeStruct((r_out, n2, e), BF16),
        grid=(n2 // nb, e // cb),
        in_specs=[
            pl.BlockSpec((nb, r_out, r_in), lambda s, j: (s, 0, 0)),
            pl.BlockSpec((r_in, nb, cb), lambda s, j: (0, s, j)),
        ],
        out_specs=pl.BlockSpec((r_out, nb, cb), lambda s, j: (0, s, j)),
        scratch_shapes=[pltpu.VMEM((nb, r_in, cb), BF16), pltpu.VMEM((nb, r_out, cb), BF16)],
        compiler_params=pltpu.CompilerParams(dimension_semantics=("parallel", "parallel")),
        name=name,
    )(gm, x3)


def _cmul(x, k, half):
    xr, xi = x[:half], x[half:]
    kr, ki = k[:half], k[half:]
    return jnp.concatenate([xr * kr - xi * ki, xr * ki + xi * kr], axis=0)


def _spectral_mul_kernel(fb_ref, fbi_ref, a_ref, ak_ref, s_ref, z_ref, *, kb):
    fb, fbi = fb_ref[...].astype(BF16), fbi_ref[...].astype(BF16)
    half = fb.shape[0] // 2
    inv = 1.0 / s_ref[...]
    xs = [_mm(fb, a_ref[q]) for q in range(kb)]
    ks = [_mm(fb, ak_ref[q]) * inv for q in range(kb)]
    ys = [_cmul(xs[q], ks[q], half).astype(BF16) for q in range(kb)]
    for q in range(kb):
        z_ref[q] = _mm(fbi, ys[q]).astype(BF16)


def _spectral_mul(fb, fbi, a3, ak3, s, *, kb, cb):
    n1, r, e = a3.shape
    slab = pl.BlockSpec((kb, r, cb), lambda i, j: (i, 0, j))
    mat = pl.BlockSpec((r, r), lambda i, j: (0, 0))
    return pl.pallas_call(
        functools.partial(_spectral_mul_kernel, kb=kb),
        out_shape=jax.ShapeDtypeStruct((n1, r, e), BF16),
        grid=(n1 // kb, e // cb),
        in_specs=[mat, mat, slab, slab, pl.BlockSpec((1, cb), lambda i, j: (0, j))],
        out_specs=slab,
        compiler_params=pltpu.CompilerParams(dimension_semantics=("parallel", "parallel")),
        name="hyena_spectral_mul",
    )(fb, fbi, a3, ak3, s)


def _ctx_conv_kernel(fc_ref, fck_ref, fci_ref, u_ref, k_ref, s_ref, y_ref):
    x = _mm(fc_ref[...].astype(BF16), u_ref[...])
    ks = _mm(fck_ref[...].astype(BF16), k_ref[...]) * (1.0 / s_ref[...])
    y = _cmul(x, ks, x.shape[0] // 2).astype(BF16)
    y_ref[...] = _mm(fci_ref[...].astype(BF16), y).astype(BF16)


def _ctx_conv(fc, fck, fci, u2, k2, s, *, cb):
    nc, e = u2.shape
    full = lambda a: pl.BlockSpec(a.shape, lambda j: (0, 0))
    col = pl.BlockSpec((nc, cb), lambda j: (0, j))
    return pl.pallas_call(
        _ctx_conv_kernel,
        out_shape=jax.ShapeDtypeStruct((nc, e), BF16),
        grid=(e // cb,),
        in_specs=[full(fc), full(fck), full(fci), col, col, pl.BlockSpec((1, cb), lambda j: (0, j))],
        out_specs=col,
        compiler_params=pltpu.CompilerParams(dimension_semantics=("parallel",)),
        name="hyena_ctx_conv",
    )(fc, fck, fci, u2, k2, s)


def _long_conv_latent(u, k_raw, s, *, nb, cb, kb, scb):
    b, l, e = u.shape
    n = 2 * l
    n1 = 1 << ((n.bit_length() - 1 + 1) // 2)
    n2 = n // n1
    g, gk, h, fb, fbi = (jnp.asarray(m) for m in _four_step_consts(n1, n2))
    a = _colmm(g, u.reshape(n1, n2, e), nb=nb, cb=cb, name="hyena_dft_rows")
    ak = _colmm(gk, k_raw.reshape(n1, n2, e), nb=nb, cb=cb, name="hyena_filter_dft_rows")
    z = _spectral_mul(fb, fbi, a.reshape(n1, 2 * n2, e), ak.reshape(n1, 2 * n2, e), s, kb=kb, cb=scb)
    y = _colmm(h, z.reshape(2 * n1, n2, e), nb=nb, cb=cb, name="hyena_idft_rows")
    return y.reshape(b, l, e)


def _long_conv_ctx(u, k_raw, s, *, cb):
    b, c, e = u.shape
    fc, fck, fci = (jnp.asarray(m) for m in _direct_dft_consts(2 * c))
    return _ctx_conv(fc, fck, fci, u.reshape(b * c, e), k_raw, s, cb=cb).reshape(b, c, e)


def _hy_out_kernel(y_ref, u_ref, g_ref, d_ref, w_ref, x_ref, gt_ref, o_ref):
    m = (y_ref[...].astype(F32) + u_ref[...].astype(F32) * d_ref[...]) * g_ref[...].astype(F32)
    o_ref[...] = x_ref[...] + gt_ref[...] * _mm(m.astype(BF16), w_ref[...])


def _hy_out(y, u, g, d_skip, w_bf, x, gt, *, tm):
    b, l, e = y.shape
    d = x.shape[-1]
    act = pl.BlockSpec((None, tm, e), lambda bb, i: (bb, i, 0))
    res = pl.BlockSpec((None, tm, d), lambda bb, i: (bb, i, 0))
    return pl.pallas_call(
        _hy_out_kernel,
        out_shape=jax.ShapeDtypeStruct(x.shape, F32),
        grid=(b, l // tm),
        in_specs=[act, act, act, pl.BlockSpec((1, e), lambda bb, i: (0, 0)),
                  pl.BlockSpec((e, d), lambda bb, i: (0, 0)), res,
                  pl.BlockSpec((None, 1, d), lambda bb, i: (bb, 0, 0))],
        out_specs=res,
        compiler_params=pltpu.CompilerParams(dimension_semantics=("parallel", "parallel")),
        name="hyena_out_proj",
    )(y, u, g, d_skip.reshape(1, e), w_bf, x, gt)


def _at_out_kernel(m_ref, w_ref, x_ref, gt_ref, o_ref):
    o_ref[...] = x_ref[...] + gt_ref[...] * _mm(m_ref[...], w_ref[...])


def _at_out(m, w_bf, x, gt, *, tm):
    b, l, e = m.shape
    d = x.shape[-1]
    res = pl.BlockSpec((None, tm, d), lambda bb, i: (bb, i, 0))
    return pl.pallas_call(
        _at_out_kernel,
        out_shape=jax.ShapeDtypeStruct(x.shape, F32),
        grid=(b, l // tm),
        in_specs=[pl.BlockSpec((None, tm, e), lambda bb, i: (bb, i, 0)),
                  pl.BlockSpec((e, d), lambda bb, i: (0, 0)), res,
                  pl.BlockSpec((None, 1, d), lambda bb, i: (bb, 0, 0))],
        out_specs=res,
        compiler_params=pltpu.CompilerParams(dimension_semantics=("parallel", "parallel")),
        name="attn_out_proj",
    )(m, w_bf, x, gt)


def _head_norm_rope(p, gain, cos, sin, scale):
    lane = lax.broadcasted_iota(jnp.int32, (1, HEAD_DIM), 1)
    first = (lane % (HEAD_DIM // 2)) < (HEAD_DIM // 4)
    outs = []
    for hd in range(p.shape[1] // HEAD_DIM):
        xh = p[:, hd * HEAD_DIM:(hd + 1) * HEAD_DIM]
        ms = jnp.mean(xh * xh, axis=-1, keepdims=True)
        yh = xh * lax.rsqrt(ms + EPS) * gain
        partner = jnp.where(first, pltpu.roll(yh, HEAD_DIM - HEAD_DIM // 4, 1),
                            pltpu.roll(yh, HEAD_DIM // 4, 1))
        outs.append((yh * cos + partner * sin) * scale)
    return jnp.concatenate(outs, axis=1)


def _at_proj_kernel(x_ref, g_ref, sc_ref, sh_ref, w_ref, qg_ref, kg_ref, cos_ref, sin_ref,
                    q_ref, k_ref, v_ref, sz_ref, hbuf):
    qd = q_ref.shape[-1]
    hbuf[...] = _norm_mod(x_ref[...], g_ref[...], sc_ref[...], sh_ref[...]).astype(BF16)

    def proj(lo):
        return _mm(hbuf[...], w_ref[:, lo:lo + GROUP_W])

    for lo in range(0, qd, GROUP_W):
        q_ref[:, lo:lo + GROUP_W] = _head_norm_rope(proj(lo), qg_ref[...], cos_ref[...], sin_ref[...],
                                                    HEAD_DIM ** -0.5 * LOG2E).astype(BF16)
    k_ref[...] = _head_norm_rope(proj(qd), kg_ref[...], cos_ref[...], sin_ref[...], 1.0).astype(BF16)
    v_ref[...] = proj(qd + GROUP_W).astype(BF16)
    for lo in range(0, qd, GROUP_W):
        p = proj(qd + 2 * GROUP_W + lo)
        sz_ref[:, lo:lo + GROUP_W] = (p * _sigmoid(p)).astype(BF16)


def _at_proj(x, g, sc, sh, w_bf, q_g, k_g, cos, sin, *, tm):
    b, l, d = x.shape
    kvd = GROUP_W
    qd = (w_bf.shape[1] - 2 * kvd) // 2
    const = lambda shape: pl.BlockSpec(shape, lambda bb, i: (0, 0), pipeline_mode=pl.Buffered(1))
    vec = pl.BlockSpec((None, 1, d), lambda bb, i: (bb, 0, 0))
    tab = pl.BlockSpec((tm, HEAD_DIM), lambda bb, i: (i, 0))
    wide = pl.BlockSpec((None, tm, qd), lambda bb, i: (bb, i, 0))
    narrow = pl.BlockSpec((None, tm, kvd), lambda bb, i: (bb, i, 0))
    return pl.pallas_call(
        _at_proj_kernel,
        out_shape=(jax.ShapeDtypeStruct((b, l, qd), BF16), jax.ShapeDtypeStruct((b, l, kvd), BF16),
                   jax.ShapeDtypeStruct((b, l, kvd), BF16), jax.ShapeDtypeStruct((b, l, qd), BF16)),
        grid=(b, l // tm),
        in_specs=[
            pl.BlockSpec((None, tm, d), lambda bb, i: (bb, i, 0)),
            const((1, d)), vec, vec, const(w_bf.shape),
            const((1, HEAD_DIM)), const((1, HEAD_DIM)), tab, tab,
        ],
        out_specs=(wide, narrow, narrow, wide),
        scratch_shapes=[pltpu.VMEM((tm, d), BF16)],
        compiler_params=pltpu.CompilerParams(dimension_semantics=("parallel", "parallel")),
        name="attn_in_proj",
    )(x, g.reshape(1, d), sc, sh, w_bf, q_g.reshape(1, -1), k_g.reshape(1, -1), cos, sin)


def _at_proj_kv_kernel(x_ref, g_ref, sc_ref, sh_ref, wk_ref, wv_ref, kg_ref, k_ref, v_ref):
    h = _norm_mod(x_ref[...], g_ref[...], sc_ref[...], sh_ref[...]).astype(BF16)
    p = _mm(h, wk_ref[...])
    gain = kg_ref[...]
    outs = []
    for hd in range(p.shape[1] // HEAD_DIM):
        xh = p[:, hd * HEAD_DIM:(hd + 1) * HEAD_DIM]
        ms = jnp.mean(xh * xh, axis=-1, keepdims=True)
        outs.append(xh * lax.rsqrt(ms + EPS) * gain)
    k_ref[...] = jnp.concatenate(outs, axis=1).astype(BF16)
    v_ref[...] = _mm(h, wv_ref[...]).astype(BF16)


def _at_proj_kv(x, g, sc, sh, w_bf, k_g, *, qd):
    b, c, d = x.shape
    kb = qd // GROUP_W
    vec = pl.BlockSpec((None, 1, d), lambda bb: (bb, 0, 0))
    out = pl.BlockSpec((None, c, GROUP_W), lambda bb: (bb, 0, 0))
    return pl.pallas_call(
        _at_proj_kv_kernel,
        out_shape=(jax.ShapeDtypeStruct((b, c, GROUP_W), BF16),) * 2,
        grid=(b,),
        in_specs=[
            pl.BlockSpec((None, c, d), lambda bb: (bb, 0, 0)),
            pl.BlockSpec((1, d), lambda bb: (0, 0)), vec, vec,
            pl.BlockSpec((d, GROUP_W), lambda bb: (0, kb)),
            pl.BlockSpec((d, GROUP_W), lambda bb: (0, kb + 1)),
            pl.BlockSpec((1, HEAD_DIM), lambda bb: (0, 0)),
        ],
        out_specs=(out, out),
        compiler_params=pltpu.CompilerParams(dimension_semantics=("parallel",)),
        name="attn_ctx_kv_proj",
    )(x, g.reshape(1, d), sc, sh, w_bf, w_bf, k_g.reshape(1, -1))


def _attn_kernel(q_ref, k_ref, v_ref, kc_ref, vc_ref, sz_ref, o_ref, vt_ref, ka_ref, kmax_ref, acc_ref,
                 sbuf, *, tq, tk, n_k, c, nsub):
    width = GROUP * tq

    @pl.when(pl.program_id(2) == 0)
    def _():
        ones = jnp.ones((VT_ROWS - HEAD_DIM, tk), BF16)
        def aug(rows):
            lane = lax.broadcasted_iota(jnp.int32, (rows, HEAD_DIM), 1)
            return jnp.where(lane < 2, 1.0, 0.0).astype(BF16)

        kk = jnp.zeros((1, 1), F32)
        chunks = [(t, k_ref[t * tk:(t + 1) * tk, :], v_ref[t * tk:(t + 1) * tk, :]) for t in range(n_k)]
        chunks.append((n_k, kc_ref[...], vc_ref[...]))
        for t, kt, vt in chunks:
            rows = kt.shape[0]
            vt_ref[t, 0:HEAD_DIM, 0:rows] = vt.astype(F32).T.astype(BF16)
            vt_ref[t, HEAD_DIM:, :] = ones
            ka_ref[t, 0:rows, 0:HEAD_DIM] = kt
            ka_ref[t, 0:rows, HEAD_DIM:] = aug(rows)
            kf = kt.astype(F32)
            kk = jnp.maximum(kk, jnp.max(jnp.sum(kf * kf, axis=1, keepdims=True), axis=0, keepdims=True))
        kmax_ref[...] = jnp.broadcast_to(jnp.sqrt(kk), kmax_ref.shape)

    qts, stab, gap = [], [], None
    for sb in range(nsub):
        q = q_ref[sb * tq:(sb + 1) * tq, :]
        qf = [q[:, gi * HEAD_DIM:(gi + 1) * HEAD_DIM].astype(F32).T for gi in range(GROUP)]
        qts.append(jnp.concatenate([x.astype(BF16) for x in qf], axis=1))
        qnorm = jnp.sqrt(jnp.concatenate([jnp.sum(x * x, axis=0, keepdims=True) for x in qf], axis=1))
        low = jnp.max(_mm(k_ref[0:STAB_KEYS, :], qts[sb]), axis=0, keepdims=True)
        stab.append(low)
        g = jnp.max(qnorm * kmax_ref[0:1, 0:1] - low)
        gap = g if gap is None else jnp.maximum(gap, g)
    acc_ref[...] = jnp.zeros_like(acc_ref)
    n_loop = (n_k - 2) // KV_UNROLL

    def run(scores, consume, first, chunk, ctx, carry):
        def body(i, carry):
            t = KV_UNROLL * i
            for u in range(KV_UNROLL):
                scores(chunk(t + u + 1), (u + 1) % 2)
                carry = consume(u % 2, vt_ref[t + u], carry)
            return carry

        scores(first, 0)
        carry = lax.fori_loop(0, n_loop, body, carry)
        for t in range(n_loop * KV_UNROLL, n_k):
            scores(chunk(t + 1) if t + 1 < n_k else ctx, (t + 1) % 2)
            carry = consume(t % 2, vt_ref[t], carry)
        consume(n_k % 2, vt_ref[n_k, :, 0:c], carry)

    def finish(denominator):
        for sb in range(nsub):
            acc = acc_ref[sb]
            ot = acc[0:HEAD_DIM] * (1.0 / denominator(acc))
            o = jnp.concatenate([ot[:, gi * tq:(gi + 1) * tq].T for gi in range(GROUP)], axis=1)
            rows = slice(sb * tq, (sb + 1) * tq)
            o_ref[rows, :] = (o * sz_ref[rows, :].astype(F32)).astype(BF16)

    @pl.when(gap <= STAB_GAP)
    def _():
        qas = []
        for sb in range(nsub):
            bits = lax.bitcast_convert_type(stab[sb], jnp.uint32) & jnp.uint32(0xFFFF0000)
            m_hi = lax.bitcast_convert_type(bits, F32)
            m_lo = stab[sb] - m_hi
            row = lax.broadcasted_iota(jnp.int32, (16, width), 0)
            extra = jnp.where(row == 0, -m_hi, jnp.where(row == 1, -m_lo, 0.0))
            qas.append(jnp.concatenate([qts[sb], extra.astype(BF16),
                                        jnp.zeros((HEAD_DIM - 16, width), BF16)], axis=0))

        def scores(kt, slot):
            for sb in range(nsub):
                sbuf[slot, sb, 0:kt.shape[0]] = _mm(kt, qas[sb])

        def consume(slot, vt, carry):
            for sb in range(nsub):
                p = jnp.exp2(sbuf[slot, sb, 0:vt.shape[1]])
                acc_ref[sb, HEAD_DIM:HEAD_DIM + 8] += jnp.sum(p.reshape(p.shape[0] // 8, 8, width), axis=0)
                acc_ref[sb, 0:HEAD_DIM] += _mm(vt[0:HEAD_DIM], p.astype(BF16))
            return carry

        run(scores, consume, ka_ref[0], lambda t: ka_ref[t], ka_ref[n_k, 0:c, :], 0)
        finish(lambda acc: jnp.sum(acc[HEAD_DIM:HEAD_DIM + 8], axis=0, keepdims=True))

    @pl.when(jnp.logical_not(gap <= STAB_GAP))
    def _():
        def scores(kt, slot):
            for sb in range(nsub):
                sbuf[slot, sb, 0:kt.shape[0]] = _mm(kt, qts[sb])

        def consume(slot, vt, ms):
            out = []
            for sb in range(nsub):
                s = sbuf[slot, sb, 0:vt.shape[1]]
                m_new = jnp.maximum(ms[sb], jnp.max(s, axis=0, keepdims=True))
                p = jnp.exp2(s - m_new).astype(BF16)
                acc_ref[sb] = jnp.exp2(ms[sb] - m_new) * acc_ref[sb] + _mm(vt, p)
                out.append(m_new)
            return tuple(out)

        def key_chunk(t):
            start = t * tk if isinstance(t, int) else pl.multiple_of(t * tk, tk)
            return k_ref[pl.ds(start, tk), :]

        ms = tuple(jnp.full((1, width), -jnp.inf, F32) for _ in range(nsub))
        run(scores, consume, k_ref[0:tk, :], key_chunk, kc_ref[...], ms)
        finish(lambda acc: acc[HEAD_DIM:HEAD_DIM + 1])


def _attention(q, k, v, kc, vc, sz, *, tq, tk, nsub):
    b, l, qd = q.shape
    c = kc.shape[1]
    n_kv = qd // GROUP_W
    n_k = l // tk
    assert n_k % 2 == 0 and n_k >= 2 and c <= tk
    qspec = pl.BlockSpec((None, nsub * tq, GROUP_W), lambda bb, hh, i: (bb, i, hh))
    kspec = pl.BlockSpec((None, l, HEAD_DIM), lambda bb, hh, i: (bb, 0, hh))
    cspec = pl.BlockSpec((None, c, HEAD_DIM), lambda bb, hh, i: (bb, 0, hh))
    return pl.pallas_call(
        functools.partial(_attn_kernel, tq=tq, tk=tk, n_k=n_k, c=c, nsub=nsub),
        out_shape=jax.ShapeDtypeStruct((b, l, qd), BF16),
        grid=(b, n_kv, l // (nsub * tq)),
        in_specs=[qspec, kspec, kspec, cspec, cspec, qspec],
        out_specs=qspec,
        scratch_shapes=[pltpu.VMEM((n_k + 1, VT_ROWS, tk), BF16),
                        pltpu.VMEM((n_k + 1, tk, 2 * HEAD_DIM), BF16),
                        pltpu.VMEM((8, LANES), F32),
                        pltpu.VMEM((nsub, VT_ROWS, GROUP * tq), F32),
                        pltpu.VMEM((2, nsub, tk, GROUP * tq), F32)],
        compiler_params=pltpu.CompilerParams(
            dimension_semantics=("parallel", "parallel", "arbitrary")),
        name="flash_attention",
    )(q, k, v, kc, vc, sz)


def _rope_tables(l):
    rows = l // GRID_W
    axis_dim = HEAD_DIM // 2
    row = np.repeat(np.arange(rows, dtype=np.float64), GRID_W)
    col = np.tile(np.arange(GRID_W, dtype=np.float64), rows)
    inv = 1.0 / (ROPE_THETA ** (np.arange(0, axis_dim, 2, dtype=np.float64) / axis_dim))
    ra, ca = row[:, None] * inv[None, :], col[:, None] * inv[None, :]
    cos = np.concatenate([np.cos(ra), np.cos(ra), np.cos(ca), np.cos(ca)], axis=-1)
    sin = np.concatenate([-np.sin(ra), np.sin(ra), -np.sin(ca), np.sin(ca)], axis=-1)
    return jnp.asarray(cos, F32), jnp.asarray(sin, F32)


def _pick(n, target):
    t = min(n, target)
    while n % t:
        t //= 2
    return t


class _Tiles(NamedTuple):
    proj_rows: int
    ctx_rows: int
    hy_cols: int
    out_rows: int
    ctx_out_rows: int
    filter_rows: int
    ctx_filter_rows: int
    dft_cols: int
    dft_group: int
    dft_slabs: int
    spec_cols: int
    at_out_rows: int
    at_rows: int
    q_rows: int
    q_subtiles: int
    key_rows: int


def _tiles(l, n_ctx, e):
    return _Tiles(
        proj_rows=_pick(l, 1024), ctx_rows=_pick(n_ctx, 1024), hy_cols=_pick(e, 256),
        out_rows=_pick(l, 512), ctx_out_rows=_pick(n_ctx, 512),
        filter_rows=_pick(l, 2048), ctx_filter_rows=_pick(n_ctx, 256),
        dft_cols=_pick(e, 1024), dft_group=16, dft_slabs=4, spec_cols=_pick(e, 2048),
        at_out_rows=_pick(l, 1024),
        at_rows=_pick(l, 1024), q_rows=_pick(l // 2, 256), q_subtiles=2, key_rows=_pick(l // 2, 512))


def kernel(x, c, ctx, c_ctx, norm_g, ada_w, ada_b,
           hy_w_in, hy_conv_w, hy_conv_b, hy_fw1, hy_fb1, hy_fw2, hy_fb2, hy_fw3, hy_fb3,
           hy_fw4, hy_freq, hy_d, hy_w_out,
           at_w_in, at_q_g, at_k_g, at_w_out):
    b, l, d = x.shape
    n_ctx = ctx.shape[1]
    e = hy_d.shape[-1]

    cv = jnp.concatenate([c, c_ctx[None], jnp.zeros((8 - b - 1, d), F32)], axis=0)
    mod = _modulation(cv, ada_w, ada_b)

    def lat(v):
        return v[:b, None, :]

    def cx(v):
        return jnp.broadcast_to(v[b][None, None, :], (b, 1, d))

    sh0, sc0, gt0 = jnp.split(mod[0], 3, axis=-1)
    w_in = hy_w_in[0].astype(BF16)
    w_out = hy_w_out[0].astype(BF16)
    fargs = (hy_fw1[0], hy_fb1[0], hy_fw2[0], hy_fb2[0], hy_fw3[0], hy_fb3[0], hy_fw4[0], hy_freq[0])
    t = _tiles(l, n_ctx, e)

    u_lat, g_lat = _hy_proj(x, norm_g[0], lat(sc0), lat(sh0), w_in, hy_conv_w[0], hy_conv_b[0],
                            tm=t.proj_rows, tn=t.hy_cols)
    u_ctx, g_ctx = _hy_proj(ctx, norm_g[0], cx(sc0), cx(sh0), w_in, hy_conv_w[0], hy_conv_b[0],
                            tm=t.ctx_rows, tn=t.hy_cols)
    k_lat, s_lat = _hyena_filter(l, *fargs, tr=t.filter_rows)
    k_ctx, s_ctx = _hyena_filter(n_ctx, *fargs, tr=t.ctx_filter_rows)
    y_lat = _long_conv_latent(u_lat, k_lat, s_lat, nb=t.dft_group, cb=t.dft_cols, kb=t.dft_slabs,
                              scb=t.spec_cols)
    y_ctx = _long_conv_ctx(u_ctx, k_ctx, s_ctx, cb=t.dft_cols)
    x1 = _hy_out(y_lat, u_lat, g_lat, hy_d[0], w_out, x, lat(gt0), tm=t.out_rows)
    ctx1 = _hy_out(y_ctx, u_ctx, g_ctx, hy_d[0], w_out, ctx, cx(gt0), tm=t.ctx_out_rows)

    sh1, sc1, gt1 = jnp.split(mod[1], 3, axis=-1)
    aw_in = at_w_in[0].astype(BF16)
    aw_out = at_w_out[0].astype(BF16)
    cos, sin = _rope_tables(l)
    q, k, v, sz = _at_proj(x1, norm_g[1], lat(sc1), lat(sh1), aw_in, at_q_g[0], at_k_g[0], cos, sin,
                           tm=t.at_rows)
    kc, vc = _at_proj_kv(ctx1, norm_g[1], cx(sc1), cx(sh1), aw_in, at_k_g[0], qd=q.shape[-1])
    om = _attention(q, k, v, kc, vc, sz, tq=t.q_rows, tk=t.key_rows, nsub=t.q_subtiles)
    return _at_out(om, aw_out, x1, lat(gt1), tm=t.at_out_rows)
```

```python
import functools
import math
from typing import NamedTuple

import numpy as np
import jax
import jax.numpy as jnp
from jax import lax
from jax.experimental import pallas as pl
from jax.experimental.pallas import tpu as pltpu

F32 = jnp.float32
BF16 = jnp.bfloat16
HIGHEST = lax.Precision.HIGHEST

EPS = 1e-6
HEAD_DIM = 128
GROUP = 4
GROUP_W = GROUP * HEAD_DIM
GRID_W = 64
ROPE_THETA = 10000.0
FILTER_BANDS = 16
DECAY_TARGET = 1e-2
MAX_DECAY = math.log(DECAY_TARGET) / 0.3
MIN_DECAY = math.log(DECAY_TARGET) / 1.5
HALO = 16
LANES = 128
VT_ROWS = HEAD_DIM + 16
LOG2E = math.log2(math.e)
KV_UNROLL = 4
STREAM_BUFFERS = 3
STAB_KEYS = 128
STAB_GAP = 90.0


def _sigmoid(x):
    return 1.0 / (1.0 + jnp.exp(-x))


def _mm(a, b):
    return jnp.dot(a, b, preferred_element_type=F32)


def _mm_f32(a, b):
    return jnp.dot(a, b, preferred_element_type=F32, precision=HIGHEST)


def _mod_kernel(c_ref, w_ref, b_ref, o_ref):
    cv = c_ref[...]
    o_ref[...] = _mm_f32(cv * _sigmoid(cv), w_ref[...]) + b_ref[...]


def _modulation(cv, ada_w, ada_b):
    depth, d, d3 = ada_w.shape
    rows = cv.shape[0]
    tn = d
    return pl.pallas_call(
        _mod_kernel,
        out_shape=jax.ShapeDtypeStruct((depth, rows, d3), F32),
        grid=(depth, d3 // tn),
        in_specs=[
            pl.BlockSpec((rows, d), lambda l, j: (0, 0)),
            pl.BlockSpec((None, d, tn), lambda l, j: (l, 0, j)),
            pl.BlockSpec((None, 1, tn), lambda l, j: (l, 0, j)),
        ],
        out_specs=pl.BlockSpec((None, rows, tn), lambda l, j: (l, 0, j)),
        name="modulation",
    )(cv, ada_w, ada_b.reshape(depth, 1, d3))


def _norm_mod(x, g, sc, sh):
    ms = jnp.mean(x * x, axis=-1, keepdims=True)
    return x * lax.rsqrt(ms + EPS) * (g * (1.0 + sc)) + sh


def _hy_proj_kernel(xm_ref, xp_ref, xn_ref, g_ref, sc_ref, sh_ref, w_ref, cw_ref, cb_ref,
                    u_ref, gate_ref, hbuf, *, tm, tn, n_i):
    i = pl.program_id(1)
    e = u_ref.shape[-1]
    g, sc, sh = g_ref[...], sc_ref[...], sh_ref[...]
    hbuf[HALO:HALO + tm, :] = _norm_mod(xm_ref[...], g, sc, sh).astype(BF16)
    hp = _norm_mod(xp_ref[...], g, sc, sh)
    hbuf[0:HALO, :] = jnp.where(i > 0, hp, 0.0).astype(BF16)
    hn = _norm_mod(xn_ref[...], g, sc, sh)
    hbuf[HALO + tm:, :] = jnp.where(i < n_i - 1, hn, 0.0).astype(BF16)

    def conv(lo):
        p = _mm(hbuf[...], w_ref[:, lo:lo + tn])
        cw = cw_ref[:, lo:lo + tn]
        rows = p.shape[0]
        prev = pltpu.roll(p, 1, 0)[HALO:HALO + tm]
        nxt = pltpu.roll(p, rows - 1, 0)[HALO:HALO + tm]
        return prev * cw[0:1] + p[HALO:HALO + tm] * cw[1:2] + nxt * cw[2:3] + cb_ref[:, lo:lo + tn]

    for j in range(e // tn):
        lo = j * tn
        u_ref[:, lo:lo + tn] = (conv(e + lo) * conv(2 * e + lo)).astype(BF16)
        z = _mm(hbuf[HALO:HALO + tm, :], w_ref[:, 3 * e + lo:3 * e + lo + tn])
        gate_ref[:, lo:lo + tn] = (conv(lo) * z * _sigmoid(z)).astype(BF16)


def _hy_proj(x, g, sc, sh, w_bf, conv_w, conv_b, *, tm, tn):
    b, l, d = x.shape
    e = w_bf.shape[1] // 4
    n_i = l // tm
    hb = tm // HALO
    last_hb = l // HALO - 1
    const = lambda shape: pl.BlockSpec(shape, lambda bb, i: (0, 0), pipeline_mode=pl.Buffered(1))
    vec = pl.BlockSpec((None, 1, d), lambda bb, i: (bb, 0, 0))
    out = pl.BlockSpec((None, tm, e), lambda bb, i: (bb, i, 0))
    return pl.pallas_call(
        functools.partial(_hy_proj_kernel, tm=tm, tn=tn, n_i=n_i),
        out_shape=(jax.ShapeDtypeStruct((b, l, e), BF16), jax.ShapeDtypeStruct((b, l, e), BF16)),
        grid=(b, n_i),
        in_specs=[
            pl.BlockSpec((None, tm, d), lambda bb, i: (bb, i, 0)),
            pl.BlockSpec((None, HALO, d), lambda bb, i: (bb, jnp.maximum(i * hb - 1, 0), 0)),
            pl.BlockSpec((None, HALO, d), lambda bb, i: (bb, jnp.minimum((i + 1) * hb, last_hb), 0)),
            const((1, d)), vec, vec,
            const((d, 4 * e)), const((3, 3 * e)), const((1, 3 * e)),
        ],
        out_specs=(out, out),
        scratch_shapes=[pltpu.VMEM((tm + 2 * HALO, d), BF16)],
        compiler_params=pltpu.CompilerParams(dimension_semantics=("parallel", "parallel")),
        name="hyena_in_proj",
    )(x, x, x, g.reshape(1, d), sc, sh, w_bf, conv_w, conv_b.reshape(1, -1))


def _split_bf16(a):
    hi = a.astype(BF16)
    return hi, (a - hi.astype(F32)).astype(BF16)


def _mm_split(w, a):
    w_hi, w_lo = _split_bf16(w)
    a_hi, a_lo = _split_bf16(a)
    return _mm(w_hi, a_hi) + _mm(w_lo, a_hi) + _mm(w_hi, a_lo)


def _filter_kernel(emb_ref, w1_ref, w2_ref, w3_ref, col_ref, w4_ref, dl_ref, k_ref, s_ref, *, l, tr):
    t_idx = pl.program_id(0)
    r_row = t_idx * tr + lax.broadcasted_iota(jnp.int32, (1, tr), 1)
    p_row = jnp.where(r_row < l, r_row, 2 * l - r_row).astype(F32)
    emb = emb_ref[...]
    z = (emb[:, 2:3] * (p_row / (l - 1.0))
         + emb[:, 3:4] * jnp.cos(emb[:, 0:1] * ((2.0 * math.pi) * p_row / l) + emb[:, 1:2]))
    col = col_ref[...]
    fq = col[:, 3:4]
    h = jnp.sin(fq * (_mm_split(w1_ref[...], z) + col[:, 0:1]))
    h = jnp.sin(fq * (_mm_split(w2_ref[...], h) + col[:, 1:2]))
    h = jnp.sin(fq * (_mm_split(w3_ref[...], h) + col[:, 2:3]))
    h_hi, h_lo = _split_bf16(h)
    w_hi, w_lo = _split_bf16(w4_ref[...])
    k = lax.dot_general(jnp.concatenate([h_hi, h_lo, h_hi], axis=0),
                        jnp.concatenate([w_hi, w_hi, w_lo], axis=0),
                        (((0,), (0,)), ((), ())), preferred_element_type=F32)
    r_col = t_idx * tr + lax.broadcasted_iota(jnp.int32, (tr, 1), 0)
    t_col = jnp.where(r_col < l, r_col, 2 * l - r_col).astype(F32) / (l - 1.0)
    k = jnp.where(r_col == l, 0.0, k * jnp.exp(-t_col * dl_ref[...]))
    k_ref[...] = k.astype(BF16)

    @pl.when(t_idx == 0)
    def _():
        s_ref[...] = jnp.zeros_like(s_ref)

    s_ref[...] += jnp.sum(jnp.abs(k), axis=0, keepdims=True)


EMB_ROWS = 48


def _filter_embedding_table():
    fr = np.linspace(1e-4, FILTER_BANDS - 1, FILTER_BANDS)
    emb = np.zeros((EMB_ROWS, 4), np.float32)
    emb[1:1 + FILTER_BANDS, 0] = fr
    emb[1 + FILTER_BANDS:1 + 2 * FILTER_BANDS, 0] = fr
    emb[1 + FILTER_BANDS:1 + 2 * FILTER_BANDS, 1] = math.pi / 2
    emb[0, 2] = 1.0
    emb[1:1 + 2 * FILTER_BANDS, 3] = 1.0
    return emb


def _hyena_filter(l, fw1, fb1, fw2, fb2, fw3, fb3, fw4, freq, *, tr):
    width = fw1.shape[1]
    e = fw4.shape[1] // 2
    w1t = jnp.zeros((width, EMB_ROWS), F32).at[:, :fw1.shape[0]].set(fw1.T)
    cols = jnp.stack([fb1, fb2, fb3, freq], axis=1)
    deltas = jnp.abs(jnp.linspace(MIN_DECAY, MAX_DECAY, e, dtype=F32)).reshape(1, e)
    half = l // tr
    small = lambda shape: pl.BlockSpec(shape, lambda t: (0, 0))
    return pl.pallas_call(
        functools.partial(_filter_kernel, l=l, tr=tr),
        out_shape=(jax.ShapeDtypeStruct((2 * l, e), BF16), jax.ShapeDtypeStruct((1, e), F32)),
        grid=(2 * half,),
        in_specs=[
            small((EMB_ROWS, 4)), small((width, EMB_ROWS)), small((width, width)), small((width, width)),
            small((width, 4)),
            pl.BlockSpec((width, e), lambda t: (0, t // half)),
            small((1, e)),
        ],
        out_specs=(pl.BlockSpec((tr, e), lambda t: (t, 0)), pl.BlockSpec((1, e), lambda t: (0, 0))),
        compiler_params=pltpu.CompilerParams(dimension_semantics=("arbitrary",)),
        name="hyena_filter",
    )(jnp.asarray(_filter_embedding_table()), w1t, fw2.T, fw3.T, cols, fw4, deltas)


@functools.lru_cache(maxsize=None)
def _four_step_consts(n1, n2):
    n = n1 * n2
    hh = n1 // 2
    k1 = np.arange(n1)
    c2 = np.arange(n2)

    def angles(sign, n1_count):
        nn = n2 * np.arange(n1_count)[None, None, :] + c2[:, None, None]
        return sign * 2.0 * np.pi * ((k1[None, :, None] * nn) % n) / n

    a = angles(-1.0, hh)
    c, s = np.cos(a), np.sin(a)
    g = np.zeros((n2, n1, 2, 2, hh))
    g[:, :, 0, 0], g[:, :, 0, 1], g[:, :, 1, 0], g[:, :, 1, 1] = c, -s, s, c
    g = g.reshape(n2, 2 * n1, n1)

    a = angles(-1.0, n1)
    gk = np.stack([np.cos(a), np.sin(a)], axis=2).reshape(n2, 2 * n1, n1)

    a = np.transpose(angles(1.0, hh), (0, 2, 1))
    c, s = np.cos(a) / n, np.sin(a) / n
    h = np.zeros((n2, 2, hh, n1, 2))
    h[:, 0, :, :, 0], h[:, 0, :, :, 1], h[:, 1, :, :, 0], h[:, 1, :, :, 1] = c, -s, s, c
    h = h.reshape(n2, n1, 2 * n1)

    phi = -2.0 * np.pi * ((c2[:, None] * c2[None, :]) % n2) / n2
    c, s = np.cos(phi), np.sin(phi)
    fb = np.block([[c, -s], [s, c]])
    fbi = np.block([[c, s], [-s, c]])
    f = lambda x: np.asarray(x, np.float32)
    return f(g), f(gk), f(h), f(fb), f(fbi)


@functools.lru_cache(maxsize=None)
def _direct_dft_consts(nc):
    hh = nc // 2
    k = np.arange(nc)
    a = -2.0 * np.pi * ((k[:, None] * np.arange(hh)[None, :]) % nc) / nc
    c, s = np.cos(a), np.sin(a)
    fc = np.block([[c, -s], [s, c]])
    a = -2.0 * np.pi * ((k[:, None] * k[None, :]) % nc) / nc
    fck = np.concatenate([np.cos(a), np.sin(a)], axis=0)
    a = 2.0 * np.pi * ((np.arange(hh)[:, None] * k[None, :]) % nc) / nc
    c, s = np.cos(a) / nc, np.sin(a) / nc
    fci = np.block([[c, -s], [s, c]])
    f = lambda x: np.asarray(x, np.float32)
    return f(fc), f(fck), f(fci)


def _colmm_kernel(g_ref, x_ref, o_ref, xs_ref, rs_ref, *, nb):
    xs_ref[...] = pltpu.einshape("abc->bac", x_ref[...])
    for q in range(nb):
        rs_ref[q] = _mm(g_ref[q].astype(BF16), xs_ref[q]).astype(rs_ref.dtype)
    o_ref[...] = pltpu.einshape("abc->bac", rs_ref[...])


def _colmm(gm, x3, *, nb, cb, name):
    n2, r_out, r_in = gm.shape
    e = x3.shape[-1]
    return pl.pallas_call(
        functools.partial(_colmm_kernel, nb=nb),
        out_shape=jax.ShapeDtypeStruct((r_out, n2, e), BF16),
        grid=(n2 // nb, e // cb),
        in_specs=[
            pl.BlockSpec((nb, r_out, r_in), lambda s, j: (s, 0, 0)),
            pl.BlockSpec((r_in, nb, cb), lambda s, j: (0, s, j)),
        ],
        out_specs=pl.BlockSpec((r_out, nb, cb), lambda s, j: (0, s, j)),
        scratch_shapes=[pltpu.VMEM((nb, r_in, cb), BF16), pltpu.VMEM((nb, r_out, cb), BF16)],
        compiler_params=pltpu.CompilerParams(dimension_semantics=("parallel", "parallel")),
        name=name,
    )(gm, x3)


def _cmul(x, k, half):
    xr, xi = x[:half], x[half:]
    kr, ki = k[:half], k[half:]
    return jnp.concatenate([xr * kr - xi * ki, xr * ki + xi * kr], axis=0)


def _spectral_mul_kernel(fb_ref, fbi_ref, a_ref, ak_ref, s_ref, z_ref, *, kb):
    fb, fbi = fb_ref[...].astype(BF16), fbi_ref[...].astype(BF16)
    half = fb.shape[0] // 2
    inv = 1.0 / s_ref[...]
    xs = [_mm(fb, a_ref[q]) for q in range(kb)]
    ks = [_mm(fb, ak_ref[q]) * inv for q in range(kb)]
    ys = [_cmul(xs[q], ks[q], half).astype(BF16) for q in range(kb)]
    for q in range(kb):
        z_ref[q] = _mm(fbi, ys[q]).astype(BF16)


def _spectral_mul(fb, fbi, a3, ak3, s, *, kb, cb):
    n1, r, e = a3.shape
    slab = pl.BlockSpec((kb, r, cb), lambda i, j: (i, 0, j))
    mat = pl.BlockSpec((r, r), lambda i, j: (0, 0))
    return pl.pallas_call(
        functools.partial(_spectral_mul_kernel, kb=kb),
        out_shape=jax.ShapeDtypeStruct((n1, r, e), BF16),
        grid=(n1 // kb, e // cb),
        in_specs=[mat, mat, slab, slab, pl.BlockSpec((1, cb), lambda i, j: (0, j))],
        out_specs=slab,
        compiler_params=pltpu.CompilerParams(dimension_semantics=("parallel", "parallel")),
        name="hyena_spectral_mul",
    )(fb, fbi, a3, ak3, s)


def _ctx_conv_kernel(fc_ref, fck_ref, fci_ref, u_ref, k_ref, s_ref, y_ref):
    x = _mm(fc_ref[...].astype(BF16), u_ref[...])
    ks = _mm(fck_ref[...].astype(BF16), k_ref[...]) * (1.0 / s_ref[...])
    y = _cmul(x, ks, x.shape[0] // 2).astype(BF16)
    y_ref[...] = _mm(fci_ref[...].astype(BF16), y).astype(BF16)


def _ctx_conv(fc, fck, fci, u2, k2, s, *, cb):
    nc, e = u2.shape
    full = lambda a: pl.BlockSpec(a.shape, lambda j: (0, 0))
    col = pl.BlockSpec((nc, cb), lambda j: (0, j))
    return pl.pallas_call(
        _ctx_conv_kernel,
        out_shape=jax.ShapeDtypeStruct((nc, e), BF16),
        grid=(e // cb,),
        in_specs=[full(fc), full(fck), full(fci), col, col, pl.BlockSpec((1, cb), lambda j: (0, j))],
        out_specs=col,
        compiler_params=pltpu.CompilerParams(dimension_semantics=("parallel",)),
        name="hyena_ctx_conv",
    )(fc, fck, fci, u2, k2, s)


def _long_conv_latent(u, k_raw, s, *, nb, cb, kb, scb):
    b, l, e = u.shape
    n = 2 * l
    n1 = 1 << ((n.bit_length() - 1 + 1) // 2)
    n2 = n // n1
    g, gk, h, fb, fbi = (jnp.asarray(m) for m in _four_step_consts(n1, n2))
    a = _colmm(g, u.reshape(n1, n2, e), nb=nb, cb=cb, name="hyena_dft_rows")
    ak = _colmm(gk, k_raw.reshape(n1, n2, e), nb=nb, cb=cb, name="hyena_filter_dft_rows")
    z = _spectral_mul(fb, fbi, a.reshape(n1, 2 * n2, e), ak.reshape(n1, 2 * n2, e), s, kb=kb, cb=scb)
    y = _colmm(h, z.reshape(2 * n1, n2, e), nb=nb, cb=cb, name="hyena_idft_rows")
    return y.reshape(b, l, e)


def _long_conv_ctx(u, k_raw, s, *, cb):
    b, c, e = u.shape
    fc, fck, fci = (jnp.asarray(m) for m in _direct_dft_consts(2 * c))
    return _ctx_conv(fc, fck, fci, u.reshape(b * c, e), k_raw, s, cb=cb).reshape(b, c, e)


def _hy_out(y, u, g, d_skip, w_bf, x, gt, *, tm):
    b, l, e = y.shape
    d = x.shape[-1]
    n_i = l // tm
    deep = pl.Buffered(min(STREAM_BUFFERS, max(2, b * n_i)))
    act = pl.BlockSpec((tm, e), lambda s: (s, 0), pipeline_mode=deep)
    res_in = pl.BlockSpec((tm, d), lambda s: (s, 0), pipeline_mode=deep)
    res = pl.BlockSpec((tm, d), lambda s: (s, 0))
    gate = pl.BlockSpec((1, 1, d), lambda s: (s // n_i, 0, 0))

    def outer(y_hbm, u_hbm, g_hbm, d_ref, w_ref, x_hbm, gt_hbm, o_hbm):
        def step(y_ref, u_ref, g_ref, x_ref, gt_ref, o_ref):
            m = (y_ref[...].astype(F32) + u_ref[...].astype(F32) * d_ref[...]) * g_ref[...].astype(F32)
            o_ref[...] = x_ref[...] + gt_ref[0] * _mm(m.astype(BF16), w_ref[...])

        pltpu.emit_pipeline(step, grid=(b * n_i,), in_specs=[act, act, act, res_in, gate],
                            out_specs=[res])(y_hbm, u_hbm, g_hbm, x_hbm, gt_hbm, o_hbm)

    hbm = pl.BlockSpec(memory_space=pl.ANY)
    vmem = pl.BlockSpec(memory_space=pltpu.VMEM)
    out = pl.pallas_call(
        outer,
        out_shape=jax.ShapeDtypeStruct((b * l, d), F32),
        in_specs=[hbm, hbm, hbm, vmem, vmem, hbm, hbm],
        out_specs=hbm,
        name="hyena_out_proj",
    )(y.reshape(b * l, e), u.reshape(b * l, e), g.reshape(b * l, e), d_skip.reshape(1, e), w_bf,
      x.reshape(b * l, d), gt)
    return out.reshape(x.shape)


def _at_out_kernel(m_ref, w_ref, x_ref, gt_ref, o_ref):
    o_ref[...] = x_ref[...] + gt_ref[...] * _mm(m_ref[...], w_ref[...])


def _at_out(m, w_bf, x, gt, *, tm):
    b, l, e = m.shape
    d = x.shape[-1]
    res = pl.BlockSpec((None, tm, d), lambda bb, i: (bb, i, 0))
    return pl.pallas_call(
        _at_out_kernel,
        out_shape=jax.ShapeDtypeStruct(x.shape, F32),
        grid=(b, l // tm),
        in_specs=[pl.BlockSpec((None, tm, e), lambda bb, i: (bb, i, 0)),
                  pl.BlockSpec((e, d), lambda bb, i: (0, 0)), res,
                  pl.BlockSpec((None, 1, d), lambda bb, i: (bb, 0, 0))],
        out_specs=res,
        compiler_params=pltpu.CompilerParams(dimension_semantics=("parallel", "parallel")),
        name="attn_out_proj",
    )(m, w_bf, x, gt)


def _head_norm_rope(p, gain, cos, sin, scale):
    lane = lax.broadcasted_iota(jnp.int32, (1, HEAD_DIM), 1)
    first = (lane % (HEAD_DIM // 2)) < (HEAD_DIM // 4)
    outs = []
    for hd in range(p.shape[1] // HEAD_DIM):
        xh = p[:, hd * HEAD_DIM:(hd + 1) * HEAD_DIM]
        ms = jnp.mean(xh * xh, axis=-1, keepdims=True)
        yh = xh * lax.rsqrt(ms + EPS) * gain
        partner = jnp.where(first, pltpu.roll(yh, HEAD_DIM - HEAD_DIM // 4, 1),
                            pltpu.roll(yh, HEAD_DIM // 4, 1))
        outs.append((yh * cos + partner * sin) * scale)
    return jnp.concatenate(outs, axis=1)


def _at_proj_kernel(x_ref, g_ref, sc_ref, sh_ref, w_ref, qg_ref, kg_ref, cos_ref, sin_ref,
                    q_ref, k_ref, v_ref, sz_ref, hbuf):
    qd = q_ref.shape[-1]
    hbuf[...] = _norm_mod(x_ref[...], g_ref[...], sc_ref[...], sh_ref[...]).astype(BF16)

    def proj(lo):
        return _mm(hbuf[...], w_ref[:, lo:lo + GROUP_W])

    for lo in range(0, qd, GROUP_W):
        q_ref[:, lo:lo + GROUP_W] = _head_norm_rope(proj(lo), qg_ref[...], cos_ref[...], sin_ref[...],
                                                    HEAD_DIM ** -0.5 * LOG2E).astype(BF16)
    k_ref[...] = _head_norm_rope(proj(qd), kg_ref[...], cos_ref[...], sin_ref[...], 1.0).astype(BF16)
    v_ref[...] = proj(qd + GROUP_W).astype(BF16)
    for lo in range(0, qd, GROUP_W):
        p = proj(qd + 2 * GROUP_W + lo)
        sz_ref[:, lo:lo + GROUP_W] = (p * _sigmoid(p)).astype(BF16)


def _at_proj(x, g, sc, sh, w_bf, q_g, k_g, cos, sin, *, tm):
    b, l, d = x.shape
    kvd = GROUP_W
    qd = (w_bf.shape[1] - 2 * kvd) // 2
    const = lambda shape: pl.BlockSpec(shape, lambda bb, i: (0, 0), pipeline_mode=pl.Buffered(1))
    vec = pl.BlockSpec((None, 1, d), lambda bb, i: (bb, 0, 0))
    tab = pl.BlockSpec((tm, HEAD_DIM), lambda bb, i: (i, 0))
    wide = pl.BlockSpec((None, tm, qd), lambda bb, i: (bb, i, 0))
    narrow = pl.BlockSpec((None, tm, kvd), lambda bb, i: (bb, i, 0))
    return pl.pallas_call(
        _at_proj_kernel,
        out_shape=(jax.ShapeDtypeStruct((b, l, qd), BF16), jax.ShapeDtypeStruct((b, l, kvd), BF16),
                   jax.ShapeDtypeStruct((b, l, kvd), BF16), jax.ShapeDtypeStruct((b, l, qd), BF16)),
        grid=(b, l // tm),
        in_specs=[
            pl.BlockSpec((None, tm, d), lambda bb, i: (bb, i, 0)),
            const((1, d)), vec, vec, const(w_bf.shape),
            const((1, HEAD_DIM)), const((1, HEAD_DIM)), tab, tab,
        ],
        out_specs=(wide, narrow, narrow, wide),
        scratch_shapes=[pltpu.VMEM((tm, d), BF16)],
        compiler_params=pltpu.CompilerParams(dimension_semantics=("parallel", "parallel")),
        name="attn_in_proj",
    )(x, g.reshape(1, d), sc, sh, w_bf, q_g.reshape(1, -1), k_g.reshape(1, -1), cos, sin)


def _at_proj_kv_kernel(x_ref, g_ref, sc_ref, sh_ref, wk_ref, wv_ref, kg_ref, k_ref, v_ref):
    h = _norm_mod(x_ref[...], g_ref[...], sc_ref[...], sh_ref[...]).astype(BF16)
    p = _mm(h, wk_ref[...])
    gain = kg_ref[...]
    outs = []
    for hd in range(p.shape[1] // HEAD_DIM):
        xh = p[:, hd * HEAD_DIM:(hd + 1) * HEAD_DIM]
        ms = jnp.mean(xh * xh, axis=-1, keepdims=True)
        outs.append(xh * lax.rsqrt(ms + EPS) * gain)
    k_ref[...] = jnp.concatenate(outs, axis=1).astype(BF16)
    v_ref[...] = _mm(h, wv_ref[...]).astype(BF16)


def _at_proj_kv(x, g, sc, sh, w_bf, k_g, *, qd):
    b, c, d = x.shape
    kb = qd // GROUP_W
    vec = pl.BlockSpec((None, 1, d), lambda bb: (bb, 0, 0))
    out = pl.BlockSpec((None, c, GROUP_W), lambda bb: (bb, 0, 0))
    return pl.pallas_call(
        _at_proj_kv_kernel,
        out_shape=(jax.ShapeDtypeStruct((b, c, GROUP_W), BF16),) * 2,
        grid=(b,),
        in_specs=[
            pl.BlockSpec((None, c, d), lambda bb: (bb, 0, 0)),
            pl.BlockSpec((1, d), lambda bb: (0, 0)), vec, vec,
            pl.BlockSpec((d, GROUP_W), lambda bb: (0, kb)),
            pl.BlockSpec((d, GROUP_W), lambda bb: (0, kb + 1)),
            pl.BlockSpec((1, HEAD_DIM), lambda bb: (0, 0)),
        ],
        out_specs=(out, out),
        compiler_params=pltpu.CompilerParams(dimension_semantics=("parallel",)),
        name="attn_ctx_kv_proj",
    )(x, g.reshape(1, d), sc, sh, w_bf, w_bf, k_g.reshape(1, -1))


def _attn_kernel(q_ref, k_ref, v_ref, kc_ref, vc_ref, sz_ref, o_ref, vt_ref, ka_ref, kmax_ref, acc_ref,
                 sbuf, *, tq, tk, n_k, c, nsub):
    width = GROUP * tq

    @pl.when(pl.program_id(2) == 0)
    def _():
        ones = jnp.ones((VT_ROWS - HEAD_DIM, tk), BF16)
        def aug(rows):
            lane = lax.broadcasted_iota(jnp.int32, (rows, HEAD_DIM), 1)
            return jnp.where(lane < 2, 1.0, 0.0).astype(BF16)

        kk = jnp.zeros((1, 1), F32)
        chunks = [(t, k_ref[t * tk:(t + 1) * tk, :], v_ref[t * tk:(t + 1) * tk, :]) for t in range(n_k)]
        chunks.append((n_k, kc_ref[...], vc_ref[...]))
        for t, kt, vt in chunks:
            rows = kt.shape[0]
            vt_ref[t, 0:HEAD_DIM, 0:rows] = vt.astype(F32).T.astype(BF16)
            vt_ref[t, HEAD_DIM:, :] = ones
            ka_ref[t, 0:rows, 0:HEAD_DIM] = kt
            ka_ref[t, 0:rows, HEAD_DIM:] = aug(rows)
            kf = kt.astype(F32)
            kk = jnp.maximum(kk, jnp.max(jnp.sum(kf * kf, axis=1, keepdims=True), axis=0, keepdims=True))
        kmax_ref[...] = jnp.broadcast_to(jnp.sqrt(kk), kmax_ref.shape)

    qts, stab, gap = [], [], None
    for sb in range(nsub):
        q = q_ref[sb * tq:(sb + 1) * tq, :]
        qf = [q[:, gi * HEAD_DIM:(gi + 1) * HEAD_DIM].astype(F32).T for gi in range(GROUP)]
        qts.append(jnp.concatenate([x.astype(BF16) for x in qf], axis=1))
        qnorm = jnp.sqrt(jnp.concatenate([jnp.sum(x * x, axis=0, keepdims=True) for x in qf], axis=1))
        low = jnp.max(_mm(k_ref[0:STAB_KEYS, :], qts[sb]), axis=0, keepdims=True)
        stab.append(low)
        g = jnp.max(qnorm * kmax_ref[0:1, 0:1] - low)
        gap = g if gap is None else jnp.maximum(gap, g)
    acc_ref[...] = jnp.zeros_like(acc_ref)
    n_loop = (n_k - 2) // KV_UNROLL

    def run(scores, consume, first, chunk, ctx, carry):
        def body(i, carry):
            t = KV_UNROLL * i
            for u in range(KV_UNROLL):
                scores(chunk(t + u + 1), (u + 1) % 2)
                carry = consume(u % 2, vt_ref[t + u], carry)
            return carry

        scores(first, 0)
        carry = lax.fori_loop(0, n_loop, body, carry)
        for t in range(n_loop * KV_UNROLL, n_k):
            scores(chunk(t + 1) if t + 1 < n_k else ctx, (t + 1) % 2)
            carry = consume(t % 2, vt_ref[t], carry)
        consume(n_k % 2, vt_ref[n_k, :, 0:c], carry)

    def finish(denominator):
        for sb in range(nsub):
            acc = acc_ref[sb]
            ot = acc[0:HEAD_DIM] * (1.0 / denominator(acc))
            o = jnp.concatenate([ot[:, gi * tq:(gi + 1) * tq].T for gi in range(GROUP)], axis=1)
            rows = slice(sb * tq, (sb + 1) * tq)
            o_ref[rows, :] = (o * sz_ref[rows, :].astype(F32)).astype(BF16)

    @pl.when(gap <= STAB_GAP)
    def _():
        qas = []
        for sb in range(nsub):
            bits = lax.bitcast_convert_type(stab[sb], jnp.uint32) & jnp.uint32(0xFFFF0000)
            m_hi = lax.bitcast_convert_type(bits, F32)
            m_lo = stab[sb] - m_hi
            row = lax.broadcasted_iota(jnp.int32, (16, width), 0)
            extra = jnp.where(row == 0, -m_hi, jnp.where(row == 1, -m_lo, 0.0))
            qas.append(jnp.concatenate([qts[sb], extra.astype(BF16),
                                        jnp.zeros((HEAD_DIM - 16, width), BF16)], axis=0))

        def scores(kt, slot):
            for sb in range(nsub):
                sbuf[slot, sb, 0:kt.shape[0]] = _mm(kt, qas[sb])

        def consume(slot, vt, carry):
            for sb in range(nsub):
                p = jnp.exp2(sbuf[slot, sb, 0:vt.shape[1]])
                acc_ref[sb, HEAD_DIM:HEAD_DIM + 8] += jnp.sum(p.reshape(p.shape[0] // 8, 8, width), axis=0)
                acc_ref[sb, 0:HEAD_DIM] += _mm(vt[0:HEAD_DIM], p.astype(BF16))
            return carry

        run(scores, consume, ka_ref[0], lambda t: ka_ref[t], ka_ref[n_k, 0:c, :], 0)
        finish(lambda acc: jnp.sum(acc[HEAD_DIM:HEAD_DIM + 8], axis=0, keepdims=True))

    @pl.when(jnp.logical_not(gap <= STAB_GAP))
    def _():
        def scores(kt, slot):
            for sb in range(nsub):
                sbuf[slot, sb, 0:kt.shape[0]] = _mm(kt, qts[sb])

        def consume(slot, vt, ms):
            out = []
            for sb in range(nsub):
                s = sbuf[slot, sb, 0:vt.shape[1]]
                m_new = jnp.maximum(ms[sb], jnp.max(s, axis=0, keepdims=True))
                p = jnp.exp2(s - m_new).astype(BF16)
                acc_ref[sb] = jnp.exp2(ms[sb] - m_new) * acc_ref[sb] + _mm(vt, p)
                out.append(m_new)
            return tuple(out)

        def key_chunk(t):
            start = t * tk if isinstance(t, int) else pl.multiple_of(t * tk, tk)
            return k_ref[pl.ds(start, tk), :]

        ms = tuple(jnp.full((1, width), -jnp.inf, F32) for _ in range(nsub))
        run(scores, consume, k_ref[0:tk, :], key_chunk, kc_ref[...], ms)
        finish(lambda acc: acc[HEAD_DIM:HEAD_DIM + 1])


def _attention(q, k, v, kc, vc, sz, *, tq, tk, nsub):
    b, l, qd = q.shape
    c = kc.shape[1]
    n_kv = qd // GROUP_W
    n_k = l // tk
    assert n_k % 2 == 0 and n_k >= 2 and c <= tk
    qspec = pl.BlockSpec((None, nsub * tq, GROUP_W), lambda bb, hh, i: (bb, i, hh))
    kspec = pl.BlockSpec((None, l, HEAD_DIM), lambda bb, hh, i: (bb, 0, hh))
    cspec = pl.BlockSpec((None, c, HEAD_DIM), lambda bb, hh, i: (bb, 0, hh))
    return pl.pallas_call(
        functools.partial(_attn_kernel, tq=tq, tk=tk, n_k=n_k, c=c, nsub=nsub),
        out_shape=jax.ShapeDtypeStruct((b, l, qd), BF16),
        grid=(b, n_kv, l // (nsub * tq)),
        in_specs=[qspec, kspec, kspec, cspec, cspec, qspec],
        out_specs=qspec,
        scratch_shapes=[pltpu.VMEM((n_k + 1, VT_ROWS, tk), BF16),
                        pltpu.VMEM((n_k + 1, tk, 2 * HEAD_DIM), BF16),
                        pltpu.VMEM((8, LANES), F32),
                        pltpu.VMEM((nsub, VT_ROWS, GROUP * tq), F32),
                        pltpu.VMEM((2, nsub, tk, GROUP * tq), F32)],
        compiler_params=pltpu.CompilerParams(
            dimension_semantics=("parallel", "parallel", "arbitrary")),
        name="flash_attention",
    )(q, k, v, kc, vc, sz)


def _rope_tables(l):
    rows = l // GRID_W
    axis_dim = HEAD_DIM // 2
    row = np.repeat(np.arange(rows, dtype=np.float64), GRID_W)
    col = np.tile(np.arange(GRID_W, dtype=np.float64), rows)
    inv = 1.0 / (ROPE_THETA ** (np.arange(0, axis_dim, 2, dtype=np.float64) / axis_dim))
    ra, ca = row[:, None] * inv[None, :], col[:, None] * inv[None, :]
    cos = np.concatenate([np.cos(ra), np.cos(ra), np.cos(ca), np.cos(ca)], axis=-1)
    sin = np.concatenate([-np.sin(ra), np.sin(ra), -np.sin(ca), np.sin(ca)], axis=-1)
    return jnp.asarray(cos, F32), jnp.asarray(sin, F32)


def _pick(n, target):
    t = min(n, target)
    while n % t:
        t //= 2
    return t


class _Tiles(NamedTuple):
    proj_rows: int
    ctx_rows: int
    hy_cols: int
    out_rows: int
    ctx_out_rows: int
    filter_rows: int
    ctx_filter_rows: int
    dft_cols: int
    dft_group: int
    dft_slabs: int
    spec_cols: int
    at_out_rows: int
    at_rows: int
    q_rows: int
    q_subtiles: int
    key_rows: int


def _tiles(l, n_ctx, e):
    return _Tiles(
        proj_rows=_pick(l, 1024), ctx_rows=_pick(n_ctx, 1024), hy_cols=_pick(e, 256),
        out_rows=_pick(l, 512), ctx_out_rows=_pick(n_ctx, 512),
        filter_rows=_pick(l, 2048), ctx_filter_rows=_pick(n_ctx, 256),
        dft_cols=_pick(e, 1024), dft_group=16, dft_slabs=4, spec_cols=_pick(e, 2048),
        at_out_rows=_pick(l, 1024),
        at_rows=_pick(l, 1024), q_rows=_pick(l // 2, 256), q_subtiles=2, key_rows=_pick(l // 2, 512))


def kernel(x, c, ctx, c_ctx, norm_g, ada_w, ada_b,
           hy_w_in, hy_conv_w, hy_conv_b, hy_fw1, hy_fb1, hy_fw2, hy_fb2, hy_fw3, hy_fb3,
           hy_fw4, hy_freq, hy_d, hy_w_out,
           at_w_in, at_q_g, at_k_g, at_w_out):
    b, l, d = x.shape
    n_ctx = ctx.shape[1]
    e = hy_d.shape[-1]

    cv = jnp.concatenate([c, c_ctx[None], jnp.zeros((8 - b - 1, d), F32)], axis=0)
    mod = _modulation(cv, ada_w, ada_b)

    def lat(v):
        return v[:b, None, :]

    def cx(v):
        return jnp.broadcast_to(v[b][None, None, :], (b, 1, d))

    sh0, sc0, gt0 = jnp.split(mod[0], 3, axis=-1)
    w_in = hy_w_in[0].astype(BF16)
    w_out = hy_w_out[0].astype(BF16)
    fargs = (hy_fw1[0], hy_fb1[0], hy_fw2[0], hy_fb2[0], hy_fw3[0], hy_fb3[0], hy_fw4[0], hy_freq[0])
    t = _tiles(l, n_ctx, e)

    u_lat, g_lat = _hy_proj(x, norm_g[0], lat(sc0), lat(sh0), w_in, hy_conv_w[0], hy_conv_b[0],
                            tm=t.proj_rows, tn=t.hy_cols)
    u_ctx, g_ctx = _hy_proj(ctx, norm_g[0], cx(sc0), cx(sh0), w_in, hy_conv_w[0], hy_conv_b[0],
                            tm=t.ctx_rows, tn=t.hy_cols)
    k_lat, s_lat = _hyena_filter(l, *fargs, tr=t.filter_rows)
    k_ctx, s_ctx = _hyena_filter(n_ctx, *fargs, tr=t.ctx_filter_rows)
    y_lat = _long_conv_latent(u_lat, k_lat, s_lat, nb=t.dft_group, cb=t.dft_cols, kb=t.dft_slabs,
                              scb=t.spec_cols)
    y_ctx = _long_conv_ctx(u_ctx, k_ctx, s_ctx, cb=t.dft_cols)
    x1 = _hy_out(y_lat, u_lat, g_lat, hy_d[0], w_out, x, lat(gt0), tm=t.out_rows)
    ctx1 = _hy_out(y_ctx, u_ctx, g_ctx, hy_d[0], w_out, ctx, cx(gt0), tm=t.ctx_out_rows)

    sh1, sc1, gt1 = jnp.split(mod[1], 3, axis=-1)
    aw_in = at_w_in[0].astype(BF16)
    aw_out = at_w_out[0].astype(BF16)
    cos, sin = _rope_tables(l)
    q, k, v, sz = _at_proj(x1, norm_g[1], lat(sc1), lat(sh1), aw_in, at_q_g[0], at_k_g[0], cos, sin,
                           tm=t.at_rows)
    kc, vc = _at_proj_kv(ctx1, norm_g[1], cx(sc1), cx(sh1), aw_in, at_k_g[0], qd=q.shape[-1])
    om = _attention(q, k, v, kc, vc, sz, tq=t.q_rows, tk=t.key_rows, nsub=t.q_subtiles)
    return _at_out(om, aw_out, x1, lat(gt1), tm=t.at_out_rows)
```
